```python
import math
import jax, jax.numpy as jnp
from jax import lax
import numpy as np

D_MODEL = 1024
BATCH = 4
SEQ = 8192
DEPTH = 4

MIX_WIDTH = 2 * D_MODEL
N_GROUPS = 4
GROUP_WIDTH = MIX_WIDTH // N_GROUPS
N_IN_SPLITS = 10
IN_WIDTH = N_IN_SPLITS * GROUP_WIDTH
EPS = 1e-6
S5_CH = 16
S5_GROUPS = GROUP_WIDTH // S5_CH
S5_STATE = 64
S5_DT_MIN = 1e-3
S5_DT_MAX = 1e-1
DA_HEADS = 4
DA_HEAD_DIM = GROUP_WIDTH // DA_HEADS // 2
DA_V_DIM = 2 * DA_HEAD_DIM
ROPE_THETA = 500000.0
ROPE_DIMS = DA_HEAD_DIM // 4
Q_BLOCK = 128
LRU_BLOCKS = 8
LRU_BLOCK_DIM = GROUP_WIDTH // LRU_BLOCKS
LRU_C = 8.0
CONV_WIDTH = 4
MEM_TOKENS = 256
MEM_HEADS = 4
MEM_HEAD_DIM = GROUP_WIDTH // MEM_HEADS

kernel_name = "hymba_s5_diffattn_rglru_memory"

F32 = jnp.float32


def rms_norm(x, g):
    xf = x.astype(F32)
    y = xf * lax.rsqrt(jnp.mean(xf * xf, axis=-1, keepdims=True) + EPS)
    return (y * g.astype(F32)).astype(x.dtype)


def rope_partial(x, cos, sin):
    xf = x.astype(F32)
    half = ROPE_DIMS // 2
    x1, x2, xp = xf[..., :half], xf[..., half:ROPE_DIMS], xf[..., ROPE_DIMS:]
    out = jnp.concatenate([x1 * cos - x2 * sin, x2 * cos + x1 * sin, xp], axis=-1)
    return out.astype(x.dtype)


def s5_mixer(u, lam_re, lam_im, log_dt, b_re, b_im, c_re, c_im, d_skip, w_glu):
    B, L, _ = u.shape
    uf = u.astype(F32).reshape(B, L, S5_GROUPS, S5_CH)
    dt = jnp.exp(log_dt.astype(F32))[:, None]
    lr, li = lam_re.astype(F32), lam_im.astype(F32)
    mag = jnp.exp(lr * dt)
    abar_re = mag * jnp.cos(li * dt)
    abar_im = mag * jnp.sin(li * dt)
    den = lr * lr + li * li
    nr, ni = abar_re - 1.0, abar_im
    f_re = (nr * lr + ni * li) / den
    f_im = (ni * lr - nr * li) / den
    br, bi = b_re.astype(F32), b_im.astype(F32)
    bb_re = f_re[..., None] * br - f_im[..., None] * bi
    bb_im = f_re[..., None] * bi + f_im[..., None] * br
    bu_re = jnp.einsum('blgh,gph->blgp', uf, bb_re)
    bu_im = jnp.einsum('blgh,gph->blgp', uf, bb_im)
    a_re = jnp.broadcast_to(abar_re, bu_re.shape)
    a_im = jnp.broadcast_to(abar_im, bu_im.shape)

    def combine(e1, e2):
        a1r, a1i, b1r, b1i = e1
        a2r, a2i, b2r, b2i = e2
        return (a2r * a1r - a2i * a1i,
                a2r * a1i + a2i * a1r,
                a2r * b1r - a2i * b1i + b2r,
                a2r * b1i + a2i * b1r + b2i)

    _, _, xr, xi = lax.associative_scan(combine, (a_re, a_im, bu_re, bu_im), axis=1)
    y = (jnp.einsum('blgp,ghp->blgh', xr, c_re.astype(F32))
         - jnp.einsum('blgp,ghp->blgh', xi, c_im.astype(F32))
         + d_skip.astype(F32) * uf)
    y = jax.nn.gelu(y.reshape(B, L, GROUP_WIDTH)).astype(u.dtype)
    ga, gb = jnp.split(y @ w_glu, 2, axis=-1)
    return ga * jax.nn.sigmoid(gb)


def diff_attention(q, k, v, lam):
    B, L = q.shape[:2]
    nb = L // Q_BLOCK
    scale = DA_HEAD_DIM ** -0.5
    k1, k2 = k[..., 0, :], k[..., 1, :]
    qb = (q * scale).reshape(B, nb, Q_BLOCK, DA_HEADS, 2, DA_HEAD_DIM)
    qb = jnp.moveaxis(qb, 1, 0)
    kpos = jnp.arange(L)

    def block(args):
        qblk, i = args
        qpos = i * Q_BLOCK + jnp.arange(Q_BLOCK)
        mask = kpos[None, :] <= qpos[:, None]
        s1 = jnp.einsum('bqhd,bkhd->bhqk', qblk[..., 0, :], k1).astype(F32)
        s2 = jnp.einsum('bqhd,bkhd->bhqk', qblk[..., 1, :], k2).astype(F32)
        s1 = jnp.where(mask, s1, -1e30)
        s2 = jnp.where(mask, s2, -1e30)
        p = jax.nn.softmax(s1, axis=-1) - lam * jax.nn.softmax(s2, axis=-1)
        return jnp.einsum('bhqk,bkhd->bqhd', p.astype(v.dtype), v)

    out = lax.map(block, (qb, jnp.arange(nb)))
    return jnp.moveaxis(out, 0, 1).reshape(B, L, DA_HEADS, DA_V_DIM)


def rglru_mixer(x, conv_w, conv_b, w_a, b_a, w_x, b_x, lam):
    B, L, W = x.shape
    xc = lax.conv_general_dilated(
        x, conv_w[:, None, :], window_strides=(1,), padding=((CONV_WIDTH - 1, 0),),
        dimension_numbers=('NWC', 'WIO', 'NWC'), feature_group_count=W) + conv_b
    xf = xc.astype(F32)
    xb = xf.reshape(B, L, LRU_BLOCKS, LRU_BLOCK_DIM)
    r = jax.nn.sigmoid(jnp.einsum('blni,nij->blnj', xb, w_a.astype(F32)).reshape(B, L, W)
                       + b_a.astype(F32))
    i = jax.nn.sigmoid(jnp.einsum('blni,nij->blnj', xb, w_x.astype(F32)).reshape(B, L, W)
                       + b_x.astype(F32))
    log_a = -LRU_C * r * jax.nn.softplus(-lam.astype(F32))
    a = jnp.exp(log_a)
    mult = jnp.sqrt(-jnp.expm1(2.0 * log_a))
    b = mult * (i * xf)

    def combine(e1, e2):
        a1, b1 = e1
        a2, b2 = e2
        return a2 * a1, a2 * b1 + b2

    _, h = lax.associative_scan(combine, (a, b), axis=1)
    return h.astype(x.dtype)


def memory_attention(q, mem_n, w_mem_kv):
    B, L = q.shape[:2]
    k, v = jnp.split(mem_n @ w_mem_kv, 2, axis=-1)
    k = k.reshape(B, -1, MEM_HEADS, MEM_HEAD_DIM)
    v = v.reshape(B, -1, MEM_HEADS, MEM_HEAD_DIM)
    qh = q.reshape(B, L, MEM_HEADS, MEM_HEAD_DIM)
    s = jnp.einsum('blhd,bmhd->bhlm', qh, k).astype(F32) * (MEM_HEAD_DIM ** -0.5)
    p = jax.nn.softmax(s, axis=-1)
    return jnp.einsum('bhlm,bmhd->blhd', p.astype(v.dtype), v).reshape(B, L, GROUP_WIDTH)


def setup_inputs(seed: int = 0) -> dict:
    key = jax.random.key(seed)
    ks = jax.random.split(key, 32)

    def nrm(k, shape, scale):
        return jax.random.normal(k, shape, F32) * scale

    x = nrm(ks[0], (BATCH, SEQ, D_MODEL), 1.0)
    mem = nrm(ks[1], (BATCH, MEM_TOKENS, D_MODEL), 1.0)
    positions = jnp.broadcast_to(jnp.arange(SEQ, dtype=jnp.int32)[None, :], (BATCH, SEQ))
    norm_g = 1.0 + nrm(ks[2], (DEPTH, D_MODEL), 0.02)
    w_in = nrm(ks[3], (DEPTH, D_MODEL, IN_WIDTH), D_MODEL ** -0.5)
    w_out = nrm(ks[4], (DEPTH, MIX_WIDTH, D_MODEL), MIX_WIDTH ** -0.5)
    n = jnp.arange(S5_STATE, dtype=F32)
    s5_lambda_re = -0.5 + nrm(ks[5], (DEPTH, S5_GROUPS, S5_STATE), 0.01)
    s5_lambda_im = math.pi * n + nrm(ks[6], (DEPTH, S5_GROUPS, S5_STATE), 0.01)
    s5_log_dt = jax.random.uniform(ks[7], (DEPTH, S5_GROUPS), F32,
                                   math.log(S5_DT_MIN), math.log(S5_DT_MAX))
    bscale = (2.0 * S5_CH) ** -0.5
    cscale = (2.0 * S5_STATE) ** -0.5
    s5_b_re = nrm(ks[8], (DEPTH, S5_GROUPS, S5_STATE, S5_CH), bscale)
    s5_b_im = nrm(ks[9], (DEPTH, S5_GROUPS, S5_STATE, S5_CH), bscale)
    s5_c_re = nrm(ks[10], (DEPTH, S5_GROUPS, S5_CH, S5_STATE), cscale)
    s5_c_im = nrm(ks[11], (DEPTH, S5_GROUPS, S5_CH, S5_STATE), cscale)
    s5_d = nrm(ks[12], (DEPTH, S5_GROUPS, S5_CH), 1.0)
    s5_w_glu = nrm(ks[13], (DEPTH, GROUP_WIDTH, 2 * GROUP_WIDTH), GROUP_WIDTH ** -0.5)
    da_lambda_q1 = nrm(ks[14], (DEPTH, DA_HEAD_DIM), 0.1)
    da_lambda_k1 = nrm(ks[15], (DEPTH, DA_HEAD_DIM), 0.1)
    da_lambda_q2 = nrm(ks[16], (DEPTH, DA_HEAD_DIM), 0.1)
    da_lambda_k2 = nrm(ks[17], (DEPTH, DA_HEAD_DIM), 0.1)
    da_subln_g = 1.0 + nrm(ks[18], (DEPTH, DA_V_DIM), 0.02)
    lru_conv_w = nrm(ks[19], (DEPTH, CONV_WIDTH, GROUP_WIDTH), CONV_WIDTH ** -0.5)
    lru_conv_b = nrm(ks[20], (DEPTH, GROUP_WIDTH), 0.01)
    lru_w_a = nrm(ks[21], (DEPTH, LRU_BLOCKS, LRU_BLOCK_DIM, LRU_BLOCK_DIM), LRU_BLOCK_DIM ** -0.5)
    lru_b_a = nrm(ks[22], (DEPTH, GROUP_WIDTH), 0.01)
    lru_w_x = nrm(ks[23], (DEPTH, LRU_BLOCKS, LRU_BLOCK_DIM, LRU_BLOCK_DIM), LRU_BLOCK_DIM ** -0.5)
    lru_b_x = nrm(ks[24], (DEPTH, GROUP_WIDTH), 0.01)
    ua = jax.random.uniform(ks[25], (DEPTH, GROUP_WIDTH), F32, 0.9, 0.999)
    sa = ua ** (1.0 / LRU_C)
    lru_lambda = jnp.log(sa) - jnp.log1p(-sa)
    mem_norm_g = 1.0 + nrm(ks[26], (DEPTH, D_MODEL), 0.02)
    w_mem_kv = nrm(ks[27], (DEPTH, D_MODEL, 2 * GROUP_WIDTH), D_MODEL ** -0.5)
    final_norm_g = 1.0 + nrm(ks[28], (D_MODEL,), 0.02)
    return {"x": x, "mem": mem, "positions": positions, "norm_g": norm_g,
            "w_in": w_in, "w_out": w_out,
            "s5_lambda_re": s5_lambda_re, "s5_lambda_im": s5_lambda_im, "s5_log_dt": s5_log_dt,
            "s5_b_re": s5_b_re, "s5_b_im": s5_b_im, "s5_c_re": s5_c_re, "s5_c_im": s5_c_im,
            "s5_d": s5_d, "s5_w_glu": s5_w_glu,
            "da_lambda_q1": da_lambda_q1, "da_lambda_k1": da_lambda_k1,
            "da_lambda_q2": da_lambda_q2, "da_lambda_k2": da_lambda_k2, "da_subln_g": da_subln_g,
            "lru_conv_w": lru_conv_w, "lru_conv_b": lru_conv_b, "lru_w_a": lru_w_a,
            "lru_b_a": lru_b_a, "lru_w_x": lru_w_x, "lru_b_x": lru_b_x, "lru_lambda": lru_lambda,
            "mem_norm_g": mem_norm_g, "w_mem_kv": w_mem_kv, "final_norm_g": final_norm_g}


def reference(x, mem, positions, norm_g, w_in, w_out,
              s5_lambda_re, s5_lambda_im, s5_log_dt, s5_b_re, s5_b_im, s5_c_re, s5_c_im,
              s5_d, s5_w_glu,
              da_lambda_q1, da_lambda_k1, da_lambda_q2, da_lambda_k2, da_subln_g,
              lru_conv_w, lru_conv_b, lru_w_a, lru_b_a, lru_w_x, lru_b_x, lru_lambda,
              mem_norm_g, w_mem_kv, final_norm_g):
    B, L, _ = x.shape
    inv_freq = ROPE_THETA ** (-jnp.arange(0, ROPE_DIMS, 2, dtype=F32) / ROPE_DIMS)
    ang = positions.astype(F32)[..., None] * inv_freq
    cos = jnp.cos(ang)[:, :, None, None, :]
    sin = jnp.sin(ang)[:, :, None, None, :]

    for layer in range(DEPTH):
        h = rms_norm(x, norm_g[layer])
        proj = h @ w_in[layer]
        a_u, a_g, qd, kd, vd, b_g, c_x, c_g, m_q, m_g = jnp.split(proj, N_IN_SPLITS, axis=-1)

        y_a = s5_mixer(a_u, s5_lambda_re[layer], s5_lambda_im[layer], s5_log_dt[layer],
                       s5_b_re[layer], s5_b_im[layer], s5_c_re[layer], s5_c_im[layer],
                       s5_d[layer], s5_w_glu[layer])

        lam_init = 0.8 - 0.6 * math.exp(-0.3 * layer)
        lam = (jnp.exp(jnp.sum(da_lambda_q1[layer].astype(F32) * da_lambda_k1[layer].astype(F32)))
               - jnp.exp(jnp.sum(da_lambda_q2[layer].astype(F32) * da_lambda_k2[layer].astype(F32)))
               + lam_init)
        q = rope_partial(qd.reshape(B, L, DA_HEADS, 2, DA_HEAD_DIM), cos, sin)
        k = rope_partial(kd.reshape(B, L, DA_HEADS, 2, DA_HEAD_DIM), cos, sin)
        v = vd.reshape(B, L, DA_HEADS, DA_V_DIM)
        o_b = diff_attention(q, k, v, lam)
        y_b = (rms_norm(o_b, da_subln_g[layer]) * (1.0 - lam_init)).reshape(B, L, GROUP_WIDTH)

        y_c = rglru_mixer(c_x, lru_conv_w[layer], lru_conv_b[layer], lru_w_a[layer],
                          lru_b_a[layer], lru_w_x[layer], lru_b_x[layer], lru_lambda[layer])

        mem_n = rms_norm(mem, mem_norm_g[layer])
        y_m = memory_attention(m_q, mem_n, w_mem_kv[layer])

        mixed = jnp.concatenate([y_a * jax.nn.silu(a_g), y_b * jax.nn.silu(b_g),
                                 y_c * jax.nn.silu(c_g), y_m * jax.nn.silu(m_g)], axis=-1)
        x = x + mixed @ w_out[layer]

    return rms_norm(x, final_norm_g)
```

```python
import functools
import math

import jax
import jax.numpy as jnp
from jax import lax
from jax.experimental import pallas as pl
from jax.experimental.pallas import tpu as pltpu

F32 = jnp.float32
BF16 = jnp.bfloat16

EPS = 1e-6
GROUP_WIDTH = 512
LANES = 128
SUBLANES = 8
S5_STATE = 64
S5_CH = 16
S5_BLOCKS = 4
S5_BLOCK_STATES = 512
DA_HEADS = 4
DA_HEAD_DIM = 64
ROPE_DIMS = 16
ROPE_THETA = 500000.0
LRU_BLOCKS = 8
LRU_C = 8.0
MEM_HEADS = 4
MEM_HEAD_DIM = 128
NEG_BIG = -1e30
VMEM_LIMIT_BYTES = 56 * 1024 * 1024

COL_AU, COL_AG, COL_BG, COL_CX, COL_CG, COL_MQ, COL_MG = range(7)
N_REST = 7 * GROUP_WIDTH


def _params(*sem):
    return pltpu.CompilerParams(dimension_semantics=sem, vmem_limit_bytes=VMEM_LIMIT_BYTES)


def _sigmoid(x):
    return 1.0 / (1.0 + jnp.exp(-x))


def _silu(x):
    return x * _sigmoid(x)


def _gelu_tanh(x):
    return 0.5 * x * (1.0 + jnp.tanh(math.sqrt(2.0 / math.pi) * (x + 0.044715 * (x * x * x))))


def _dot(a, b):
    return jnp.dot(a, b, preferred_element_type=F32)


def _dot_nt(a, b):
    return lax.dot_general(a, b, (((1,), (1,)), ((), ())), preferred_element_type=F32)


def _rms(x, g):
    ms = jnp.mean(x * x, axis=-1, keepdims=True)
    return x * lax.rsqrt(ms + EPS) * g


def _inproj_kernel(x_ref, g_ref, w_ref, rc_ref, rs1_ref, rs2_ref, rest_ref, qkv_ref):
    hb = _rms(x_ref[...], g_ref[...]).astype(BF16)
    rest_ref[...] = _dot(hb, w_ref[:, :N_REST])
    qk = _dot(hb, w_ref[:, N_REST:N_REST + 2 * GROUP_WIDTH])
    c, s1, s2 = rc_ref[...], rs1_ref[...], rs2_ref[...]
    for j in range(2 * GROUP_WIDTH // LANES):
        t = qk[:, j * LANES:(j + 1) * LANES]
        t = t * c + pltpu.roll(t, LANES - ROPE_DIMS // 2, axis=1) * s1 + pltpu.roll(t, ROPE_DIMS // 2, axis=1) * s2
        qkv_ref[:, j * LANES:(j + 1) * LANES] = t.astype(BF16)
    qkv_ref[:, 2 * GROUP_WIDTH:] = _dot(hb, w_ref[:, N_REST + 2 * GROUP_WIDTH:]).astype(BF16)


def _inproj(x, g, w_all, layer, rc, rs1, rs2, tm):
    T, D = x.shape
    n_in = w_all.shape[-1]
    return pl.pallas_call(
        _inproj_kernel,
        grid=(T // tm,),
        in_specs=[
            pl.BlockSpec((tm, D), lambda i: (i, 0)),
            pl.BlockSpec((None, 1, D), lambda i: (layer, 0, 0)),
            pl.BlockSpec((None, D, n_in), lambda i: (layer, 0, 0)),
            pl.BlockSpec((tm, LANES), lambda i: (i, 0)),
            pl.BlockSpec((tm, LANES), lambda i: (i, 0)),
            pl.BlockSpec((tm, LANES), lambda i: (i, 0)),
        ],
        out_specs=[
            pl.BlockSpec((tm, N_REST), lambda i: (i, 0)),
            pl.BlockSpec((tm, 3 * GROUP_WIDTH), lambda i: (i, 0)),
        ],
        out_shape=[jax.ShapeDtypeStruct((T, N_REST), F32),
                   jax.ShapeDtypeStruct((T, 3 * GROUP_WIDTH), BF16)],
        compiler_params=_params("parallel"),
        name="inproj",
    )(x, g, w_all, rc, rs1, rs2)


def _s5_kernel(u_ref, g_ref, wb_ref, coef_ref, wc_ref, d_ref, wglu_ref, o_ref, xs_ref, *, chunk):
    c = pl.program_id(1)
    n_state = xs_ref.shape[1]

    @pl.when(c == 0)
    def _():
        xs_ref[0:SUBLANES, :] = jnp.zeros((SUBLANES, n_state), F32)

    @pl.when(c > 0)
    def _():
        xs_ref[0:SUBLANES, :] = xs_ref[chunk:chunk + SUBLANES, :]

    u = u_ref[...]
    ub = u.astype(BF16)
    blk = 2 * S5_BLOCK_STATES
    for q in range(S5_BLOCKS):
        xs_ref[SUBLANES:SUBLANES + chunk, q * blk:(q + 1) * blk] = _dot(ub[:, q * LANES:(q + 1) * LANES], wb_ref[q])

    def cmul_add(br, bi, ar, ai, sr, si):
        return br + (ar * sr - ai * si), bi + (ar * si + ai * sr)

    def tile_body(r, carry):
        base = pl.multiple_of(SUBLANES + r * SUBLANES, SUBLANES)
        for q in range(S5_BLOCKS):
            for j in range(S5_BLOCK_STATES // LANES):
                lr = q * blk + j * LANES
                li = lr + S5_BLOCK_STATES
                lc = q * S5_BLOCK_STATES + j * LANES
                rows = pl.ds(base, SUBLANES)
                prev = pl.ds(base - SUBLANES, SUBLANES)
                cr = jnp.broadcast_to(xs_ref[prev, lr:lr + LANES][SUBLANES - 1:, :], (SUBLANES, LANES))
                ci = jnp.broadcast_to(xs_ref[prev, li:li + LANES][SUBLANES - 1:, :], (SUBLANES, LANES))
                br = xs_ref[rows, lr:lr + LANES]
                bi = xs_ref[rows, li:li + LANES]
                br, bi = cmul_add(br, bi, coef_ref[0, :, lc:lc + LANES], coef_ref[1, :, lc:lc + LANES], cr, ci)
                for k, shift in ((1, 1), (2, 2), (3, 4)):
                    sr = pltpu.roll(br, shift, axis=0)
                    si = pltpu.roll(bi, shift, axis=0)
                    br, bi = cmul_add(br, bi, coef_ref[2 * k, :, lc:lc + LANES],
                                      coef_ref[2 * k + 1, :, lc:lc + LANES], sr, si)
                xs_ref[rows, lr:lr + LANES] = br
                xs_ref[rows, li:li + LANES] = bi
        return carry

    lax.fori_loop(0, chunk // SUBLANES, tile_body, 0)

    ys = []
    for q in range(S5_BLOCKS):
        xb = xs_ref[SUBLANES:SUBLANES + chunk, q * blk:(q + 1) * blk].astype(BF16)
        ys.append(_dot(xb, wc_ref[q]))
    y = jnp.concatenate(ys, axis=1) + d_ref[...] * u
    y = _gelu_tanh(y)
    z = _dot(y.astype(BF16), wglu_ref[...])
    out = z[:, :GROUP_WIDTH] * _sigmoid(z[:, GROUP_WIDTH:]) * _silu(g_ref[...])
    o_ref[...] = out.astype(BF16)


def _s5(rest, wb, coef, wc, d, wglu, layer, B, L, chunk):
    T = B * L
    nc = L // chunk
    n_state = S5_BLOCKS * 2 * S5_BLOCK_STATES
    return pl.pallas_call(
        functools.partial(_s5_kernel, chunk=chunk),
        grid=(B, nc),
        in_specs=[
            pl.BlockSpec((chunk, GROUP_WIDTH), lambda b, c: (b * nc + c, COL_AU)),
            pl.BlockSpec((chunk, GROUP_WIDTH), lambda b, c: (b * nc + c, COL_AG)),
            pl.BlockSpec((None,) + wb.shape[1:], lambda b, c: (layer, 0, 0, 0)),
            pl.BlockSpec((None,) + coef.shape[1:], lambda b, c: (layer, 0, 0, 0)),
            pl.BlockSpec((None,) + wc.shape[1:], lambda b, c: (layer, 0, 0, 0)),
            pl.BlockSpec((None, 1, GROUP_WIDTH), lambda b, c: (layer, 0, 0)),
            pl.BlockSpec((None,) + wglu.shape[1:], lambda b, c: (layer, 0, 0)),
        ],
        out_specs=pl.BlockSpec((chunk, GROUP_WIDTH), lambda b, c: (b * nc + c, 0)),
        out_shape=jax.ShapeDtypeStruct((T, GROUP_WIDTH), BF16),
        scratch_shapes=[pltpu.VMEM((chunk + SUBLANES, n_state), F32)],
        compiler_params=_params("arbitrary", "arbitrary"),
        name="s5",
    )(rest, rest, wb, coef, wc, d, wglu)


def _attn_kernel(lamv_ref, q_ref, k_ref, v_ref, g_ref, sg_ref, o_ref,
                 qs_ref, m_ref, l_ref, acc_ref, *, tq, tk, lam_init):
    i = pl.program_id(2)
    q = q_ref[...]
    lane = lax.broadcasted_iota(jnp.int32, q.shape, 1)
    zero = jnp.zeros_like(q)
    qs_ref[0:tq, :] = jnp.where(lane < DA_HEAD_DIM, q, zero)
    qs_ref[tq:, :] = jnp.where(lane >= DA_HEAD_DIM, q, zero)
    m_ref[...] = jnp.full(m_ref.shape, NEG_BIG, F32)
    l_ref[...] = jnp.zeros(l_ref.shape, F32)
    acc_ref[...] = jnp.zeros(acc_ref.shape, F32)

    def block(kstart, masked):
        k = k_ref[pl.ds(kstart, tk), :]
        v = v_ref[pl.ds(kstart, tk), :]
        s = _dot_nt(qs_ref[...], k)
        if masked:
            row = lax.broadcasted_iota(jnp.int32, s.shape, 0)
            col = lax.broadcasted_iota(jnp.int32, s.shape, 1)
            qpos = i * tq + jnp.where(row >= tq, row - tq, row)
            s = jnp.where(kstart + col <= qpos, s, NEG_BIG)
        m_prev = m_ref[...]
        m_new = jnp.maximum(m_prev, jnp.max(s, axis=-1, keepdims=True))
        alpha = jnp.exp(m_prev - m_new)
        p = jnp.exp(s - m_new)
        l_ref[...] = alpha * l_ref[...] + jnp.sum(p, axis=-1, keepdims=True)
        acc_ref[...] = alpha * acc_ref[...] + _dot(p.astype(BF16), v)
        m_ref[...] = m_new

    n_full = (i * tq) // tk

    def full_body(kb, carry):
        block(pl.multiple_of(kb * tk, tk), False)
        return carry

    lax.fori_loop(0, n_full, full_body, 0)
    block(pl.multiple_of(n_full * tk, tk), True)

    lv = lamv_ref[...]
    lam = (jnp.exp(jnp.sum(lv[0:1, :] * lv[1:2, :], axis=-1, keepdims=True))
           - jnp.exp(jnp.sum(lv[2:3, :] * lv[3:4, :], axis=-1, keepdims=True)) + lam_init)
    acc = acc_ref[...]
    l = l_ref[...]
    o = acc[:tq] / l[:tq] - lam * (acc[tq:] / l[tq:])
    o = _rms(o, sg_ref[...]) * (1.0 - lam_init)
    o_ref[...] = (o * _silu(g_ref[...])).astype(BF16)


def _attn(qkv, rest, lamv, sg, layer, B, L, tq, tk):
    T = B * L
    nq = L // tq
    lam_init = 0.8 - 0.6 * math.exp(-0.3 * layer)
    gate_blk = COL_BG * (GROUP_WIDTH // LANES)
    return pl.pallas_call(
        functools.partial(_attn_kernel, tq=tq, tk=tk, lam_init=lam_init),
        grid=(B, DA_HEADS, nq),
        in_specs=[
            pl.BlockSpec((None, 4, DA_HEAD_DIM), lambda b, h, i: (layer, 0, 0)),
            pl.BlockSpec((tq, LANES), lambda b, h, i: (b * nq + i, h)),
            pl.BlockSpec((L, LANES), lambda b, h, i: (b, DA_HEADS + h)),
            pl.BlockSpec((L, LANES), lambda b, h, i: (b, 2 * DA_HEADS + h)),
            pl.BlockSpec((tq, LANES), lambda b, h, i: (b * nq + i, gate_blk + h)),
            pl.BlockSpec((None, 1, LANES), lambda b, h, i: (layer, 0, 0)),
        ],
        out_specs=pl.BlockSpec((tq, LANES), lambda b, h, i: (b * nq + i, h)),
        out_shape=jax.ShapeDtypeStruct((T, GROUP_WIDTH), BF16),
        scratch_shapes=[
            pltpu.VMEM((2 * tq, LANES), BF16),
            pltpu.VMEM((2 * tq, 1), F32),
            pltpu.VMEM((2 * tq, 1), F32),
            pltpu.VMEM((2 * tq, LANES), F32),
        ],
        compiler_params=_params("parallel", "parallel", "arbitrary"),
        name="diffattn",
    )(lamv, qkv, qkv, qkv, rest, sg)


def _lru_kernel(x_ref, g_ref, cw_ref, cb_ref, wg_ref, bg_ref, lam_ref, o_ref,
                xe_ref, a_ref, h_ref, *, chunk):
    c = pl.program_id(1)

    @pl.when(c == 0)
    def _():
        xe_ref[0:SUBLANES, :] = jnp.zeros((SUBLANES, GROUP_WIDTH), F32)
        h_ref[0:SUBLANES, :] = jnp.zeros((SUBLANES, GROUP_WIDTH), F32)

    @pl.when(c > 0)
    def _():
        xe_ref[0:SUBLANES, :] = xe_ref[chunk:chunk + SUBLANES, :]
        h_ref[0:SUBLANES, :] = h_ref[chunk:chunk + SUBLANES, :]

    x = x_ref[...]
    xe_ref[SUBLANES:SUBLANES + chunk, :] = x
    xc = (cw_ref[3:4, :] * x
          + cw_ref[2:3, :] * xe_ref[SUBLANES - 1:SUBLANES - 1 + chunk, :]
          + cw_ref[1:2, :] * xe_ref[SUBLANES - 2:SUBLANES - 2 + chunk, :]
          + cw_ref[0:1, :] * xe_ref[SUBLANES - 3:SUBLANES - 3 + chunk, :]
          + cb_ref[...])
    z = _dot(xc.astype(BF16), wg_ref[...]) + bg_ref[...]
    r = _sigmoid(z[:, :GROUP_WIDTH])
    gate_i = _sigmoid(z[:, GROUP_WIDTH:])
    nl = -lam_ref[...]
    softplus = jnp.maximum(nl, 0.0) + jnp.log1p(jnp.exp(-jnp.abs(nl)))
    log_a = (-LRU_C) * r * softplus
    a = jnp.exp(log_a)
    a_ref[...] = a
    mult = jnp.sqrt(-jnp.tanh(log_a) * (1.0 + a * a))
    h_ref[SUBLANES:SUBLANES + chunk, :] = mult * (gate_i * xc)

    row = lax.broadcasted_iota(jnp.int32, (SUBLANES, LANES), 0)

    def tile_body(t, carry):
        base = pl.multiple_of(t * SUBLANES, SUBLANES)
        for j in range(GROUP_WIDTH // LANES):
            cols = slice(j * LANES, (j + 1) * LANES)
            a = a_ref[pl.ds(base, SUBLANES), cols]
            b = h_ref[pl.ds(base + SUBLANES, SUBLANES), cols]
            hprev = jnp.broadcast_to(h_ref[pl.ds(base, SUBLANES), cols][SUBLANES - 1:, :], (SUBLANES, LANES))
            b = b + jnp.where(row == 0, a * hprev, 0.0)
            for shift in (1, 2, 4):
                keep = row >= shift
                b = b + jnp.where(keep, a * pltpu.roll(b, shift, axis=0), 0.0)
                if shift != 4:
                    a = jnp.where(keep, a * pltpu.roll(a, shift, axis=0), a)
            h_ref[pl.ds(base + SUBLANES, SUBLANES), cols] = b
        return carry

    lax.fori_loop(0, chunk // SUBLANES, tile_body, 0)
    o_ref[...] = (h_ref[SUBLANES:SUBLANES + chunk, :] * _silu(g_ref[...])).astype(BF16)


def _lru(rest, cw, cb, wg, bg, lam, layer, B, L, chunk):
    T = B * L
    nc = L // chunk
    return pl.pallas_call(
        functools.partial(_lru_kernel, chunk=chunk),
        grid=(B, nc),
        in_specs=[
            pl.BlockSpec((chunk, GROUP_WIDTH), lambda b, c: (b * nc + c, COL_CX)),
            pl.BlockSpec((chunk, GROUP_WIDTH), lambda b, c: (b * nc + c, COL_CG)),
            pl.BlockSpec((None, 4, GROUP_WIDTH), lambda b, c: (layer, 0, 0)),
            pl.BlockSpec((None, 1, GROUP_WIDTH), lambda b, c: (layer, 0, 0)),
            pl.BlockSpec((None, GROUP_WIDTH, 2 * GROUP_WIDTH), lambda b, c: (layer, 0, 0)),
            pl.BlockSpec((None, 1, 2 * GROUP_WIDTH), lambda b, c: (layer, 0, 0)),
            pl.BlockSpec((None, 1, GROUP_WIDTH), lambda b, c: (layer, 0, 0)),
        ],
        out_specs=pl.BlockSpec((chunk, GROUP_WIDTH), lambda b, c: (b * nc + c, 0)),
        out_shape=jax.ShapeDtypeStruct((T, GROUP_WIDTH), BF16),
        scratch_shapes=[
            pltpu.VMEM((chunk + SUBLANES, GROUP_WIDTH), F32),
            pltpu.VMEM((chunk, GROUP_WIDTH), F32),
            pltpu.VMEM((chunk + SUBLANES, GROUP_WIDTH), F32),
        ],
        compiler_params=_params("arbitrary", "arbitrary"),
        name="rglru",
    )(rest, rest, cw, cb, wg, bg, lam)


def _memkv_kernel(mem_ref, g_ref, w_ref, o_ref):
    o_ref[...] = _dot(_rms(mem_ref[...], g_ref[...]).astype(BF16), w_ref[...]).astype(BF16)


def _memkv(mem2d, g, w, B, M):
    depth, D, n_kv = w.shape
    return pl.pallas_call(
        _memkv_kernel,
        grid=(depth, B),
        in_specs=[
            pl.BlockSpec((M, D), lambda l, b: (b, 0)),
            pl.BlockSpec((None, 1, D), lambda l, b: (l, 0, 0)),
            pl.BlockSpec((None, D, n_kv), lambda l, b: (l, 0, 0)),
        ],
        out_specs=pl.BlockSpec((None, M, n_kv), lambda l, b: (l, b, 0)),
        out_shape=jax.ShapeDtypeStruct((depth, B * M, n_kv), BF16),
        compiler_params=_params("parallel", "parallel"),
        name="memkv",
    )(mem2d, g, w)


def _memattn_kernel(q_ref, g_ref, kv_ref, o_ref):
    q = q_ref[...].astype(BF16)
    scale = MEM_HEAD_DIM ** -0.5
    outs = []
    for h in range(MEM_HEADS):
        cols = slice(h * MEM_HEAD_DIM, (h + 1) * MEM_HEAD_DIM)
        k = kv_ref[:, h * MEM_HEAD_DIM:(h + 1) * MEM_HEAD_DIM]
        v = kv_ref[:, GROUP_WIDTH + h * MEM_HEAD_DIM:GROUP_WIDTH + (h + 1) * MEM_HEAD_DIM]
        s = _dot_nt(q[:, cols], k) * scale
        e = jnp.exp(s - jnp.max(s, axis=-1, keepdims=True))
        p = e / jnp.sum(e, axis=-1, keepdims=True)
        outs.append(_dot(p.astype(BF16), v))
    o_ref[...] = (jnp.concatenate(outs, axis=1) * _silu(g_ref[...])).astype(BF16)


def _memattn(rest, kv, layer, B, L, M, tm):
    T = B * L
    per_b = L // tm
    return pl.pallas_call(
        _memattn_kernel,
        grid=(T // tm,),
        in_specs=[
            pl.BlockSpec((tm, GROUP_WIDTH), lambda i: (i, COL_MQ)),
            pl.BlockSpec((tm, GROUP_WIDTH), lambda i: (i, COL_MG)),
            pl.BlockSpec((None, M, 2 * GROUP_WIDTH), lambda i: (layer, i // per_b, 0)),
        ],
        out_specs=pl.BlockSpec((tm, GROUP_WIDTH), lambda i: (i, 0)),
        out_shape=jax.ShapeDtypeStruct((T, GROUP_WIDTH), BF16),
        compiler_params=_params("parallel"),
        name="memattn",
    )(rest, rest, kv)


def _outproj_kernel(ya_ref, yb_ref, yc_ref, ym_ref, w_ref, x_ref, fg_ref, o_ref, *, final):
    acc = x_ref[...]
    for n, y_ref in enumerate((ya_ref, yb_ref, yc_ref, ym_ref)):
        acc = acc + _dot(y_ref[...], w_ref[n * GROUP_WIDTH:(n + 1) * GROUP_WIDTH, :])
    o_ref[...] = _rms(acc, fg_ref[...]) if final else acc


def _outproj(ys, w_all, layer, x, fg, final, tm):
    T, D = x.shape
    y_spec = pl.BlockSpec((tm, GROUP_WIDTH), lambda i: (i, 0))
    return pl.pallas_call(
        functools.partial(_outproj_kernel, final=final),
        grid=(T // tm,),
        in_specs=[y_spec, y_spec, y_spec, y_spec,
                  pl.BlockSpec((None,) + w_all.shape[1:], lambda i: (layer, 0, 0)),
                  pl.BlockSpec((tm, D), lambda i: (i, 0)),
                  pl.BlockSpec((1, D), lambda i: (0, 0))],
        out_specs=pl.BlockSpec((tm, D), lambda i: (i, 0)),
        out_shape=jax.ShapeDtypeStruct((T, D), F32),
        compiler_params=_params("parallel"),
        name="outproj",
    )(*ys, w_all, x, fg)


def _block_diag(blocks):
    n = blocks.shape[-3]
    eye = jnp.eye(n, dtype=blocks.dtype)
    out = jnp.einsum('...nrc,nm->...nrmc', blocks, eye)
    return out.reshape(blocks.shape[:-3] + (n * blocks.shape[-2], n * blocks.shape[-1]))


def _s5_prep(lam_re, lam_im, log_dt, b_re, b_im, c_re, c_im):
    depth = lam_re.shape[0]
    dt = jnp.exp(log_dt)[..., None]
    mag = jnp.exp(lam_re * dt)
    abar_re = mag * jnp.cos(lam_im * dt)
    abar_im = mag * jnp.sin(lam_im * dt)
    den = lam_re * lam_re + lam_im * lam_im
    nr, ni = abar_re - 1.0, abar_im
    f_re = (nr * lam_re + ni * lam_im) / den
    f_im = (ni * lam_re - nr * lam_im) / den
    bb_re = f_re[..., None] * b_re - f_im[..., None] * b_im
    bb_im = f_re[..., None] * b_im + f_im[..., None] * b_re
    gpb = LANES // S5_CH

    def in_blocks(bb):
        return _block_diag(jnp.swapaxes(bb, -1, -2).reshape(depth, S5_BLOCKS, gpb, S5_CH, S5_STATE))

    def out_blocks(cc):
        return _block_diag(jnp.swapaxes(cc, -1, -2).reshape(depth, S5_BLOCKS, gpb, S5_STATE, S5_CH))

    wb = jnp.concatenate([in_blocks(bb_re), in_blocks(bb_im)], axis=-1).astype(BF16)
    wc = jnp.concatenate([out_blocks(c_re), out_blocks(-c_im)], axis=-2).astype(BF16)

    ar = abar_re.reshape(depth, 1, -1)
    ai = abar_im.reshape(depth, 1, -1)
    a2r, a2i = ar * ar - ai * ai, 2.0 * ar * ai
    a4r, a4i = a2r * a2r - a2i * a2i, 2.0 * a2r * a2i
    row = jnp.arange(SUBLANES)[None, :, None]

    def tile(v, keep):
        return jnp.where(keep, jnp.broadcast_to(v, (depth, SUBLANES, v.shape[-1])), 0.0)

    coef = jnp.stack([tile(ar, row == 0), tile(ai, row == 0),
                      tile(ar, row >= 1), tile(ai, row >= 1),
                      tile(a2r, row >= 2), tile(a2i, row >= 2),
                      tile(a4r, row >= 4), tile(a4i, row >= 4)], axis=1)
    return wb, wc, coef.astype(F32)


def _rope_tables(positions):
    inv_freq = ROPE_THETA ** (-jnp.arange(0, ROPE_DIMS, 2, dtype=F32) / ROPE_DIMS)
    ang = positions.astype(F32).reshape(-1, 1) * inv_freq
    cos, sin = jnp.cos(ang), jnp.sin(ang)
    half = ROPE_DIMS // 2
    pad = DA_HEAD_DIM - ROPE_DIMS
    n = cos.shape[0]
    ones, zeros = jnp.ones((n, pad), F32), jnp.zeros((n, pad), F32)
    zh = jnp.zeros((n, half), F32)
    rc = jnp.concatenate([cos, cos, ones], axis=1)
    rs1 = jnp.concatenate([-sin, zh, zeros], axis=1)
    rs2 = jnp.concatenate([zh, sin, zeros], axis=1)
    rep = LANES // DA_HEAD_DIM
    return jnp.tile(rc, (1, rep)), jnp.tile(rs1, (1, rep)), jnp.tile(rs2, (1, rep))


def _tiles(L):
    return dict(tm=min(256, L), chunk=min(256, L), tq=min(256, L), tk=min(512, L), tmem=min(512, L))


def kernel(x, mem, positions, norm_g, w_in, w_out, s5_lambda_re, s5_lambda_im, s5_log_dt, s5_b_re, s5_b_im, s5_c_re, s5_c_im, s5_d, s5_w_glu, da_lambda_q1, da_lambda_k1, da_lambda_q2, da_lambda_k2, da_subln_g, lru_conv_w, lru_conv_b, lru_w_a, lru_b_a, lru_w_x, lru_b_x, lru_lambda, mem_norm_g, w_mem_kv, final_norm_g):
    B, L, D = x.shape
    M = mem.shape[1]
    depth = w_in.shape[0]
    T = B * L
    t = _tiles(L)
    assert L % t["tk"] == 0 and t["tk"] % t["tq"] == 0 and L % t["tmem"] == 0

    gw = GROUP_WIDTH
    cols = [w_in[..., n * gw:(n + 1) * gw] for n in range(10)]
    a_u, a_g, qd, kd, vd, b_g, c_x, c_g, m_q, m_g = cols
    w_in_b = jnp.concatenate([a_u, a_g, b_g, c_x, c_g, m_q, m_g,
                              qd * (DA_HEAD_DIM ** -0.5), kd, vd], axis=-1).astype(BF16)
    w_out_b = w_out.astype(BF16)
    wglu_b = s5_w_glu.astype(BF16)
    wb, wc, coef = _s5_prep(s5_lambda_re, s5_lambda_im, s5_log_dt, s5_b_re, s5_b_im, s5_c_re, s5_c_im)
    s5_d2 = s5_d.reshape(depth, 1, gw)
    lamv = jnp.stack([da_lambda_q1, da_lambda_k1, da_lambda_q2, da_lambda_k2], axis=1)
    sg = da_subln_g.reshape(depth, 1, LANES)
    wg = jnp.concatenate([_block_diag(lru_w_a), _block_diag(lru_w_x)], axis=-1).astype(BF16)
    bg = jnp.concatenate([lru_b_a, lru_b_x], axis=-1).reshape(depth, 1, 2 * gw)
    lru_cb = lru_conv_b.reshape(depth, 1, gw)
    lru_lam = lru_lambda.reshape(depth, 1, gw)
    norm_g3 = norm_g.reshape(depth, 1, D)
    rc, rs1, rs2 = _rope_tables(positions)

    kv = _memkv(mem.reshape(B * M, D), mem_norm_g.reshape(depth, 1, D), w_mem_kv.astype(BF16), B, M)

    xf = x.reshape(T, D)
    fg = final_norm_g.reshape(1, D)
    for layer in range(depth):
        rest, qkv = _inproj(xf, norm_g3, w_in_b, layer, rc, rs1, rs2, t["tm"])
        y_a = _s5(rest, wb, coef, wc, s5_d2, wglu_b, layer, B, L, t["chunk"])
        y_b = _attn(qkv, rest, lamv, sg, layer, B, L, t["tq"], t["tk"])
        y_c = _lru(rest, lru_conv_w, lru_cb, wg, bg, lru_lam, layer, B, L, t["chunk"])
        y_m = _memattn(rest, kv, layer, B, L, M, t["tmem"])
        xf = _outproj((y_a, y_b, y_c, y_m), w_out_b, layer, xf, fg, layer == depth - 1, t["tm"])
    return xf.reshape(B, L, D)
```

```python
import functools
import math

import jax
import jax.numpy as jnp
from jax import lax
from jax.experimental import pallas as pl
from jax.experimental.pallas import tpu as pltpu

F32 = jnp.float32
BF16 = jnp.bfloat16

EPS = 1e-6
GROUP_WIDTH = 512
LANES = 128
SUBLANES = 8
S5_STATE = 64
S5_CH = 16
S5_BLOCKS = 4
S5_BLOCK_STATES = 512
DA_HEADS = 4
DA_HEAD_DIM = 64
ROPE_DIMS = 16
ROPE_THETA = 500000.0
LRU_BLOCKS = 8
LRU_C = 8.0
MEM_HEADS = 4
MEM_HEAD_DIM = 128
NEG_BIG = -1e30
LOG2E = math.log2(math.e)
VMEM_LIMIT_BYTES = 56 * 1024 * 1024

COL_AU, COL_AG, COL_BG, COL_CX, COL_CG, COL_MQ, COL_MG = range(7)
N_REST = 7 * GROUP_WIDTH


def _params(*sem):
    return pltpu.CompilerParams(dimension_semantics=sem, vmem_limit_bytes=VMEM_LIMIT_BYTES)


def _sigmoid(x):
    return 0.5 * jnp.tanh(0.5 * x) + 0.5


def _silu(x):
    return x * _sigmoid(x)


def _gelu_tanh(x):
    return 0.5 * x * (1.0 + jnp.tanh(math.sqrt(2.0 / math.pi) * (x + 0.044715 * (x * x * x))))


def _dot(a, b):
    return jnp.dot(a, b, preferred_element_type=F32)


def _dot_nt(a, b):
    return lax.dot_general(a, b, (((1,), (1,)), ((), ())), preferred_element_type=F32)


def _rms(x, g):
    ms = jnp.mean(x * x, axis=-1, keepdims=True)
    return x * lax.rsqrt(ms + EPS) * g


def _inproj_kernel(x_ref, g_ref, w_ref, rc_ref, rs1_ref, rs2_ref, rest_ref, qkv_ref):
    hb = _rms(x_ref[...], g_ref[...]).astype(BF16)
    rest_ref[...] = _dot(hb, w_ref[:, :N_REST])
    qk = _dot(hb, w_ref[:, N_REST:N_REST + 2 * GROUP_WIDTH])
    c, s1, s2 = rc_ref[...], rs1_ref[...], rs2_ref[...]
    for j in range(2 * GROUP_WIDTH // LANES):
        t = qk[:, j * LANES:(j + 1) * LANES]
        t = t * c + pltpu.roll(t, LANES - ROPE_DIMS // 2, axis=1) * s1 + pltpu.roll(t, ROPE_DIMS // 2, axis=1) * s2
        if j < GROUP_WIDTH // LANES:
            t = t * LOG2E
        qkv_ref[:, j * LANES:(j + 1) * LANES] = t.astype(BF16)
    qkv_ref[:, 2 * GROUP_WIDTH:] = _dot(hb, w_ref[:, N_REST + 2 * GROUP_WIDTH:]).astype(BF16)


def _inproj(x, g, w_all, layer, rc, rs1, rs2, tm):
    T, D = x.shape
    n_in = w_all.shape[-1]
    return pl.pallas_call(
        _inproj_kernel,
        grid=(T // tm,),
        in_specs=[
            pl.BlockSpec((tm, D), lambda i: (i, 0)),
            pl.BlockSpec((None, 1, D), lambda i: (layer, 0, 0)),
            pl.BlockSpec((None, D, n_in), lambda i: (layer, 0, 0)),
            pl.BlockSpec((tm, LANES), lambda i: (i, 0)),
            pl.BlockSpec((tm, LANES), lambda i: (i, 0)),
            pl.BlockSpec((tm, LANES), lambda i: (i, 0)),
        ],
        out_specs=[
            pl.BlockSpec((tm, N_REST), lambda i: (i, 0)),
            pl.BlockSpec((tm, 3 * GROUP_WIDTH), lambda i: (i, 0)),
        ],
        out_shape=[jax.ShapeDtypeStruct((T, N_REST), F32),
                   jax.ShapeDtypeStruct((T, 3 * GROUP_WIDTH), BF16)],
        compiler_params=_params("parallel"),
        name="inproj",
    )(x, g, w_all, rc, rs1, rs2)


def _s5_kernel(u_ref, g_ref, wb_ref, coef_ref, wc_ref, d_ref, wglu_ref, o_ref,
               bu_ref, xs_ref, carry_ref, *, chunk):
    @pl.when(pl.program_id(1) == 0)
    def _():
        carry_ref[...] = jnp.zeros(carry_ref.shape, F32)

    u = u_ref[...]
    ub = u.astype(BF16)
    blk = 2 * S5_BLOCK_STATES
    for q in range(S5_BLOCKS):
        bu_ref[:, q * blk:(q + 1) * blk] = _dot(ub[:, q * LANES:(q + 1) * LANES], wb_ref[q])

    def cmul_add(br, bi, ar, ai, sr, si):
        return br + (ar * sr - ai * si), bi + (ar * si + ai * sr)

    def tile_body(r, carry):
        rows = pl.ds(pl.multiple_of(r * SUBLANES, SUBLANES), SUBLANES)
        for q in range(S5_BLOCKS):
            for j in range(S5_BLOCK_STATES // LANES):
                lr = q * blk + j * LANES
                li = lr + S5_BLOCK_STATES
                lc = q * S5_BLOCK_STATES + j * LANES
                br = bu_ref[rows, lr:lr + LANES]
                bi = bu_ref[rows, li:li + LANES]
                br, bi = cmul_add(br, bi, coef_ref[0, :, lc:lc + LANES], coef_ref[1, :, lc:lc + LANES],
                                  carry_ref[:, lr:lr + LANES], carry_ref[:, li:li + LANES])
                for k, shift in ((1, 1), (2, 2), (3, 4)):
                    sr = pltpu.roll(br, shift, axis=0)
                    si = pltpu.roll(bi, shift, axis=0)
                    br, bi = cmul_add(br, bi, coef_ref[2 * k, :, lc:lc + LANES],
                                      coef_ref[2 * k + 1, :, lc:lc + LANES], sr, si)
                xs_ref[rows, lr:lr + LANES] = br
                xs_ref[rows, li:li + LANES] = bi
                carry_ref[:, lr:lr + LANES] = jnp.broadcast_to(br[SUBLANES - 1:, :], (SUBLANES, LANES))
                carry_ref[:, li:li + LANES] = jnp.broadcast_to(bi[SUBLANES - 1:, :], (SUBLANES, LANES))
        return carry

    lax.fori_loop(0, chunk // SUBLANES, tile_body, 0)

    ys = []
    for q in range(S5_BLOCKS):
        xb = xs_ref[:, q * blk:(q + 1) * blk].astype(BF16)
        ys.append(_dot(xb, wc_ref[q]))
    y = jnp.concatenate(ys, axis=1) + d_ref[...] * u
    y = _gelu_tanh(y)
    z = _dot(y.astype(BF16), wglu_ref[...])
    out = z[:, :GROUP_WIDTH] * _sigmoid(z[:, GROUP_WIDTH:]) * _silu(g_ref[...])
    o_ref[...] = out.astype(BF16)


def _s5(rest, wb, coef, wc, d, wglu, layer, B, L, chunk):
    T = B * L
    nc = L // chunk
    n_state = S5_BLOCKS * 2 * S5_BLOCK_STATES
    return pl.pallas_call(
        functools.partial(_s5_kernel, chunk=chunk),
        grid=(B, nc),
        in_specs=[
            pl.BlockSpec((chunk, GROUP_WIDTH), lambda b, c: (b * nc + c, COL_AU)),
            pl.BlockSpec((chunk, GROUP_WIDTH), lambda b, c: (b * nc + c, COL_AG)),
            pl.BlockSpec((None,) + wb.shape[1:], lambda b, c: (layer, 0, 0, 0)),
            pl.BlockSpec((None,) + coef.shape[1:], lambda b, c: (layer, 0, 0, 0)),
            pl.BlockSpec((None,) + wc.shape[1:], lambda b, c: (layer, 0, 0, 0)),
            pl.BlockSpec((None, 1, GROUP_WIDTH), lambda b, c: (layer, 0, 0)),
            pl.BlockSpec((None,) + wglu.shape[1:], lambda b, c: (layer, 0, 0)),
        ],
        out_specs=pl.BlockSpec((chunk, GROUP_WIDTH), lambda b, c: (b * nc + c, 0)),
        out_shape=jax.ShapeDtypeStruct((T, GROUP_WIDTH), BF16),
        scratch_shapes=[pltpu.VMEM((chunk, n_state), F32),
                        pltpu.VMEM((chunk, n_state), F32),
                        pltpu.VMEM((SUBLANES, n_state), F32)],
        compiler_params=_params("arbitrary", "arbitrary"),
        name="s5",
    )(rest, rest, wb, coef, wc, d, wglu)


def _attn_kernel(lamv_ref, q_ref, k_ref, v_ref, g_ref, sg_ref, o_ref,
                 qs_ref, s_ref, m_ref, acc_ref, *, tq, tk, lam_init):
    i = pl.program_id(2)
    q = q_ref[...]
    lane = lax.broadcasted_iota(jnp.int32, q.shape, 1)
    zero = jnp.zeros_like(q)
    qs_ref[0:tq, :] = jnp.where(lane < DA_HEAD_DIM, q, zero)
    qs_ref[tq:, :] = jnp.where(lane >= DA_HEAD_DIM, q, zero)
    m_ref[...] = jnp.full(m_ref.shape, NEG_BIG, F32)
    acc_ref[...] = jnp.zeros(acc_ref.shape, F32)
    ones = jnp.ones((tk, LANES), BF16)

    def scores(kb, slot):
        s_ref[slot] = _dot_nt(qs_ref[...], k_ref[pl.ds(pl.multiple_of(kb * tk, tk), tk), :])

    def softmax_pv(kb, slot, masked):
        kstart = pl.multiple_of(kb * tk, tk)
        v = jnp.concatenate([v_ref[pl.ds(kstart, tk), :], ones], axis=1)
        s = s_ref[slot]
        if masked:
            row = lax.broadcasted_iota(jnp.int32, s.shape, 0)
            col = lax.broadcasted_iota(jnp.int32, s.shape, 1)
            qpos = i * tq + jnp.where(row >= tq, row - tq, row)
            s = jnp.where(kstart + col <= qpos, s, NEG_BIG)
        m_prev = m_ref[...]
        m_new = jnp.maximum(m_prev, jnp.max(s, axis=-1, keepdims=True))
        alpha = jnp.exp2(m_prev - m_new)
        p = jnp.exp2(s - jnp.concatenate([m_new] * (tk // LANES), axis=1))
        acc_ref[...] = jnp.concatenate([alpha, alpha], axis=1) * acc_ref[...] + _dot(p.astype(BF16), v)
        m_ref[...] = m_new

    scores(0, 0)

    def pair_body(jj, carry):
        kb = 2 * jj
        scores(kb + 1, 1)
        softmax_pv(kb, 0, False)
        scores(kb + 2, 0)
        softmax_pv(kb + 1, 1, False)
        return carry

    odd = lax.rem(i, 2)
    lax.fori_loop(0, lax.div(i, 2), pair_body, 0)

    @pl.when(odd == 0)
    def _():
        softmax_pv(i, 0, True)

    @pl.when(odd == 1)
    def _():
        scores(i, 1)
        softmax_pv(i - 1, 0, False)
        softmax_pv(i, 1, True)

    lv = lamv_ref[...]
    lam = (jnp.exp(jnp.sum(lv[0:1, :] * lv[1:2, :], axis=-1, keepdims=True))
           - jnp.exp(jnp.sum(lv[2:3, :] * lv[3:4, :], axis=-1, keepdims=True)) + lam_init)
    acc = acc_ref[...]
    o = (acc[:tq, :LANES] / acc[:tq, LANES:]) - lam * (acc[tq:, :LANES] / acc[tq:, LANES:])
    o = _rms(o, sg_ref[...]) * (1.0 - lam_init)
    o_ref[...] = (o * _silu(g_ref[...])).astype(BF16)


def _attn(qkv, rest, lamv, sg, layer, B, L, tq, tk):
    T = B * L
    nq = L // tq
    assert tq == tk, "the causal block bookkeeping assumes square score blocks"
    lam_init = 0.8 - 0.6 * math.exp(-0.3 * layer)
    gate_blk = COL_BG * (GROUP_WIDTH // LANES)
    return pl.pallas_call(
        functools.partial(_attn_kernel, tq=tq, tk=tk, lam_init=lam_init),
        grid=(B, DA_HEADS, nq),
        in_specs=[
            pl.BlockSpec((None, 4, DA_HEAD_DIM), lambda b, h, i: (layer, 0, 0)),
            pl.BlockSpec((tq, LANES), lambda b, h, i: (b * nq + i, h)),
            pl.BlockSpec((L, LANES), lambda b, h, i: (b, DA_HEADS + h)),
            pl.BlockSpec((L, LANES), lambda b, h, i: (b, 2 * DA_HEADS + h)),
            pl.BlockSpec((tq, LANES), lambda b, h, i: (b * nq + i, gate_blk + h)),
            pl.BlockSpec((None, 1, LANES), lambda b, h, i: (layer, 0, 0)),
        ],
        out_specs=pl.BlockSpec((tq, LANES), lambda b, h, i: (b * nq + i, h)),
        out_shape=jax.ShapeDtypeStruct((T, GROUP_WIDTH), BF16),
        scratch_shapes=[
            pltpu.VMEM((2 * tq, LANES), BF16),
            pltpu.VMEM((2, 2 * tq, tk), F32),
            pltpu.VMEM((2 * tq, LANES), F32),
            pltpu.VMEM((2 * tq, 2 * LANES), F32),
        ],
        compiler_params=_params("parallel", "parallel", "arbitrary"),
        name="diffattn",
    )(lamv, qkv, qkv, qkv, rest, sg)


def _lru_kernel(x_ref, g_ref, cw_ref, cb_ref, wg_ref, bg_ref, lam_ref, o_ref,
                xe_ref, a_ref, b_ref, h_ref, carry_ref, *, chunk):
    c = pl.program_id(1)

    @pl.when(c == 0)
    def _():
        xe_ref[0:SUBLANES, :] = jnp.zeros((SUBLANES, GROUP_WIDTH), F32)
        carry_ref[...] = jnp.zeros(carry_ref.shape, F32)

    @pl.when(c > 0)
    def _():
        xe_ref[0:SUBLANES, :] = xe_ref[chunk:chunk + SUBLANES, :]

    x = x_ref[...]
    xe_ref[SUBLANES:SUBLANES + chunk, :] = x
    xc = (cw_ref[3:4, :] * x
          + cw_ref[2:3, :] * xe_ref[SUBLANES - 1:SUBLANES - 1 + chunk, :]
          + cw_ref[1:2, :] * xe_ref[SUBLANES - 2:SUBLANES - 2 + chunk, :]
          + cw_ref[0:1, :] * xe_ref[SUBLANES - 3:SUBLANES - 3 + chunk, :]
          + cb_ref[...])
    z = _dot(xc.astype(BF16), wg_ref[...]) + bg_ref[...]
    r = _sigmoid(z[:, :GROUP_WIDTH])
    gate_i = _sigmoid(z[:, GROUP_WIDTH:])
    nl = -lam_ref[...]
    softplus = jnp.maximum(nl, 0.0) + jnp.log1p(jnp.exp(-jnp.abs(nl)))
    log_a = (-LRU_C) * r * softplus
    a = jnp.exp(log_a)
    a_ref[...] = a
    mult = jnp.sqrt(-jnp.tanh(log_a) * (1.0 + a * a))
    b_ref[...] = mult * (gate_i * xc)

    row = lax.broadcasted_iota(jnp.int32, (SUBLANES, LANES), 0)

    def tile_body(t, carry):
        rows = pl.ds(pl.multiple_of(t * SUBLANES, SUBLANES), SUBLANES)
        for j in range(GROUP_WIDTH // LANES):
            cols = slice(j * LANES, (j + 1) * LANES)
            a = a_ref[rows, cols]
            b = b_ref[rows, cols]
            b = b + jnp.where(row == 0, a * carry_ref[:, cols], 0.0)
            for shift in (1, 2, 4):
                keep = row >= shift
                b = b + jnp.where(keep, a * pltpu.roll(b, shift, axis=0), 0.0)
                if shift != 4:
                    a = jnp.where(keep, a * pltpu.roll(a, shift, axis=0), a)
            h_ref[rows, cols] = b
            carry_ref[:, cols] = jnp.broadcast_to(b[SUBLANES - 1:, :], (SUBLANES, LANES))
        return carry

    lax.fori_loop(0, chunk // SUBLANES, tile_body, 0)
    o_ref[...] = (h_ref[...] * _silu(g_ref[...])).astype(BF16)


def _lru(rest, cw, cb, wg, bg, lam, layer, B, L, chunk):
    T = B * L
    nc = L // chunk
    return pl.pallas_call(
        functools.partial(_lru_kernel, chunk=chunk),
        grid=(B, nc),
        in_specs=[
            pl.BlockSpec((chunk, GROUP_WIDTH), lambda b, c: (b * nc + c, COL_CX)),
            pl.BlockSpec((chunk, GROUP_WIDTH), lambda b, c: (b * nc + c, COL_CG)),
            pl.BlockSpec((None, 4, GROUP_WIDTH), lambda b, c: (layer, 0, 0)),
            pl.BlockSpec((None, 1, GROUP_WIDTH), lambda b, c: (layer, 0, 0)),
            pl.BlockSpec((None, GROUP_WIDTH, 2 * GROUP_WIDTH), lambda b, c: (layer, 0, 0)),
            pl.BlockSpec((None, 1, 2 * GROUP_WIDTH), lambda b, c: (layer, 0, 0)),
            pl.BlockSpec((None, 1, GROUP_WIDTH), lambda b, c: (layer, 0, 0)),
        ],
        out_specs=pl.BlockSpec((chunk, GROUP_WIDTH), lambda b, c: (b * nc + c, 0)),
        out_shape=jax.ShapeDtypeStruct((T, GROUP_WIDTH), BF16),
        scratch_shapes=[
            pltpu.VMEM((chunk + SUBLANES, GROUP_WIDTH), F32),
            pltpu.VMEM((chunk, GROUP_WIDTH), F32),
            pltpu.VMEM((chunk, GROUP_WIDTH), F32),
            pltpu.VMEM((chunk, GROUP_WIDTH), F32),
            pltpu.VMEM((SUBLANES, GROUP_WIDTH), F32),
        ],
        compiler_params=_params("arbitrary", "arbitrary"),
        name="rglru",
    )(rest, rest, cw, cb, wg, bg, lam)


def _memkv_kernel(mem_ref, g_ref, w_ref, o_ref):
    o_ref[...] = _dot(_rms(mem_ref[...], g_ref[...]).astype(BF16), w_ref[...]).astype(BF16)


def _memkv(mem2d, g, w, B, M):
    depth, D, n_kv = w.shape
    return pl.pallas_call(
        _memkv_kernel,
        grid=(depth, B),
        in_specs=[
            pl.BlockSpec((M, D), lambda l, b: (b, 0)),
            pl.BlockSpec((None, 1, D), lambda l, b: (l, 0, 0)),
            pl.BlockSpec((None, D, n_kv), lambda l, b: (l, 0, 0)),
        ],
        out_specs=pl.BlockSpec((None, M, n_kv), lambda l, b: (l, b, 0)),
        out_shape=jax.ShapeDtypeStruct((depth, B * M, n_kv), BF16),
        compiler_params=_params("parallel", "parallel"),
        name="memkv",
    )(mem2d, g, w)


def _memattn_kernel(q_ref, g_ref, kv_ref, o_ref):
    q = q_ref[...].astype(BF16)
    scale = MEM_HEAD_DIM ** -0.5
    outs = []
    for h in range(MEM_HEADS):
        cols = slice(h * MEM_HEAD_DIM, (h + 1) * MEM_HEAD_DIM)
        k = kv_ref[:, h * MEM_HEAD_DIM:(h + 1) * MEM_HEAD_DIM]
        v = kv_ref[:, GROUP_WIDTH + h * MEM_HEAD_DIM:GROUP_WIDTH + (h + 1) * MEM_HEAD_DIM]
        s = _dot_nt(q[:, cols], k) * scale
        e = jnp.exp(s - jnp.max(s, axis=-1, keepdims=True))
        p = e / jnp.sum(e, axis=-1, keepdims=True)
        outs.append(_dot(p.astype(BF16), v))
    o_ref[...] = (jnp.concatenate(outs, axis=1) * _silu(g_ref[...])).astype(BF16)


def _memattn(rest, kv, layer, B, L, M, tm):
    T = B * L
    per_b = L // tm
    return pl.pallas_call(
        _memattn_kernel,
        grid=(T // tm,),
        in_specs=[
            pl.BlockSpec((tm, GROUP_WIDTH), lambda i: (i, COL_MQ)),
            pl.BlockSpec((tm, GROUP_WIDTH), lambda i: (i, COL_MG)),
            pl.BlockSpec((None, M, 2 * GROUP_WIDTH), lambda i: (layer, i // per_b, 0)),
        ],
        out_specs=pl.BlockSpec((tm, GROUP_WIDTH), lambda i: (i, 0)),
        out_shape=jax.ShapeDtypeStruct((T, GROUP_WIDTH), BF16),
        compiler_params=_params("parallel"),
        name="memattn",
    )(rest, rest, kv)


def _outproj_kernel(ya_ref, yb_ref, yc_ref, ym_ref, w_ref, x_ref, fg_ref, o_ref, *, final):
    acc = x_ref[...]
    for n, y_ref in enumerate((ya_ref, yb_ref, yc_ref, ym_ref)):
        acc = acc + _dot(y_ref[...], w_ref[n * GROUP_WIDTH:(n + 1) * GROUP_WIDTH, :])
    o_ref[...] = _rms(acc, fg_ref[...]) if final else acc


def _outproj(ys, w_all, layer, x, fg, final, tm):
    T, D = x.shape
    y_spec = pl.BlockSpec((tm, GROUP_WIDTH), lambda i: (i, 0))
    return pl.pallas_call(
        functools.partial(_outproj_kernel, final=final),
        grid=(T // tm,),
        in_specs=[y_spec, y_spec, y_spec, y_spec,
                  pl.BlockSpec((None,) + w_all.shape[1:], lambda i: (layer, 0, 0)),
                  pl.BlockSpec((tm, D), lambda i: (i, 0)),
                  pl.BlockSpec((1, D), lambda i: (0, 0))],
        out_specs=pl.BlockSpec((tm, D), lambda i: (i, 0)),
        out_shape=jax.ShapeDtypeStruct((T, D), F32),
        compiler_params=_params("parallel"),
        name="outproj",
    )(*ys, w_all, x, fg)


def _block_diag(blocks):
    n = blocks.shape[-3]
    eye = jnp.eye(n, dtype=blocks.dtype)
    out = jnp.einsum('...nrc,nm->...nrmc', blocks, eye)
    return out.reshape(blocks.shape[:-3] + (n * blocks.shape[-2], n * blocks.shape[-1]))


def _s5_prep(lam_re, lam_im, log_dt, b_re, b_im, c_re, c_im):
    depth = lam_re.shape[0]
    dt = jnp.exp(log_dt)[..., None]
    mag = jnp.exp(lam_re * dt)
    abar_re = mag * jnp.cos(lam_im * dt)
    abar_im = mag * jnp.sin(lam_im * dt)
    den = lam_re * lam_re + lam_im * lam_im
    nr, ni = abar_re - 1.0, abar_im
    f_re = (nr * lam_re + ni * lam_im) / den
    f_im = (ni * lam_re - nr * lam_im) / den
    bb_re = f_re[..., None] * b_re - f_im[..., None] * b_im
    bb_im = f_re[..., None] * b_im + f_im[..., None] * b_re
    gpb = LANES // S5_CH

    def in_blocks(bb):
        return _block_diag(jnp.swapaxes(bb, -1, -2).reshape(depth, S5_BLOCKS, gpb, S5_CH, S5_STATE))

    def out_blocks(cc):
        return _block_diag(jnp.swapaxes(cc, -1, -2).reshape(depth, S5_BLOCKS, gpb, S5_STATE, S5_CH))

    wb = jnp.concatenate([in_blocks(bb_re), in_blocks(bb_im)], axis=-1).astype(BF16)
    wc = jnp.concatenate([out_blocks(c_re), out_blocks(-c_im)], axis=-2).astype(BF16)

    ar = abar_re.reshape(depth, 1, -1)
    ai = abar_im.reshape(depth, 1, -1)
    a2r, a2i = ar * ar - ai * ai, 2.0 * ar * ai
    a4r, a4i = a2r * a2r - a2i * a2i, 2.0 * a2r * a2i
    row = jnp.arange(SUBLANES)[None, :, None]

    def tile(v, keep):
        return jnp.where(keep, jnp.broadcast_to(v, (depth, SUBLANES, v.shape[-1])), 0.0)

    coef = jnp.stack([tile(ar, row == 0), tile(ai, row == 0),
                      tile(ar, row >= 1), tile(ai, row >= 1),
                      tile(a2r, row >= 2), tile(a2i, row >= 2),
                      tile(a4r, row >= 4), tile(a4i, row >= 4)], axis=1)
    return wb, wc, coef.astype(F32)


def _rope_tables(positions):
    inv_freq = ROPE_THETA ** (-jnp.arange(0, ROPE_DIMS, 2, dtype=F32) / ROPE_DIMS)
    ang = positions.astype(F32).reshape(-1, 1) * inv_freq
    cos, sin = jnp.cos(ang), jnp.sin(ang)
    half = ROPE_DIMS // 2
    pad = DA_HEAD_DIM - ROPE_DIMS
    n = cos.shape[0]
    ones, zeros = jnp.ones((n, pad), F32), jnp.zeros((n, pad), F32)
    zh = jnp.zeros((n, half), F32)
    rc = jnp.concatenate([cos, cos, ones], axis=1)
    rs1 = jnp.concatenate([-sin, zh, zeros], axis=1)
    rs2 = jnp.concatenate([zh, sin, zeros], axis=1)
    rep = LANES // DA_HEAD_DIM
    return jnp.tile(rc, (1, rep)), jnp.tile(rs1, (1, rep)), jnp.tile(rs2, (1, rep))


def _tiles(L):
    return dict(tm=min(256, L), chunk=min(256, L), tq=min(512, L), tk=min(512, L), tmem=min(512, L))


def kernel(x, mem, positions, norm_g, w_in, w_out, s5_lambda_re, s5_lambda_im, s5_log_dt, s5_b_re, s5_b_im, s5_c_re, s5_c_im, s5_d, s5_w_glu, da_lambda_q1, da_lambda_k1, da_lambda_q2, da_lambda_k2, da_subln_g, lru_conv_w, lru_conv_b, lru_w_a, lru_b_a, lru_w_x, lru_b_x, lru_lambda, mem_norm_g, w_mem_kv, final_norm_g):
    B, L, D = x.shape
    M = mem.shape[1]
    depth = w_in.shape[0]
    T = B * L
    t = _tiles(L)
    assert L % t["tk"] == 0 and t["tk"] % t["tq"] == 0 and L % t["tmem"] == 0

    gw = GROUP_WIDTH
    cols = [w_in[..., n * gw:(n + 1) * gw] for n in range(10)]
    a_u, a_g, qd, kd, vd, b_g, c_x, c_g, m_q, m_g = cols
    w_in_b = jnp.concatenate([a_u, a_g, b_g, c_x, c_g, m_q, m_g,
                              qd * (DA_HEAD_DIM ** -0.5), kd, vd], axis=-1).astype(BF16)
    w_out_b = w_out.astype(BF16)
    wglu_b = s5_w_glu.astype(BF16)
    wb, wc, coef = _s5_prep(s5_lambda_re, s5_lambda_im, s5_log_dt, s5_b_re, s5_b_im, s5_c_re, s5_c_im)
    s5_d2 = s5_d.reshape(depth, 1, gw)
    lamv = jnp.stack([da_lambda_q1, da_lambda_k1, da_lambda_q2, da_lambda_k2], axis=1)
    sg = da_subln_g.reshape(depth, 1, LANES)
    wg = jnp.concatenate([_block_diag(lru_w_a), _block_diag(lru_w_x)], axis=-1).astype(BF16)
    bg = jnp.concatenate([lru_b_a, lru_b_x], axis=-1).reshape(depth, 1, 2 * gw)
    lru_cb = lru_conv_b.reshape(depth, 1, gw)
    lru_lam = lru_lambda.reshape(depth, 1, gw)
    norm_g3 = norm_g.reshape(depth, 1, D)
    rc, rs1, rs2 = _rope_tables(positions)

    kv = _memkv(mem.reshape(B * M, D), mem_norm_g.reshape(depth, 1, D), w_mem_kv.astype(BF16), B, M)

    xf = x.reshape(T, D)
    fg = final_norm_g.reshape(1, D)
    for layer in range(depth):
        rest, qkv = _inproj(xf, norm_g3, w_in_b, layer, rc, rs1, rs2, t["tm"])
        y_a = _s5(rest, wb, coef, wc, s5_d2, wglu_b, layer, B, L, t["chunk"])
        y_b = _attn(qkv, rest, lamv, sg, layer, B, L, t["tq"], t["tk"])
        y_c = _lru(rest, lru_conv_w, lru_cb, wg, bg, lru_lam, layer, B, L, t["chunk"])
        y_m = _memattn(rest, kv, layer, B, L, M, t["tmem"])
        xf = _outproj((y_a, y_b, y_c, y_m), w_out_b, layer, xf, fg, layer == depth - 1, t["tm"])
    return xf.reshape(B, L, D)
```

```python
import functools
import math

import jax
import jax.numpy as jnp
from jax import lax
from jax.experimental import pallas as pl
from jax.experimental.pallas import tpu as pltpu

F32 = jnp.float32
BF16 = jnp.bfloat16

EPS = 1e-6
GROUP_WIDTH = 512
LANES = 128
SUBLANES = 8
S5_STATE = 64
S5_CH = 16
S5_BLOCKS = 4
S5_BLOCK_STATES = 512
DA_HEADS = 4
DA_HEAD_DIM = 64
ROPE_DIMS = 16
ROPE_THETA = 500000.0
LRU_BLOCKS = 8
LRU_C = 8.0
MEM_HEADS = 4
MEM_HEAD_DIM = 128
NEG_BIG = -1e30
LOG2E = math.log2(math.e)
Q_SCALE = DA_HEAD_DIM ** -0.5 * LOG2E
VMEM_LIMIT_BYTES = 56 * 1024 * 1024

W_COL_AU = 0
W_COL_Q = 2 * GROUP_WIDTH
W_COL_REST = 5 * GROUP_WIDTH
COL_BG, COL_CX, COL_CG, COL_MQ, COL_MG = range(5)
N_REST = 5 * GROUP_WIDTH


def _params(*sem):
    return pltpu.CompilerParams(dimension_semantics=sem, vmem_limit_bytes=VMEM_LIMIT_BYTES)


def _sigmoid(x):
    return 0.5 * jnp.tanh(0.5 * x) + 0.5


def _silu(x):
    return x * _sigmoid(x)


def _gelu_tanh(x):
    return 0.5 * x * (1.0 + jnp.tanh(math.sqrt(2.0 / math.pi) * (x + 0.044715 * (x * x * x))))


def _dot(a, b):
    return jnp.dot(a, b, preferred_element_type=F32)


def _dot_nt(a, b):
    return lax.dot_general(a, b, (((1,), (1,)), ((), ())), preferred_element_type=F32)


def _rms(x, g):
    ms = jnp.mean(x * x, axis=-1, keepdims=True)
    return x * lax.rsqrt(ms + EPS) * g


def _s5_input(u, wb_ref, bu_ref):
    ub = u.astype(BF16)
    blk = 2 * S5_BLOCK_STATES
    for q in range(S5_BLOCKS):
        bu_ref[:, q * blk:(q + 1) * blk] = _dot(ub[:, q * LANES:(q + 1) * LANES], wb_ref[q])


def _s5_scan_out(u, gate, coef_ref, wc_ref, d_ref, wglu_ref, bu_ref, xs_ref, carry_ref, chunk):
    blk = 2 * S5_BLOCK_STATES

    def cmul_add(br, bi, ar, ai, sr, si):
        return br + (ar * sr - ai * si), bi + (ar * si + ai * sr)

    for r in range(chunk // SUBLANES):
        rows = slice(r * SUBLANES, (r + 1) * SUBLANES)
        for q in range(S5_BLOCKS):
            for j in range(S5_BLOCK_STATES // LANES):
                lr = q * blk + j * LANES
                li = lr + S5_BLOCK_STATES
                lc = q * S5_BLOCK_STATES + j * LANES
                br = bu_ref[rows, lr:lr + LANES]
                bi = bu_ref[rows, li:li + LANES]
                br, bi = cmul_add(br, bi, coef_ref[0, :, lc:lc + LANES], coef_ref[1, :, lc:lc + LANES],
                                  carry_ref[:, lr:lr + LANES], carry_ref[:, li:li + LANES])
                for k, shift in ((1, 1), (2, 2), (3, 4)):
                    sr = pltpu.roll(br, shift, axis=0)
                    si = pltpu.roll(bi, shift, axis=0)
                    br, bi = cmul_add(br, bi, coef_ref[2 * k, :, lc:lc + LANES],
                                      coef_ref[2 * k + 1, :, lc:lc + LANES], sr, si)
                xs_ref[rows, lr:lr + LANES] = br
                xs_ref[rows, li:li + LANES] = bi
                carry_ref[:, lr:lr + LANES] = jnp.broadcast_to(br[SUBLANES - 1:, :], (SUBLANES, LANES))
                carry_ref[:, li:li + LANES] = jnp.broadcast_to(bi[SUBLANES - 1:, :], (SUBLANES, LANES))

    ys = []
    for q in range(S5_BLOCKS):
        xb = xs_ref[:, q * blk:(q + 1) * blk].astype(BF16)
        ys.append(_dot(xb, wc_ref[q]))
    y = jnp.concatenate(ys, axis=1) + d_ref[...] * u
    y = _gelu_tanh(y)
    z = _dot(y.astype(BF16), wglu_ref[...])
    return z[:, :GROUP_WIDTH] * _sigmoid(z[:, GROUP_WIDTH:]) * _silu(gate)


def _inproj_s5_kernel(x_ref, g_ref, w_ref, rc_ref, rs1_ref, rs2_ref,
                      wb_ref, coef_ref, wc_ref, d_ref, wglu_ref,
                      rest_ref, qkv_ref, ya_ref, bu_ref, xs_ref, carry_ref, *, chunk):
    @pl.when(pl.program_id(1) == 0)
    def _():
        carry_ref[...] = jnp.zeros(carry_ref.shape, F32)

    gw = GROUP_WIDTH
    hb = _rms(x_ref[...], g_ref[...]).astype(BF16)
    ug = _dot(hb, w_ref[:, W_COL_AU:W_COL_AU + 2 * gw])
    _s5_input(ug[:, :gw], wb_ref, bu_ref)

    rest_ref[...] = _dot(hb, w_ref[:, W_COL_REST:W_COL_REST + N_REST])
    qk = _dot(hb, w_ref[:, W_COL_Q:W_COL_Q + 2 * gw])
    c, s1, s2 = rc_ref[...], rs1_ref[...], rs2_ref[...]
    for j in range(2 * gw // LANES):
        t = qk[:, j * LANES:(j + 1) * LANES]
        t = t * c + pltpu.roll(t, LANES - ROPE_DIMS // 2, axis=1) * s1 + pltpu.roll(t, ROPE_DIMS // 2, axis=1) * s2
        if j < gw // LANES:
            t = t * Q_SCALE
        qkv_ref[:, j * LANES:(j + 1) * LANES] = t.astype(BF16)
    qkv_ref[:, 2 * gw:] = _dot(hb, w_ref[:, W_COL_Q + 2 * gw:W_COL_Q + 3 * gw]).astype(BF16)

    ya_ref[...] = _s5_scan_out(ug[:, :gw], ug[:, gw:], coef_ref, wc_ref, d_ref, wglu_ref,
                               bu_ref, xs_ref, carry_ref, chunk).astype(BF16)


def _inproj_s5(x, g, w_all, layer, rc, rs1, rs2, wb, coef, wc, d, wglu, B, L, chunk):
    T, D = x.shape
    n_in = w_all.shape[-1]
    nc = L // chunk
    n_state = S5_BLOCKS * 2 * S5_BLOCK_STATES

    def row_blk(b, c):
        return (b * nc + c, 0)

    return pl.pallas_call(
        functools.partial(_inproj_s5_kernel, chunk=chunk),
        grid=(B, nc),
        in_specs=[
            pl.BlockSpec((chunk, D), row_blk),
            pl.BlockSpec((None, 1, D), lambda b, c: (layer, 0, 0)),
            pl.BlockSpec((None, D, n_in), lambda b, c: (layer, 0, 0)),
            pl.BlockSpec((chunk, LANES), row_blk),
            pl.BlockSpec((chunk, LANES), row_blk),
            pl.BlockSpec((chunk, LANES), row_blk),
            pl.BlockSpec((None,) + wb.shape[1:], lambda b, c: (layer, 0, 0, 0)),
            pl.BlockSpec((None,) + coef.shape[1:], lambda b, c: (layer, 0, 0, 0)),
            pl.BlockSpec((None,) + wc.shape[1:], lambda b, c: (layer, 0, 0, 0)),
            pl.BlockSpec((None, 1, GROUP_WIDTH), lambda b, c: (layer, 0, 0)),
            pl.BlockSpec((None,) + wglu.shape[1:], lambda b, c: (layer, 0, 0)),
        ],
        out_specs=[
            pl.BlockSpec((chunk, N_REST), row_blk),
            pl.BlockSpec((chunk, 3 * GROUP_WIDTH), row_blk),
            pl.BlockSpec((chunk, GROUP_WIDTH), row_blk),
        ],
        out_shape=[jax.ShapeDtypeStruct((T, N_REST), F32),
                   jax.ShapeDtypeStruct((T, 3 * GROUP_WIDTH), BF16),
                   jax.ShapeDtypeStruct((T, GROUP_WIDTH), BF16)],
        scratch_shapes=[pltpu.VMEM((chunk, n_state), F32),
                        pltpu.VMEM((chunk, n_state), F32),
                        pltpu.VMEM((SUBLANES, n_state), F32)],
        compiler_params=_params("arbitrary", "arbitrary"),
        name="inproj_s5",
    )(x, g, w_all, rc, rs1, rs2, wb, coef, wc, d, wglu)


def _attn_kernel(lamv_ref, q_ref, k_ref, v_ref, g_ref, sg_ref, o_ref,
                 qs_ref, s_ref, m_ref, acc_ref, *, tq, tk, lam_init):
    i = pl.program_id(2)
    lane = lax.broadcasted_iota(jnp.int32, (tq, LANES), 1)
    for st in range(2):
        q = q_ref[:, st * LANES:(st + 1) * LANES]
        zero = jnp.zeros_like(q)
        qs_ref[st, 0:tq, :] = jnp.where(lane < DA_HEAD_DIM, q, zero)
        qs_ref[st, tq:, :] = jnp.where(lane >= DA_HEAD_DIM, q, zero)
    m_ref[...] = jnp.full(m_ref.shape, NEG_BIG, F32)
    acc_ref[...] = jnp.zeros(acc_ref.shape, F32)
    ones = jnp.ones((tk, LANES), BF16)

    def scores(st, kb):
        k = k_ref[pl.ds(pl.multiple_of(kb * tk, tk), tk), st * LANES:(st + 1) * LANES]
        s_ref[st] = _dot_nt(qs_ref[st], k)

    def softmax_pv(st, kb, masked):
        kstart = pl.multiple_of(kb * tk, tk)
        v = jnp.concatenate([v_ref[pl.ds(kstart, tk), st * LANES:(st + 1) * LANES], ones], axis=1)
        s = s_ref[st]
        if masked:
            row = lax.broadcasted_iota(jnp.int32, s.shape, 0)
            col = lax.broadcasted_iota(jnp.int32, s.shape, 1)
            qpos = i * tq + jnp.where(row >= tq, row - tq, row)
            s = jnp.where(kstart + col <= qpos, s, NEG_BIG)
        m_prev = m_ref[st]
        m_new = jnp.maximum(m_prev, jnp.max(s, axis=-1, keepdims=True))
        alpha = jnp.exp2(m_prev - m_new)
        p = jnp.exp2(s - jnp.concatenate([m_new] * (tk // LANES), axis=1))
        acc_ref[st] = jnp.concatenate([alpha, alpha], axis=1) * acc_ref[st] + _dot(p.astype(BF16), v)
        m_ref[st] = m_new

    scores(0, i)
    scores(1, i)
    softmax_pv(0, i, True)
    scores(0, 0)
    softmax_pv(1, i, True)

    def body(kb, carry):
        scores(1, kb)
        softmax_pv(0, kb, False)
        scores(0, kb + 1)
        softmax_pv(1, kb, False)
        return carry

    lax.fori_loop(0, i, body, 0)

    lv = lamv_ref[...]
    lam = (jnp.exp(jnp.sum(lv[0:1, :] * lv[1:2, :], axis=-1, keepdims=True))
           - jnp.exp(jnp.sum(lv[2:3, :] * lv[3:4, :], axis=-1, keepdims=True)) + lam_init)
    for st in range(2):
        acc = acc_ref[st]
        o = (acc[:tq, :LANES] / acc[:tq, LANES:]) - lam * (acc[tq:, :LANES] / acc[tq:, LANES:])
        o = _rms(o, sg_ref[...]) * (1.0 - lam_init)
        cols = slice(st * LANES, (st + 1) * LANES)
        o_ref[:, cols] = (o * _silu(g_ref[:, cols])).astype(BF16)


def _attn(qkv, rest, lamv, sg, layer, B, L, tq, tk):
    T = B * L
    nq = L // tq
    assert tq == tk, "the causal block bookkeeping assumes square score blocks"
    lam_init = 0.8 - 0.6 * math.exp(-0.3 * layer)
    pair = 2 * LANES
    n_pair = GROUP_WIDTH // pair
    gate_blk = COL_BG * n_pair
    return pl.pallas_call(
        functools.partial(_attn_kernel, tq=tq, tk=tk, lam_init=lam_init),
        grid=(B, n_pair, nq),
        in_specs=[
            pl.BlockSpec((None, 4, DA_HEAD_DIM), lambda b, h, i: (layer, 0, 0)),
            pl.BlockSpec((tq, pair), lambda b, h, i: (b * nq + i, h)),
            pl.BlockSpec((L, pair), lambda b, h, i: (b, n_pair + h)),
            pl.BlockSpec((L, pair), lambda b, h, i: (b, 2 * n_pair + h)),
            pl.BlockSpec((tq, pair), lambda b, h, i: (b * nq + i, gate_blk + h)),
            pl.BlockSpec((None, 1, LANES), lambda b, h, i: (layer, 0, 0)),
        ],
        out_specs=pl.BlockSpec((tq, pair), lambda b, h, i: (b * nq + i, h)),
        out_shape=jax.ShapeDtypeStruct((T, GROUP_WIDTH), BF16),
        scratch_shapes=[
            pltpu.VMEM((2, 2 * tq, LANES), BF16),
            pltpu.VMEM((2, 2 * tq, tk), F32),
            pltpu.VMEM((2, 2 * tq, LANES), F32),
            pltpu.VMEM((2, 2 * tq, 2 * LANES), F32),
        ],
        compiler_params=_params("parallel", "parallel", "arbitrary"),
        name="diffattn",
    )(lamv, qkv, qkv, qkv, rest, sg)


def _memkv_kernel(mem_ref, g_ref, w_ref, o_ref):
    o_ref[...] = _dot(_rms(mem_ref[...], g_ref[...]).astype(BF16), w_ref[...]).astype(BF16)


def _memkv(mem2d, g, w, B, M):
    depth, D, n_kv = w.shape
    return pl.pallas_call(
        _memkv_kernel,
        grid=(depth, B),
        in_specs=[
            pl.BlockSpec((M, D), lambda l, b: (b, 0)),
            pl.BlockSpec((None, 1, D), lambda l, b: (l, 0, 0)),
            pl.BlockSpec((None, D, n_kv), lambda l, b: (l, 0, 0)),
        ],
        out_specs=pl.BlockSpec((None, M, n_kv), lambda l, b: (l, b, 0)),
        out_shape=jax.ShapeDtypeStruct((depth, B * M, n_kv), BF16),
        compiler_params=_params("parallel", "parallel"),
        name="memkv",
    )(mem2d, g, w)


def _lru_gates(x, cw_ref, cb_ref, wg_ref, bg_ref, xe_ref, chunk):
    xe_ref[SUBLANES:SUBLANES + chunk, :] = x
    xc = (cw_ref[3:4, :] * x
          + cw_ref[2:3, :] * xe_ref[SUBLANES - 1:SUBLANES - 1 + chunk, :]
          + cw_ref[1:2, :] * xe_ref[SUBLANES - 2:SUBLANES - 2 + chunk, :]
          + cw_ref[0:1, :] * xe_ref[SUBLANES - 3:SUBLANES - 3 + chunk, :]
          + cb_ref[...])
    return xc, _dot(xc.astype(BF16), wg_ref[...]) + bg_ref[...]


def _lru_scan(xc, z, gate, lam_ref, a_ref, b_ref, h_ref, carry_ref, chunk):
    r = _sigmoid(z[:, :GROUP_WIDTH])
    gate_i = _sigmoid(z[:, GROUP_WIDTH:])
    nl = -lam_ref[...]
    softplus = jnp.maximum(nl, 0.0) + jnp.log1p(jnp.exp(-jnp.abs(nl)))
    log_a = (-LRU_C) * r * softplus
    a = jnp.exp(log_a)
    a_ref[...] = a
    mult = jnp.sqrt(-jnp.tanh(log_a) * (1.0 + a * a))
    b_ref[...] = mult * (gate_i * xc)

    row = lax.broadcasted_iota(jnp.int32, (SUBLANES, LANES), 0)
    for t in range(chunk // SUBLANES):
        rows = slice(t * SUBLANES, (t + 1) * SUBLANES)
        for j in range(GROUP_WIDTH // LANES):
            cols = slice(j * LANES, (j + 1) * LANES)
            a = a_ref[rows, cols]
            b = b_ref[rows, cols]
            b = b + jnp.where(row == 0, a * carry_ref[:, cols], 0.0)
            for shift in (1, 2, 4):
                keep = row >= shift
                b = b + jnp.where(keep, a * pltpu.roll(b, shift, axis=0), 0.0)
                if shift != 4:
                    a = jnp.where(keep, a * pltpu.roll(a, shift, axis=0), a)
            h_ref[rows, cols] = b
            carry_ref[:, cols] = jnp.broadcast_to(b[SUBLANES - 1:, :], (SUBLANES, LANES))
    return h_ref[...] * _silu(gate)


def _memattn_body(q, gate, kv_ref):
    qb = q.astype(BF16)
    scale = MEM_HEAD_DIM ** -0.5
    outs = []
    for h in range(MEM_HEADS):
        cols = slice(h * MEM_HEAD_DIM, (h + 1) * MEM_HEAD_DIM)
        k = kv_ref[:, h * MEM_HEAD_DIM:(h + 1) * MEM_HEAD_DIM]
        v = kv_ref[:, GROUP_WIDTH + h * MEM_HEAD_DIM:GROUP_WIDTH + (h + 1) * MEM_HEAD_DIM]
        s = _dot_nt(qb[:, cols], k) * scale
        e = jnp.exp(s - jnp.max(s, axis=-1, keepdims=True))
        p = e / jnp.sum(e, axis=-1, keepdims=True)
        outs.append(_dot(p.astype(BF16), v))
    return jnp.concatenate(outs, axis=1) * _silu(gate)


def _tail_kernel(cx_ref, cg_ref, mq_ref, mg_ref, ya_ref, yb_ref, x_ref, kv_ref,
                 cw_ref, cb_ref, wg_ref, bg_ref, lam_ref, w_ref, fg_ref, o_ref,
                 xe_ref, a_ref, b_ref, h_ref, carry_ref, *, chunk, final):
    c = pl.program_id(1)

    @pl.when(c == 0)
    def _():
        xe_ref[0:SUBLANES, :] = jnp.zeros((SUBLANES, GROUP_WIDTH), F32)
        carry_ref[...] = jnp.zeros(carry_ref.shape, F32)

    @pl.when(c > 0)
    def _():
        xe_ref[0:SUBLANES, :] = xe_ref[chunk:chunk + SUBLANES, :]

    gw = GROUP_WIDTH
    xc, z = _lru_gates(cx_ref[...], cw_ref, cb_ref, wg_ref, bg_ref, xe_ref, chunk)
    acc = x_ref[...] + _dot(ya_ref[...], w_ref[0:gw, :]) + _dot(yb_ref[...], w_ref[gw:2 * gw, :])
    y_m = _memattn_body(mq_ref[...], mg_ref[...], kv_ref)
    acc = acc + _dot(y_m.astype(BF16), w_ref[3 * gw:4 * gw, :])
    y_c = _lru_scan(xc, z, cg_ref[...], lam_ref, a_ref, b_ref, h_ref, carry_ref, chunk)
    acc = acc + _dot(y_c.astype(BF16), w_ref[2 * gw:3 * gw, :])
    o_ref[...] = _rms(acc, fg_ref[...]) if final else acc


def _tail(rest, y_a, y_b, x, kv, cw, cb, wg, bg, lam, w_all, fg, layer, final, B, L, M, chunk):
    T, D = x.shape
    nc = L // chunk
    gw = GROUP_WIDTH

    def row_blk(b, c):
        return (b * nc + c, 0)

    def rest_spec(col):
        return pl.BlockSpec((chunk, gw), lambda b, c: (b * nc + c, col))

    def per_layer(shape):
        return pl.BlockSpec((None,) + shape, lambda b, c: (layer,) + (0,) * len(shape))

    return pl.pallas_call(
        functools.partial(_tail_kernel, chunk=chunk, final=final),
        grid=(B, nc),
        in_specs=[
            rest_spec(COL_CX), rest_spec(COL_CG), rest_spec(COL_MQ), rest_spec(COL_MG),
            pl.BlockSpec((chunk, gw), row_blk),
            pl.BlockSpec((chunk, gw), row_blk),
            pl.BlockSpec((chunk, D), row_blk),
            pl.BlockSpec((None, M, 2 * gw), lambda b, c: (layer, b, 0)),
            per_layer((4, gw)), per_layer((1, gw)), per_layer((gw, 2 * gw)), per_layer((1, 2 * gw)),
            per_layer((1, gw)), per_layer(w_all.shape[1:]),
            pl.BlockSpec((1, D), lambda b, c: (0, 0)),
        ],
        out_specs=pl.BlockSpec((chunk, D), row_blk),
        out_shape=jax.ShapeDtypeStruct((T, D), F32),
        scratch_shapes=[
            pltpu.VMEM((chunk + SUBLANES, gw), F32),
            pltpu.VMEM((chunk, gw), F32),
            pltpu.VMEM((chunk, gw), F32),
            pltpu.VMEM((chunk, gw), F32),
            pltpu.VMEM((SUBLANES, gw), F32),
        ],
        compiler_params=_params("arbitrary", "arbitrary"),
        name="tail",
    )(rest, rest, rest, rest, y_a, y_b, x, kv, cw, cb, wg, bg, lam, w_all, fg)


def _block_diag(blocks):
    n = blocks.shape[-3]
    eye = jnp.eye(n, dtype=blocks.dtype)
    out = jnp.einsum('...nrc,nm->...nrmc', blocks, eye)
    return out.reshape(blocks.shape[:-3] + (n * blocks.shape[-2], n * blocks.shape[-1]))


def _s5_prep(lam_re, lam_im, log_dt, b_re, b_im, c_re, c_im):
    depth = lam_re.shape[0]
    dt = jnp.exp(log_dt)[..., None]
    mag = jnp.exp(lam_re * dt)
    abar_re = mag * jnp.cos(lam_im * dt)
    abar_im = mag * jnp.sin(lam_im * dt)
    den = lam_re * lam_re + lam_im * lam_im
    nr, ni = abar_re - 1.0, abar_im
    f_re = (nr * lam_re + ni * lam_im) / den
    f_im = (ni * lam_re - nr * lam_im) / den
    bb_re = f_re[..., None] * b_re - f_im[..., None] * b_im
    bb_im = f_re[..., None] * b_im + f_im[..., None] * b_re
    gpb = LANES // S5_CH

    def in_blocks(bb):
        return _block_diag(jnp.swapaxes(bb, -1, -2).reshape(depth, S5_BLOCKS, gpb, S5_CH, S5_STATE))

    def out_blocks(cc):
        return _block_diag(jnp.swapaxes(cc, -1, -2).reshape(depth, S5_BLOCKS, gpb, S5_STATE, S5_CH))

    wb = jnp.concatenate([in_blocks(bb_re), in_blocks(bb_im)], axis=-1).astype(BF16)
    wc = jnp.concatenate([out_blocks(c_re), out_blocks(-c_im)], axis=-2).astype(BF16)

    ar = abar_re.reshape(depth, 1, -1)
    ai = abar_im.reshape(depth, 1, -1)
    a2r, a2i = ar * ar - ai * ai, 2.0 * ar * ai
    a4r, a4i = a2r * a2r - a2i * a2i, 2.0 * a2r * a2i
    row = jnp.arange(SUBLANES)[None, :, None]

    def tile(v, keep):
        return jnp.where(keep, jnp.broadcast_to(v, (depth, SUBLANES, v.shape[-1])), 0.0)

    coef = jnp.stack([tile(ar, row == 0), tile(ai, row == 0),
                      tile(ar, row >= 1), tile(ai, row >= 1),
                      tile(a2r, row >= 2), tile(a2i, row >= 2),
                      tile(a4r, row >= 4), tile(a4i, row >= 4)], axis=1)
    return wb, wc, coef.astype(F32)


def _rope_tables(positions):
    inv_freq = ROPE_THETA ** (-jnp.arange(0, ROPE_DIMS, 2, dtype=F32) / ROPE_DIMS)
    ang = positions.astype(F32).reshape(-1, 1) * inv_freq
    cos, sin = jnp.cos(ang), jnp.sin(ang)
    half = ROPE_DIMS // 2
    pad = DA_HEAD_DIM - ROPE_DIMS
    n = cos.shape[0]
    ones, zeros = jnp.ones((n, pad), F32), jnp.zeros((n, pad), F32)
    zh = jnp.zeros((n, half), F32)
    rc = jnp.concatenate([cos, cos, ones], axis=1)
    rs1 = jnp.concatenate([-sin, zh, zeros], axis=1)
    rs2 = jnp.concatenate([zh, sin, zeros], axis=1)
    rep = LANES // DA_HEAD_DIM
    return jnp.tile(rc, (1, rep)), jnp.tile(rs1, (1, rep)), jnp.tile(rs2, (1, rep))


def _tiles(L):
    return dict(chunk=min(256, L), tq=min(512, L), tk=min(512, L))


def kernel(x, mem, positions, norm_g, w_in, w_out, s5_lambda_re, s5_lambda_im, s5_log_dt, s5_b_re, s5_b_im, s5_c_re, s5_c_im, s5_d, s5_w_glu, da_lambda_q1, da_lambda_k1, da_lambda_q2, da_lambda_k2, da_subln_g, lru_conv_w, lru_conv_b, lru_w_a, lru_b_a, lru_w_x, lru_b_x, lru_lambda, mem_norm_g, w_mem_kv, final_norm_g):
    B, L, D = x.shape
    M = mem.shape[1]
    depth = w_in.shape[0]
    T = B * L
    t = _tiles(L)
    assert L % t["tk"] == 0 and L % t["chunk"] == 0

    gw = GROUP_WIDTH
    w_in_b = w_in.astype(BF16)
    w_out_b = w_out.astype(BF16)
    wglu_b = s5_w_glu.astype(BF16)
    wb, wc, coef = _s5_prep(s5_lambda_re, s5_lambda_im, s5_log_dt, s5_b_re, s5_b_im, s5_c_re, s5_c_im)
    s5_d2 = s5_d.reshape(depth, 1, gw)
    lamv = jnp.stack([da_lambda_q1, da_lambda_k1, da_lambda_q2, da_lambda_k2], axis=1)
    sg = da_subln_g.reshape(depth, 1, LANES)
    wg = jnp.concatenate([_block_diag(lru_w_a), _block_diag(lru_w_x)], axis=-1).astype(BF16)
    bg = jnp.concatenate([lru_b_a, lru_b_x], axis=-1).reshape(depth, 1, 2 * gw)
    lru_cb = lru_conv_b.reshape(depth, 1, gw)
    lru_lam = lru_lambda.reshape(depth, 1, gw)
    norm_g3 = norm_g.reshape(depth, 1, D)
    rc, rs1, rs2 = _rope_tables(positions)

    kv = _memkv(mem.reshape(B * M, D), mem_norm_g.reshape(depth, 1, D), w_mem_kv.astype(BF16), B, M)

    xf = x.reshape(T, D)
    fg = final_norm_g.reshape(1, D)
    for layer in range(depth):
        rest, qkv, y_a = _inproj_s5(xf, norm_g3, w_in_b, layer, rc, rs1, rs2,
                                    wb, coef, wc, s5_d2, wglu_b, B, L, t["chunk"])
        y_b = _attn(qkv, rest, lamv, sg, layer, B, L, t["tq"], t["tk"])
        xf = _tail(rest, y_a, y_b, xf, kv, lru_conv_w, lru_cb, wg, bg, lru_lam, w_out_b, fg,
                   layer, layer == depth - 1, B, L, M, t["chunk"])
    return xf.reshape(B, L, D)
```

```python
import functools
import math

import jax
import jax.numpy as jnp
from jax import lax
from jax.experimental import pallas as pl
from jax.experimental.pallas import tpu as pltpu

F32 = jnp.float32
BF16 = jnp.bfloat16

EPS = 1e-6
GROUP_WIDTH = 512
LANES = 128
SUBLANES = 8
S5_STATE = 64
S5_CH = 16
S5_BLOCKS = 4
S5_BLOCK_STATES = 512
S5_UNROLL = 4
DA_HEADS = 4
DA_HEAD_DIM = 64
ROPE_DIMS = 16
ROPE_THETA = 500000.0
LRU_BLOCKS = 8
LRU_C = 8.0
MEM_HEADS = 4
MEM_HEAD_DIM = 128
NEG_BIG = -1e30
LOG2E = math.log2(math.e)
Q_SCALE = DA_HEAD_DIM ** -0.5 * LOG2E
VMEM_LIMIT_BYTES = 56 * 1024 * 1024

W_COL_AU = 0
W_COL_Q = 2 * GROUP_WIDTH
W_COL_REST = 5 * GROUP_WIDTH
COL_BG, COL_CX, COL_CG, COL_MQ, COL_MG = range(5)
N_REST = 5 * GROUP_WIDTH


def _params(*sem):
    return pltpu.CompilerParams(dimension_semantics=sem, vmem_limit_bytes=VMEM_LIMIT_BYTES)


def _sigmoid(x):
    return 0.5 * jnp.tanh(0.5 * x) + 0.5


def _silu(x):
    return x * _sigmoid(x)


def _gelu_tanh(x):
    return 0.5 * x * (1.0 + jnp.tanh(math.sqrt(2.0 / math.pi) * (x + 0.044715 * (x * x * x))))


def _dot(a, b):
    return jnp.dot(a, b, preferred_element_type=F32)


def _dot_nt(a, b):
    return lax.dot_general(a, b, (((1,), (1,)), ((), ())), preferred_element_type=F32)


def _rms(x, g):
    ms = jnp.mean(x * x, axis=-1, keepdims=True)
    return x * lax.rsqrt(ms + EPS) * g


def _s5_input(u, wb_ref, us_ref, xs_ref, chunk):
    us_ref[0:SUBLANES, :] = jnp.zeros((SUBLANES, GROUP_WIDTH), F32)
    us_ref[SUBLANES:SUBLANES + chunk, :] = u
    row_in_tile = lax.broadcasted_iota(jnp.int32, (chunk, LANES), 0) % SUBLANES
    blk = 2 * S5_BLOCK_STATES
    for q in range(S5_BLOCKS):
        cols = slice(q * LANES, (q + 1) * LANES)
        parts = [u[:, cols].astype(BF16)]
        for d in range(1, S5_UNROLL):
            shifted = us_ref[SUBLANES - d:SUBLANES - d + chunk, cols]
            parts.append(jnp.where(row_in_tile >= d, shifted, 0.0).astype(BF16))
        xs_ref[:, q * blk:(q + 1) * blk] = _dot(jnp.concatenate(parts, axis=1), wb_ref[q])


def _s5_scan_out(u, gate, coef_ref, wc_ref, d_ref, wglu_ref, xs_ref, carry_ref, chunk):
    blk = 2 * S5_BLOCK_STATES

    def cmul_add(br, bi, ar, ai, sr, si):
        return br + (ar * sr - ai * si), bi + (ar * si + ai * sr)

    for r in range(chunk // SUBLANES):
        rows = slice(r * SUBLANES, (r + 1) * SUBLANES)
        for q in range(S5_BLOCKS):
            for j in range(S5_BLOCK_STATES // LANES):
                lr = q * blk + j * LANES
                li = lr + S5_BLOCK_STATES
                lc = q * S5_BLOCK_STATES + j * LANES
                xr, xi = cmul_add(xs_ref[rows, lr:lr + LANES], xs_ref[rows, li:li + LANES],
                                  coef_ref[0, :, lc:lc + LANES], coef_ref[1, :, lc:lc + LANES],
                                  carry_ref[:, lr:lr + LANES], carry_ref[:, li:li + LANES])
                xr, xi = cmul_add(xr, xi, coef_ref[2, :, lc:lc + LANES], coef_ref[3, :, lc:lc + LANES],
                                  pltpu.roll(xr, S5_UNROLL, axis=0), pltpu.roll(xi, S5_UNROLL, axis=0))
                xs_ref[rows, lr:lr + LANES] = xr
                xs_ref[rows, li:li + LANES] = xi
                carry_ref[:, lr:lr + LANES] = jnp.broadcast_to(xr[SUBLANES - 1:, :], (SUBLANES, LANES))
                carry_ref[:, li:li + LANES] = jnp.broadcast_to(xi[SUBLANES - 1:, :], (SUBLANES, LANES))

    ys = []
    for q in range(S5_BLOCKS):
        xb = xs_ref[:, q * blk:(q + 1) * blk].astype(BF16)
        ys.append(_dot(xb, wc_ref[q]))
    y = jnp.concatenate(ys, axis=1) + d_ref[...] * u
    y = _gelu_tanh(y)
    z = _dot(y.astype(BF16), wglu_ref[...])
    return z[:, :GROUP_WIDTH] * _sigmoid(z[:, GROUP_WIDTH:]) * _silu(gate)


def _inproj_s5_kernel(x_ref, g_ref, w_ref, rc_ref, rs1_ref, rs2_ref,
                      wb_ref, coef_ref, wc_ref, d_ref, wglu_ref,
                      rest_ref, qkv_ref, ya_ref, us_ref, xs_ref, carry_ref, *, chunk):
    @pl.when(pl.program_id(1) == 0)
    def _():
        carry_ref[...] = jnp.zeros(carry_ref.shape, F32)

    gw = GROUP_WIDTH
    hb = _rms(x_ref[...], g_ref[...]).astype(BF16)
    u = _dot(hb, w_ref[:, W_COL_AU:W_COL_AU + gw])
    _s5_input(u, wb_ref, us_ref, xs_ref, chunk)

    gate = _dot(hb, w_ref[:, W_COL_AU + gw:W_COL_AU + 2 * gw])
    rest_ref[...] = _dot(hb, w_ref[:, W_COL_REST:W_COL_REST + N_REST])
    qk = _dot(hb, w_ref[:, W_COL_Q:W_COL_Q + 2 * gw])
    c, s1, s2 = rc_ref[...], rs1_ref[...], rs2_ref[...]
    for j in range(2 * gw // LANES):
        t = qk[:, j * LANES:(j + 1) * LANES]
        t = t * c + pltpu.roll(t, LANES - ROPE_DIMS // 2, axis=1) * s1 + pltpu.roll(t, ROPE_DIMS // 2, axis=1) * s2
        if j < gw // LANES:
            t = t * Q_SCALE
        qkv_ref[:, j * LANES:(j + 1) * LANES] = t.astype(BF16)
    qkv_ref[:, 2 * gw:] = _dot(hb, w_ref[:, W_COL_Q + 2 * gw:W_COL_Q + 3 * gw]).astype(BF16)

    ya_ref[...] = _s5_scan_out(u, gate, coef_ref, wc_ref, d_ref, wglu_ref,
                               xs_ref, carry_ref, chunk).astype(BF16)


def _inproj_s5(x, g, w_all, layer, rc, rs1, rs2, wb, coef, wc, d, wglu, B, L, chunk):
    T, D = x.shape
    n_in = w_all.shape[-1]
    nc = L // chunk
    n_state = S5_BLOCKS * 2 * S5_BLOCK_STATES

    def row_blk(b, c):
        return (b * nc + c, 0)

    return pl.pallas_call(
        functools.partial(_inproj_s5_kernel, chunk=chunk),
        grid=(B, nc),
        in_specs=[
            pl.BlockSpec((chunk, D), row_blk),
            pl.BlockSpec((None, 1, D), lambda b, c: (layer, 0, 0)),
            pl.BlockSpec((None, D, n_in), lambda b, c: (layer, 0, 0), pipeline_mode=pl.Buffered(1)),
            pl.BlockSpec((chunk, LANES), row_blk),
            pl.BlockSpec((chunk, LANES), row_blk),
            pl.BlockSpec((chunk, LANES), row_blk),
            pl.BlockSpec((None,) + wb.shape[1:], lambda b, c: (layer, 0, 0, 0)),
            pl.BlockSpec((None,) + coef.shape[1:], lambda b, c: (layer, 0, 0, 0)),
            pl.BlockSpec((None,) + wc.shape[1:], lambda b, c: (layer, 0, 0, 0)),
            pl.BlockSpec((None, 1, GROUP_WIDTH), lambda b, c: (layer, 0, 0)),
            pl.BlockSpec((None,) + wglu.shape[1:], lambda b, c: (layer, 0, 0)),
        ],
        out_specs=[
            pl.BlockSpec((chunk, N_REST), row_blk),
            pl.BlockSpec((chunk, 3 * GROUP_WIDTH), row_blk),
            pl.BlockSpec((chunk, GROUP_WIDTH), row_blk),
        ],
        out_shape=[jax.ShapeDtypeStruct((T, N_REST), F32),
                   jax.ShapeDtypeStruct((T, 3 * GROUP_WIDTH), BF16),
                   jax.ShapeDtypeStruct((T, GROUP_WIDTH), BF16)],
        scratch_shapes=[pltpu.VMEM((chunk + SUBLANES, GROUP_WIDTH), F32),
                        pltpu.VMEM((chunk, n_state), F32),
                        pltpu.VMEM((SUBLANES, n_state), F32)],
        compiler_params=_params("arbitrary", "arbitrary"),
        name="inproj_s5",
    )(x, g, w_all, rc, rs1, rs2, wb, coef, wc, d, wglu)


def _attn_kernel(lamv_ref, q_ref, k_ref, v_ref, g_ref, sg_ref, o_ref,
                 qs_ref, s_ref, m_ref, acc_ref, *, tq, tk, lam_init):
    i = pl.program_id(2)
    lane = lax.broadcasted_iota(jnp.int32, (tq, LANES), 1)
    for st in range(2):
        q = q_ref[:, st * LANES:(st + 1) * LANES]
        zero = jnp.zeros_like(q)
        qs_ref[st, 0:tq, :] = jnp.where(lane < DA_HEAD_DIM, q, zero)
        qs_ref[st, tq:, :] = jnp.where(lane >= DA_HEAD_DIM, q, zero)
    m_ref[...] = jnp.full(m_ref.shape, NEG_BIG, F32)
    acc_ref[...] = jnp.zeros(acc_ref.shape, F32)
    ones = jnp.ones((tk, LANES), BF16)

    def scores(st, kb):
        k = k_ref[pl.ds(pl.multiple_of(kb * tk, tk), tk), st * LANES:(st + 1) * LANES]
        s_ref[st] = _dot_nt(qs_ref[st], k)

    def softmax_pv(st, kb, masked):
        kstart = pl.multiple_of(kb * tk, tk)
        v = jnp.concatenate([v_ref[pl.ds(kstart, tk), st * LANES:(st + 1) * LANES], ones], axis=1)
        s = s_ref[st]
        if masked:
            row = lax.broadcasted_iota(jnp.int32, s.shape, 0)
            col = lax.broadcasted_iota(jnp.int32, s.shape, 1)
            qpos = i * tq + jnp.where(row >= tq, row - tq, row)
            s = jnp.where(kstart + col <= qpos, s, NEG_BIG)
        m_prev = m_ref[st]
        m_new = jnp.maximum(m_prev, jnp.max(s, axis=-1, keepdims=True))
        alpha = jnp.exp2(m_prev - m_new)
        p = jnp.exp2(s - jnp.concatenate([m_new] * (tk // LANES), axis=1))
        acc_ref[st] = jnp.concatenate([alpha, alpha], axis=1) * acc_ref[st] + _dot(p.astype(BF16), v)
        m_ref[st] = m_new

    scores(0, i)
    scores(1, i)
    softmax_pv(0, i, True)
    scores(0, 0)
    softmax_pv(1, i, True)

    def step(kb):
        scores(1, kb)
        softmax_pv(0, kb, False)
        scores(0, kb + 1)
        softmax_pv(1, kb, False)

    def two_steps(jj, carry):
        step(2 * jj)
        step(2 * jj + 1)
        return carry

    lax.fori_loop(0, lax.div(i, 2), two_steps, 0)

    @pl.when(lax.rem(i, 2) == 1)
    def _():
        step(i - 1)

    lv = lamv_ref[...]
    lam = (jnp.exp(jnp.sum(lv[0:1, :] * lv[1:2, :], axis=-1, keepdims=True))
           - jnp.exp(jnp.sum(lv[2:3, :] * lv[3:4, :], axis=-1, keepdims=True)) + lam_init)
    for st in range(2):
        acc = acc_ref[st]
        o = (acc[:tq, :LANES] / acc[:tq, LANES:]) - lam * (acc[tq:, :LANES] / acc[tq:, LANES:])
        o = _rms(o, sg_ref[...]) * (1.0 - lam_init)
        cols = slice(st * LANES, (st + 1) * LANES)
        o_ref[:, cols] = (o * _silu(g_ref[:, cols])).astype(BF16)


def _attn(qkv, rest, lamv, sg, layer, B, L, tq, tk):
    T = B * L
    nq = L // tq
    assert tq == tk, "the causal block bookkeeping assumes square score blocks"
    lam_init = 0.8 - 0.6 * math.exp(-0.3 * layer)
    pair = 2 * LANES
    n_pair = GROUP_WIDTH // pair
    gate_blk = COL_BG * n_pair
    return pl.pallas_call(
        functools.partial(_attn_kernel, tq=tq, tk=tk, lam_init=lam_init),
        grid=(B, n_pair, nq),
        in_specs=[
            pl.BlockSpec((None, 4, DA_HEAD_DIM), lambda b, h, i: (layer, 0, 0)),
            pl.BlockSpec((tq, pair), lambda b, h, i: (b * nq + i, h)),
            pl.BlockSpec((L, pair), lambda b, h, i: (b, n_pair + h)),
            pl.BlockSpec((L, pair), lambda b, h, i: (b, 2 * n_pair + h)),
            pl.BlockSpec((tq, pair), lambda b, h, i: (b * nq + i, gate_blk + h)),
            pl.BlockSpec((None, 1, LANES), lambda b, h, i: (layer, 0, 0)),
        ],
        out_specs=pl.BlockSpec((tq, pair), lambda b, h, i: (b * nq + i, h)),
        out_shape=jax.ShapeDtypeStruct((T, GROUP_WIDTH), BF16),
        scratch_shapes=[
            pltpu.VMEM((2, 2 * tq, LANES), BF16),
            pltpu.VMEM((2, 2 * tq, tk), F32),
            pltpu.VMEM((2, 2 * tq, LANES), F32),
            pltpu.VMEM((2, 2 * tq, 2 * LANES), F32),
        ],
        compiler_params=_params("parallel", "parallel", "arbitrary"),
        name="diffattn",
    )(lamv, qkv, qkv, qkv, rest, sg)


def _memkv_kernel(mem_ref, g_ref, w_ref, o_ref):
    o_ref[...] = _dot(_rms(mem_ref[...], g_ref[...]).astype(BF16), w_ref[...]).astype(BF16)


def _memkv(mem2d, g, w, B, M):
    depth, D, n_kv = w.shape
    return pl.pallas_call(
        _memkv_kernel,
        grid=(depth, B),
        in_specs=[
            pl.BlockSpec((M, D), lambda l, b: (b, 0)),
            pl.BlockSpec((None, 1, D), lambda l, b: (l, 0, 0)),
            pl.BlockSpec((None, D, n_kv), lambda l, b: (l, 0, 0)),
        ],
        out_specs=pl.BlockSpec((None, M, n_kv), lambda l, b: (l, b, 0)),
        out_shape=jax.ShapeDtypeStruct((depth, B * M, n_kv), BF16),
        compiler_params=_params("parallel", "parallel"),
        name="memkv",
    )(mem2d, g, w)


def _lru_gates(x, cw_ref, cb_ref, wg_ref, bg_ref, xe_ref, chunk):
    xe_ref[SUBLANES:SUBLANES + chunk, :] = x
    xc = (cw_ref[3:4, :] * x
          + cw_ref[2:3, :] * xe_ref[SUBLANES - 1:SUBLANES - 1 + chunk, :]
          + cw_ref[1:2, :] * xe_ref[SUBLANES - 2:SUBLANES - 2 + chunk, :]
          + cw_ref[0:1, :] * xe_ref[SUBLANES - 3:SUBLANES - 3 + chunk, :]
          + cb_ref[...])
    return xc, _dot(xc.astype(BF16), wg_ref[...]) + bg_ref[...]


def _lru_scan(xc, z, gate, lam_ref, a_ref, b_ref, h_ref, carry_ref, chunk):
    r = _sigmoid(z[:, :GROUP_WIDTH])
    gate_i = _sigmoid(z[:, GROUP_WIDTH:])
    nl = -lam_ref[...]
    softplus = jnp.maximum(nl, 0.0) + jnp.log1p(jnp.exp(-jnp.abs(nl)))
    log_a = (-LRU_C) * r * softplus
    a = jnp.exp(log_a)
    a_ref[...] = a
    mult = jnp.sqrt(-jnp.tanh(log_a) * (1.0 + a * a))
    b_ref[...] = mult * (gate_i * xc)

    row = lax.broadcasted_iota(jnp.int32, (SUBLANES, LANES), 0)
    for t in range(chunk // SUBLANES):
        rows = slice(t * SUBLANES, (t + 1) * SUBLANES)
        for j in range(GROUP_WIDTH // LANES):
            cols = slice(j * LANES, (j + 1) * LANES)
            a = a_ref[rows, cols]
            b = b_ref[rows, cols]
            b = b + jnp.where(row == 0, a * carry_ref[:, cols], 0.0)
            for shift in (1, 2, 4):
                keep = row >= shift
                b = b + jnp.where(keep, a * pltpu.roll(b, shift, axis=0), 0.0)
                if shift != 4:
                    a = jnp.where(keep, a * pltpu.roll(a, shift, axis=0), a)
            h_ref[rows, cols] = b
            carry_ref[:, cols] = jnp.broadcast_to(b[SUBLANES - 1:, :], (SUBLANES, LANES))
    return h_ref[...] * _silu(gate)


def _memattn_body(q, gate, kv_ref):
    qb = q.astype(BF16)
    scale = MEM_HEAD_DIM ** -0.5
    outs = []
    for h in range(MEM_HEADS):
        cols = slice(h * MEM_HEAD_DIM, (h + 1) * MEM_HEAD_DIM)
        k = kv_ref[:, h * MEM_HEAD_DIM:(h + 1) * MEM_HEAD_DIM]
        v = kv_ref[:, GROUP_WIDTH + h * MEM_HEAD_DIM:GROUP_WIDTH + (h + 1) * MEM_HEAD_DIM]
        s = _dot_nt(qb[:, cols], k) * scale
        e = jnp.exp(s - jnp.max(s, axis=-1, keepdims=True))
        p = e / jnp.sum(e, axis=-1, keepdims=True)
        outs.append(_dot(p.astype(BF16), v))
    return jnp.concatenate(outs, axis=1) * _silu(gate)


def _tail_kernel(cx_ref, cg_ref, mq_ref, mg_ref, ya_ref, yb_ref, x_ref, kv_ref,
                 cw_ref, cb_ref, wg_ref, bg_ref, lam_ref, w_ref, fg_ref, o_ref,
                 xe_ref, a_ref, b_ref, h_ref, carry_ref, *, chunk, final):
    c = pl.program_id(1)

    @pl.when(c == 0)
    def _():
        xe_ref[0:SUBLANES, :] = jnp.zeros((SUBLANES, GROUP_WIDTH), F32)
        carry_ref[...] = jnp.zeros(carry_ref.shape, F32)

    @pl.when(c > 0)
    def _():
        xe_ref[0:SUBLANES, :] = xe_ref[chunk:chunk + SUBLANES, :]

    gw = GROUP_WIDTH
    xc, z = _lru_gates(cx_ref[...], cw_ref, cb_ref, wg_ref, bg_ref, xe_ref, chunk)
    acc = x_ref[...] + _dot(ya_ref[...], w_ref[0:gw, :]) + _dot(yb_ref[...], w_ref[gw:2 * gw, :])
    y_m = _memattn_body(mq_ref[...], mg_ref[...], kv_ref)
    acc = acc + _dot(y_m.astype(BF16), w_ref[3 * gw:4 * gw, :])
    y_c = _lru_scan(xc, z, cg_ref[...], lam_ref, a_ref, b_ref, h_ref, carry_ref, chunk)
    acc = acc + _dot(y_c.astype(BF16), w_ref[2 * gw:3 * gw, :])
    o_ref[...] = _rms(acc, fg_ref[...]) if final else acc


def _tail(rest, y_a, y_b, x, kv, cw, cb, wg, bg, lam, w_all, fg, layer, final, B, L, M, chunk):
    T, D = x.shape
    nc = L // chunk
    gw = GROUP_WIDTH

    def row_blk(b, c):
        return (b * nc + c, 0)

    def rest_spec(col):
        return pl.BlockSpec((chunk, gw), lambda b, c: (b * nc + c, col))

    def per_layer(shape):
        return pl.BlockSpec((None,) + shape, lambda b, c: (layer,) + (0,) * len(shape))

    return pl.pallas_call(
        functools.partial(_tail_kernel, chunk=chunk, final=final),
        grid=(B, nc),
        in_specs=[
            rest_spec(COL_CX), rest_spec(COL_CG), rest_spec(COL_MQ), rest_spec(COL_MG),
            pl.BlockSpec((chunk, gw), row_blk),
            pl.BlockSpec((chunk, gw), row_blk),
            pl.BlockSpec((chunk, D), row_blk),
            pl.BlockSpec((None, M, 2 * gw), lambda b, c: (layer, b, 0)),
            per_layer((4, gw)), per_layer((1, gw)), per_layer((gw, 2 * gw)), per_layer((1, 2 * gw)),
            per_layer((1, gw)), per_layer(w_all.shape[1:]),
            pl.BlockSpec((1, D), lambda b, c: (0, 0)),
        ],
        out_specs=pl.BlockSpec((chunk, D), row_blk),
        out_shape=jax.ShapeDtypeStruct((T, D), F32),
        scratch_shapes=[
            pltpu.VMEM((chunk + SUBLANES, gw), F32),
            pltpu.VMEM((chunk, gw), F32),
            pltpu.VMEM((chunk, gw), F32),
            pltpu.VMEM((chunk, gw), F32),
            pltpu.VMEM((SUBLANES, gw), F32),
        ],
        compiler_params=_params("arbitrary", "arbitrary"),
        name="tail",
    )(rest, rest, rest, rest, y_a, y_b, x, kv, cw, cb, wg, bg, lam, w_all, fg)


def _block_diag(blocks):
    n = blocks.shape[-3]
    eye = jnp.eye(n, dtype=blocks.dtype)
    out = jnp.einsum('...nrc,nm->...nrmc', blocks, eye)
    return out.reshape(blocks.shape[:-3] + (n * blocks.shape[-2], n * blocks.shape[-1]))


def _s5_prep(lam_re, lam_im, log_dt, b_re, b_im, c_re, c_im):
    depth = lam_re.shape[0]
    dt = jnp.exp(log_dt)[..., None]
    mag = jnp.exp(lam_re * dt)
    abar_re = mag * jnp.cos(lam_im * dt)
    abar_im = mag * jnp.sin(lam_im * dt)
    den = lam_re * lam_re + lam_im * lam_im
    nr, ni = abar_re - 1.0, abar_im
    f_re = (nr * lam_re + ni * lam_im) / den
    f_im = (ni * lam_re - nr * lam_im) / den
    bb_re = f_re[..., None] * b_re - f_im[..., None] * b_im
    bb_im = f_re[..., None] * b_im + f_im[..., None] * b_re
    gpb = LANES // S5_CH

    def in_blocks(bb):
        return _block_diag(jnp.swapaxes(bb, -1, -2).reshape(depth, S5_BLOCKS, gpb, S5_CH, S5_STATE))

    def out_blocks(cc):
        return _block_diag(jnp.swapaxes(cc, -1, -2).reshape(depth, S5_BLOCKS, gpb, S5_STATE, S5_CH))

    pr, pi = jnp.ones_like(abar_re), jnp.zeros_like(abar_im)
    wb_d, powers = [], []
    for _ in range(S5_UNROLL):
        wd_re = pr[..., None] * bb_re - pi[..., None] * bb_im
        wd_im = pr[..., None] * bb_im + pi[..., None] * bb_re
        wb_d.append(jnp.concatenate([in_blocks(wd_re), in_blocks(wd_im)], axis=-1))
        pr, pi = pr * abar_re - pi * abar_im, pr * abar_im + pi * abar_re
        powers.append((pr.reshape(depth, 1, -1), pi.reshape(depth, 1, -1)))
    wb = jnp.concatenate(wb_d, axis=-2).astype(BF16)
    wc = jnp.concatenate([out_blocks(c_re), out_blocks(-c_im)], axis=-2).astype(BF16)

    zero = jnp.zeros_like(powers[0][0])
    pad = [zero] * (SUBLANES - S5_UNROLL)
    carry_re = jnp.concatenate([p[0] for p in powers] + pad, axis=1)
    carry_im = jnp.concatenate([p[1] for p in powers] + pad, axis=1)
    step_re = jnp.concatenate([zero] * S5_UNROLL + [powers[-1][0]] * (SUBLANES - S5_UNROLL), axis=1)
    step_im = jnp.concatenate([zero] * S5_UNROLL + [powers[-1][1]] * (SUBLANES - S5_UNROLL), axis=1)
    coef = jnp.stack([carry_re, carry_im, step_re, step_im], axis=1)
    return wb, wc, coef.astype(F32)


def _rope_tables(positions):
    inv_freq = ROPE_THETA ** (-jnp.arange(0, ROPE_DIMS, 2, dtype=F32) / ROPE_DIMS)
    ang = positions.astype(F32).reshape(-1, 1) * inv_freq
    cos, sin = jnp.cos(ang), jnp.sin(ang)
    half = ROPE_DIMS // 2
    pad = DA_HEAD_DIM - ROPE_DIMS
    n = cos.shape[0]
    ones, zeros = jnp.ones((n, pad), F32), jnp.zeros((n, pad), F32)
    zh = jnp.zeros((n, half), F32)
    rc = jnp.concatenate([cos, cos, ones], axis=1)
    rs1 = jnp.concatenate([-sin, zh, zeros], axis=1)
    rs2 = jnp.concatenate([zh, sin, zeros], axis=1)
    rep = LANES // DA_HEAD_DIM
    return jnp.tile(rc, (1, rep)), jnp.tile(rs1, (1, rep)), jnp.tile(rs2, (1, rep))


def _tiles(L):
    return dict(chunk=min(256, L), tq=min(512, L), tk=min(512, L))


def kernel(x, mem, positions, norm_g, w_in, w_out, s5_lambda_re, s5_lambda_im, s5_log_dt, s5_b_re, s5_b_im, s5_c_re, s5_c_im, s5_d, s5_w_glu, da_lambda_q1, da_lambda_k1, da_lambda_q2, da_lambda_k2, da_subln_g, lru_conv_w, lru_conv_b, lru_w_a, lru_b_a, lru_w_x, lru_b_x, lru_lambda, mem_norm_g, w_mem_kv, final_norm_g):
    B, L, D = x.shape
    M = mem.shape[1]
    depth = w_in.shape[0]
    T = B * L
    t = _tiles(L)
    assert L % t["tk"] == 0 and L % t["chunk"] == 0

    gw = GROUP_WIDTH
    w_in_b = w_in.astype(BF16)
    w_out_b = w_out.astype(BF16)
    wglu_b = s5_w_glu.astype(BF16)
    wb, wc, coef = _s5_prep(s5_lambda_re, s5_lambda_im, s5_log_dt, s5_b_re, s5_b_im, s5_c_re, s5_c_im)
    s5_d2 = s5_d.reshape(depth, 1, gw)
    lamv = jnp.stack([da_lambda_q1, da_lambda_k1, da_lambda_q2, da_lambda_k2], axis=1)
    sg = da_subln_g.reshape(depth, 1, LANES)
    wg = jnp.concatenate([_block_diag(lru_w_a), _block_diag(lru_w_x)], axis=-1).astype(BF16)
    bg = jnp.concatenate([lru_b_a, lru_b_x], axis=-1).reshape(depth, 1, 2 * gw)
    lru_cb = lru_conv_b.reshape(depth, 1, gw)
    lru_lam = lru_lambda.reshape(depth, 1, gw)
    norm_g3 = norm_g.reshape(depth, 1, D)
    rc, rs1, rs2 = _rope_tables(positions)

    kv = _memkv(mem.reshape(B * M, D), mem_norm_g.reshape(depth, 1, D), w_mem_kv.astype(BF16), B, M)

    xf = x.reshape(T, D)
    fg = final_norm_g.reshape(1, D)
    for layer in range(depth):
        rest, qkv, y_a = _inproj_s5(xf, norm_g3, w_in_b, layer, rc, rs1, rs2,
                                    wb, coef, wc, s5_d2, wglu_b, B, L, t["chunk"])
        y_b = _attn(qkv, rest, lamv, sg, layer, B, L, t["tq"], t["tk"])
        xf = _tail(rest, y_a, y_b, xf, kv, lru_conv_w, lru_cb, wg, bg, lru_lam, w_out_b, fg,
                   layer, layer == depth - 1, B, L, M, t["chunk"])
    return xf.reshape(B, L, D)
```

```python
import functools
import math

import jax
import jax.numpy as jnp
from jax import lax
from jax.experimental import pallas as pl
from jax.experimental.pallas import tpu as pltpu

F32 = jnp.float32
BF16 = jnp.bfloat16

EPS = 1e-6
GROUP_WIDTH = 512
LANES = 128
SUBLANES = 8
S5_STATE = 64
S5_CH = 16
S5_BLOCKS = 4
S5_BLOCK_STATES = 512
S5_UNROLL = 4
S5_ROW_GROUPS = 2
TAIL_ROW_GROUPS = 1
DA_HEADS = 4
DA_HEAD_DIM = 64
ROPE_DIMS = 16
ROPE_THETA = 500000.0
LRU_BLOCKS = 8
LRU_C = 8.0
MEM_HEADS = 4
MEM_HEAD_DIM = 128
NEG_BIG = -1e30
LOG2E = math.log2(math.e)
Q_SCALE = DA_HEAD_DIM ** -0.5 * LOG2E
VMEM_LIMIT_BYTES = 56 * 1024 * 1024

W_COL_AU = 0
W_COL_Q = 2 * GROUP_WIDTH
W_COL_REST = 5 * GROUP_WIDTH
COL_BG, COL_CX, COL_CG, COL_MQ, COL_MG = range(5)
N_REST = 5 * GROUP_WIDTH


def _params(*sem):
    return pltpu.CompilerParams(dimension_semantics=sem, vmem_limit_bytes=VMEM_LIMIT_BYTES)


def _sigmoid(x):
    return 0.5 * jnp.tanh(0.5 * x) + 0.5


def _silu(x):
    return x * _sigmoid(x)


def _gelu_tanh(x):
    return 0.5 * x * (1.0 + jnp.tanh(math.sqrt(2.0 / math.pi) * (x + 0.044715 * (x * x * x))))


def _dot(a, b):
    return jnp.dot(a, b, preferred_element_type=F32)


def _dot_nt(a, b):
    return lax.dot_general(a, b, (((1,), (1,)), ((), ())), preferred_element_type=F32)


def _rms(x, g):
    ms = jnp.mean(x * x, axis=-1, keepdims=True)
    return x * lax.rsqrt(ms + EPS) * g


def _s5_input(u, wb_ref, us_ref, xs_ref, chunk):
    us_ref[0:SUBLANES, :] = jnp.zeros((SUBLANES, GROUP_WIDTH), F32)
    us_ref[SUBLANES:SUBLANES + chunk, :] = u
    row_in_tile = lax.broadcasted_iota(jnp.int32, (chunk, LANES), 0) % SUBLANES
    blk = 2 * S5_BLOCK_STATES
    for q in range(S5_BLOCKS):
        cols = slice(q * LANES, (q + 1) * LANES)
        parts = [u[:, cols].astype(BF16)]
        for d in range(1, S5_UNROLL):
            shifted = us_ref[SUBLANES - d:SUBLANES - d + chunk, cols]
            parts.append(jnp.where(row_in_tile >= d, shifted, 0.0).astype(BF16))
        xs_ref[:, q * blk:(q + 1) * blk] = _dot(jnp.concatenate(parts, axis=1), wb_ref[q])


def _s5_scan_out(u, gate, coef_ref, wc_ref, d_ref, wglu_ref, xs_ref, carry_ref, chunk):
    blk = 2 * S5_BLOCK_STATES

    def cmul_add(br, bi, ar, ai, sr, si):
        return br + (ar * sr - ai * si), bi + (ar * si + ai * sr)

    for r in range(chunk // SUBLANES):
        rows = slice(r * SUBLANES, (r + 1) * SUBLANES)
        for q in range(S5_BLOCKS):
            for j in range(S5_BLOCK_STATES // LANES):
                lr = q * blk + j * LANES
                li = lr + S5_BLOCK_STATES
                lc = q * S5_BLOCK_STATES + j * LANES
                xr, xi = cmul_add(xs_ref[rows, lr:lr + LANES], xs_ref[rows, li:li + LANES],
                                  coef_ref[0, :, lc:lc + LANES], coef_ref[1, :, lc:lc + LANES],
                                  carry_ref[:, lr:lr + LANES], carry_ref[:, li:li + LANES])
                xr, xi = cmul_add(xr, xi, coef_ref[2, :, lc:lc + LANES], coef_ref[3, :, lc:lc + LANES],
                                  pltpu.roll(xr, S5_UNROLL, axis=0), pltpu.roll(xi, S5_UNROLL, axis=0))
                xs_ref[rows, lr:lr + LANES] = xr
                xs_ref[rows, li:li + LANES] = xi
                carry_ref[:, lr:lr + LANES] = jnp.broadcast_to(xr[SUBLANES - 1:, :], (SUBLANES, LANES))
                carry_ref[:, li:li + LANES] = jnp.broadcast_to(xi[SUBLANES - 1:, :], (SUBLANES, LANES))

    ys = []
    for q in range(S5_BLOCKS):
        xb = xs_ref[:, q * blk:(q + 1) * blk].astype(BF16)
        ys.append(_dot(xb, wc_ref[q]))
    y = jnp.concatenate(ys, axis=1) + d_ref[...] * u
    y = _gelu_tanh(y)
    z = _dot(y.astype(BF16), wglu_ref[...])
    return z[:, :GROUP_WIDTH] * _sigmoid(z[:, GROUP_WIDTH:]) * _silu(gate)


def _inproj_s5_kernel(x_ref, g_ref, w_ref, rc_ref, rs1_ref, rs2_ref,
                      wb_ref, coef_ref, wc_ref, d_ref, wglu_ref,
                      rest_ref, qkv_ref, ya_ref, us_ref, xs_ref, carry_ref, *, chunk, n_sub):
    @pl.when(pl.program_id(1) == 0)
    def _():
        carry_ref[...] = jnp.zeros(carry_ref.shape, F32)

    gw = GROUP_WIDTH
    sub = chunk // n_sub

    def projections(h):
        rows = pl.ds(h * sub, sub)
        hb = _rms(x_ref[rows, :], g_ref[...]).astype(BF16)
        u = _dot(hb, w_ref[:, W_COL_AU:W_COL_AU + gw])
        _s5_input(u, wb_ref, us_ref.at[h], xs_ref.at[rows], sub)
        gate = _dot(hb, w_ref[:, W_COL_AU + gw:W_COL_AU + 2 * gw])
        rest_ref[rows, :] = _dot(hb, w_ref[:, W_COL_REST:W_COL_REST + N_REST])
        qk = _dot(hb, w_ref[:, W_COL_Q:W_COL_Q + 2 * gw])
        c, s1, s2 = rc_ref[rows, :], rs1_ref[rows, :], rs2_ref[rows, :]
        for j in range(2 * gw // LANES):
            t = qk[:, j * LANES:(j + 1) * LANES]
            t = (t * c + pltpu.roll(t, LANES - ROPE_DIMS // 2, axis=1) * s1
                 + pltpu.roll(t, ROPE_DIMS // 2, axis=1) * s2)
            if j < gw // LANES:
                t = t * Q_SCALE
            qkv_ref[rows, j * LANES:(j + 1) * LANES] = t.astype(BF16)
        qkv_ref[rows, 2 * gw:] = _dot(hb, w_ref[:, W_COL_Q + 2 * gw:W_COL_Q + 3 * gw]).astype(BF16)
        return u, gate

    fronts = [projections(h) for h in range(n_sub)]
    for h, (u, gate) in enumerate(fronts):
        rows = pl.ds(h * sub, sub)
        ya_ref[rows, :] = _s5_scan_out(u, gate, coef_ref, wc_ref, d_ref, wglu_ref,
                                       xs_ref.at[rows], carry_ref, sub).astype(BF16)


def _inproj_s5(x, g, w_all, layer, rc, rs1, rs2, wb, coef, wc, d, wglu, B, L, chunk):
    T, D = x.shape
    n_in = w_all.shape[-1]
    nc = L // chunk
    n_state = S5_BLOCKS * 2 * S5_BLOCK_STATES

    def row_blk(b, c):
        return (b * nc + c, 0)

    return pl.pallas_call(
        functools.partial(_inproj_s5_kernel, chunk=chunk, n_sub=S5_ROW_GROUPS),
        grid=(B, nc),
        in_specs=[
            pl.BlockSpec((chunk, D), row_blk),
            pl.BlockSpec((None, 1, D), lambda b, c: (layer, 0, 0)),
            pl.BlockSpec((None, D, n_in), lambda b, c: (layer, 0, 0), pipeline_mode=pl.Buffered(1)),
            pl.BlockSpec((chunk, LANES), row_blk),
            pl.BlockSpec((chunk, LANES), row_blk),
            pl.BlockSpec((chunk, LANES), row_blk),
            pl.BlockSpec((None,) + wb.shape[1:], lambda b, c: (layer, 0, 0, 0)),
            pl.BlockSpec((None,) + coef.shape[1:], lambda b, c: (layer, 0, 0, 0)),
            pl.BlockSpec((None,) + wc.shape[1:], lambda b, c: (layer, 0, 0, 0)),
            pl.BlockSpec((None, 1, GROUP_WIDTH), lambda b, c: (layer, 0, 0)),
            pl.BlockSpec((None,) + wglu.shape[1:], lambda b, c: (layer, 0, 0)),
        ],
        out_specs=[
            pl.BlockSpec((chunk, N_REST), row_blk),
            pl.BlockSpec((chunk, 3 * GROUP_WIDTH), row_blk),
            pl.BlockSpec((chunk, GROUP_WIDTH), row_blk),
        ],
        out_shape=[jax.ShapeDtypeStruct((T, N_REST), F32),
                   jax.ShapeDtypeStruct((T, 3 * GROUP_WIDTH), BF16),
                   jax.ShapeDtypeStruct((T, GROUP_WIDTH), BF16)],
        scratch_shapes=[pltpu.VMEM((S5_ROW_GROUPS, chunk // S5_ROW_GROUPS + SUBLANES, GROUP_WIDTH), F32),
                        pltpu.VMEM((chunk, n_state), F32),
                        pltpu.VMEM((SUBLANES, n_state), F32)],
        compiler_params=_params("arbitrary", "arbitrary"),
        name="inproj_s5",
    )(x, g, w_all, rc, rs1, rs2, wb, coef, wc, d, wglu)


def _attn_kernel(lamv_ref, q_ref, k_ref, v_ref, g_ref, sg_ref, o_ref,
                 qs_ref, s_ref, m_ref, acc_ref, *, tq, tk, lam_init):
    i = pl.program_id(2)
    lane = lax.broadcasted_iota(jnp.int32, (tq, LANES), 1)
    for st in range(2):
        q = q_ref[:, st * LANES:(st + 1) * LANES]
        zero = jnp.zeros_like(q)
        qs_ref[st, 0:tq, :] = jnp.where(lane < DA_HEAD_DIM, q, zero)
        qs_ref[st, tq:, :] = jnp.where(lane >= DA_HEAD_DIM, q, zero)
    m_ref[...] = jnp.full(m_ref.shape, NEG_BIG, F32)
    acc_ref[...] = jnp.zeros(acc_ref.shape, F32)
    ones = jnp.ones((tk, LANES), BF16)

    def scores(st, kb):
        k = k_ref[pl.ds(pl.multiple_of(kb * tk, tk), tk), st * LANES:(st + 1) * LANES]
        s_ref[st] = _dot_nt(qs_ref[st], k)

    def softmax_pv(st, kb, masked):
        kstart = pl.multiple_of(kb * tk, tk)
        v = jnp.concatenate([v_ref[pl.ds(kstart, tk), st * LANES:(st + 1) * LANES], ones], axis=1)
        s = s_ref[st]
        if masked:
            row = lax.broadcasted_iota(jnp.int32, s.shape, 0)
            col = lax.broadcasted_iota(jnp.int32, s.shape, 1)
            qpos = i * tq + jnp.where(row >= tq, row - tq, row)
            s = jnp.where(kstart + col <= qpos, s, NEG_BIG)
        m_prev = m_ref[st]
        m_new = jnp.maximum(m_prev, jnp.max(s, axis=-1, keepdims=True))
        alpha = jnp.exp2(m_prev - m_new)
        p = jnp.exp2(s - jnp.concatenate([m_new] * (tk // LANES), axis=1))
        acc_ref[st] = jnp.concatenate([alpha, alpha], axis=1) * acc_ref[st] + _dot(p.astype(BF16), v)
        m_ref[st] = m_new

    scores(0, i)
    scores(1, i)
    softmax_pv(0, i, True)
    scores(0, 0)
    softmax_pv(1, i, True)

    def step(kb):
        scores(1, kb)
        softmax_pv(0, kb, False)
        scores(0, kb + 1)
        softmax_pv(1, kb, False)

    def two_steps(jj, carry):
        step(2 * jj)
        step(2 * jj + 1)
        return carry

    lax.fori_loop(0, lax.div(i, 2), two_steps, 0)

    @pl.when(lax.rem(i, 2) == 1)
    def _():
        step(i - 1)

    lv = lamv_ref[...]
    lam = (jnp.exp(jnp.sum(lv[0:1, :] * lv[1:2, :], axis=-1, keepdims=True))
           - jnp.exp(jnp.sum(lv[2:3, :] * lv[3:4, :], axis=-1, keepdims=True)) + lam_init)
    for st in range(2):
        acc = acc_ref[st]
        o = (acc[:tq, :LANES] / acc[:tq, LANES:]) - lam * (acc[tq:, :LANES] / acc[tq:, LANES:])
        o = _rms(o, sg_ref[...]) * (1.0 - lam_init)
        cols = slice(st * LANES, (st + 1) * LANES)
        o_ref[:, cols] = (o * _silu(g_ref[:, cols])).astype(BF16)


def _attn(qkv, rest, lamv, sg, layer, B, L, tq, tk):
    T = B * L
    nq = L // tq
    assert tq == tk, "the causal block bookkeeping assumes square score blocks"
    lam_init = 0.8 - 0.6 * math.exp(-0.3 * layer)
    pair = 2 * LANES
    n_pair = GROUP_WIDTH // pair
    gate_blk = COL_BG * n_pair
    return pl.pallas_call(
        functools.partial(_attn_kernel, tq=tq, tk=tk, lam_init=lam_init),
        grid=(B, n_pair, nq),
        in_specs=[
            pl.BlockSpec((None, 4, DA_HEAD_DIM), lambda b, h, i: (layer, 0, 0)),
            pl.BlockSpec((tq, pair), lambda b, h, i: (b * nq + i, h)),
            pl.BlockSpec((L, pair), lambda b, h, i: (b, n_pair + h)),
            pl.BlockSpec((L, pair), lambda b, h, i: (b, 2 * n_pair + h)),
            pl.BlockSpec((tq, pair), lambda b, h, i: (b * nq + i, gate_blk + h)),
            pl.BlockSpec((None, 1, LANES), lambda b, h, i: (layer, 0, 0)),
        ],
        out_specs=pl.BlockSpec((tq, pair), lambda b, h, i: (b * nq + i, h)),
        out_shape=jax.ShapeDtypeStruct((T, GROUP_WIDTH), BF16),
        scratch_shapes=[
            pltpu.VMEM((2, 2 * tq, LANES), BF16),
            pltpu.VMEM((2, 2 * tq, tk), F32),
            pltpu.VMEM((2, 2 * tq, LANES), F32),
            pltpu.VMEM((2, 2 * tq, 2 * LANES), F32),
        ],
        compiler_params=_params("parallel", "parallel", "arbitrary"),
        name="diffattn",
    )(lamv, qkv, qkv, qkv, rest, sg)


def _memkv_kernel(mem_ref, g_ref, w_ref, o_ref):
    o_ref[...] = _dot(_rms(mem_ref[...], g_ref[...]).astype(BF16), w_ref[...]).astype(BF16)


def _memkv(mem2d, g, w, B, M):
    depth, D, n_kv = w.shape
    return pl.pallas_call(
        _memkv_kernel,
        grid=(depth, B),
        in_specs=[
            pl.BlockSpec((M, D), lambda l, b: (b, 0)),
            pl.BlockSpec((None, 1, D), lambda l, b: (l, 0, 0)),
            pl.BlockSpec((None, D, n_kv), lambda l, b: (l, 0, 0)),
        ],
        out_specs=pl.BlockSpec((None, M, n_kv), lambda l, b: (l, b, 0)),
        out_shape=jax.ShapeDtypeStruct((depth, B * M, n_kv), BF16),
        compiler_params=_params("parallel", "parallel"),
        name="memkv",
    )(mem2d, g, w)


def _lru_gates(x, cw_ref, cb_ref, wg_ref, bg_ref, xe_ref, row0, n_rows):
    base = SUBLANES + row0
    xe_ref[base:base + n_rows, :] = x
    xc = (cw_ref[3:4, :] * x
          + cw_ref[2:3, :] * xe_ref[base - 1:base - 1 + n_rows, :]
          + cw_ref[1:2, :] * xe_ref[base - 2:base - 2 + n_rows, :]
          + cw_ref[0:1, :] * xe_ref[base - 3:base - 3 + n_rows, :]
          + cb_ref[...])
    return xc, _dot(xc.astype(BF16), wg_ref[...]) + bg_ref[...]


def _lru_scan(xc, z, gate, lam_ref, a_ref, b_ref, h_ref, carry_ref, chunk):
    r = _sigmoid(z[:, :GROUP_WIDTH])
    gate_i = _sigmoid(z[:, GROUP_WIDTH:])
    nl = -lam_ref[...]
    softplus = jnp.maximum(nl, 0.0) + jnp.log1p(jnp.exp(-jnp.abs(nl)))
    log_a = (-LRU_C) * r * softplus
    a = jnp.exp(log_a)
    a_ref[...] = a
    mult = jnp.sqrt(-jnp.tanh(log_a) * (1.0 + a * a))
    b_ref[...] = mult * (gate_i * xc)

    row = lax.broadcasted_iota(jnp.int32, (SUBLANES, LANES), 0)
    for t in range(chunk // SUBLANES):
        rows = slice(t * SUBLANES, (t + 1) * SUBLANES)
        for j in range(GROUP_WIDTH // LANES):
            cols = slice(j * LANES, (j + 1) * LANES)
            a = a_ref[rows, cols]
            b = b_ref[rows, cols] + jnp.where(row == 0, a, 0.0) * carry_ref[:, cols]
            w = jnp.where(row == 0, 0.0, a)
            for shift in (1, 2, 4):
                b = b + w * pltpu.roll(b, shift, axis=0)
                if shift != 4:
                    w = w * pltpu.roll(w, shift, axis=0)
            h_ref[rows, cols] = b
            carry_ref[:, cols] = jnp.broadcast_to(b[SUBLANES - 1:, :], (SUBLANES, LANES))
    return h_ref[...] * _silu(gate)


def _memattn_body(q, gate, kv_ref):
    qb = q.astype(BF16)
    scale = MEM_HEAD_DIM ** -0.5 * LOG2E
    outs = []
    for h in range(MEM_HEADS):
        cols = slice(h * MEM_HEAD_DIM, (h + 1) * MEM_HEAD_DIM)
        k = kv_ref[:, h * MEM_HEAD_DIM:(h + 1) * MEM_HEAD_DIM]
        v = kv_ref[:, GROUP_WIDTH + h * MEM_HEAD_DIM:GROUP_WIDTH + (h + 1) * MEM_HEAD_DIM]
        s = _dot_nt(qb[:, cols], k) * scale
        e = jnp.exp2(s - jnp.max(s, axis=-1, keepdims=True))
        outs.append(_dot(e.astype(BF16), v) / jnp.sum(e, axis=-1, keepdims=True))
    return jnp.concatenate(outs, axis=1) * _silu(gate)


def _tail_kernel(cx_ref, cg_ref, mq_ref, mg_ref, ya_ref, yb_ref, x_ref, kv_ref,
                 cw_ref, cb_ref, wg_ref, bg_ref, lam_ref, w_ref, fg_ref, o_ref,
                 xe_ref, a_ref, b_ref, h_ref, carry_ref, *, chunk, n_sub, final):
    c = pl.program_id(1)

    @pl.when(c == 0)
    def _():
        xe_ref[0:SUBLANES, :] = jnp.zeros((SUBLANES, GROUP_WIDTH), F32)
        carry_ref[...] = jnp.zeros(carry_ref.shape, F32)

    @pl.when(c > 0)
    def _():
        xe_ref[0:SUBLANES, :] = xe_ref[chunk:chunk + SUBLANES, :]

    gw = GROUP_WIDTH
    sub = chunk // n_sub

    def front(h):
        rows = pl.ds(h * sub, sub)
        xc, z = _lru_gates(cx_ref[rows, :], cw_ref, cb_ref, wg_ref, bg_ref, xe_ref, h * sub, sub)
        acc = (x_ref[rows, :] + _dot(ya_ref[rows, :], w_ref[0:gw, :])
               + _dot(yb_ref[rows, :], w_ref[gw:2 * gw, :]))
        y_m = _memattn_body(mq_ref[rows, :], mg_ref[rows, :], kv_ref)
        return xc, z, acc + _dot(y_m.astype(BF16), w_ref[3 * gw:4 * gw, :])

    fronts = [front(h) for h in range(n_sub)]
    for h, (xc, z, acc) in enumerate(fronts):
        rows = pl.ds(h * sub, sub)
        y_c = _lru_scan(xc, z, cg_ref[rows, :], lam_ref, a_ref.at[rows], b_ref.at[rows],
                        h_ref.at[rows], carry_ref, sub)
        acc = acc + _dot(y_c.astype(BF16), w_ref[2 * gw:3 * gw, :])
        o_ref[rows, :] = _rms(acc, fg_ref[...]) if final else acc


def _tail(rest, y_a, y_b, x, kv, cw, cb, wg, bg, lam, w_all, fg, layer, final, B, L, M, chunk):
    T, D = x.shape
    nc = L // chunk
    gw = GROUP_WIDTH

    def row_blk(b, c):
        return (b * nc + c, 0)

    def rest_spec(col):
        return pl.BlockSpec((chunk, gw), lambda b, c: (b * nc + c, col))

    def per_layer(shape):
        return pl.BlockSpec((None,) + shape, lambda b, c: (layer,) + (0,) * len(shape))

    return pl.pallas_call(
        functools.partial(_tail_kernel, chunk=chunk, n_sub=TAIL_ROW_GROUPS, final=final),
        grid=(B, nc),
        in_specs=[
            rest_spec(COL_CX), rest_spec(COL_CG), rest_spec(COL_MQ), rest_spec(COL_MG),
            pl.BlockSpec((chunk, gw), row_blk),
            pl.BlockSpec((chunk, gw), row_blk),
            pl.BlockSpec((chunk, D), row_blk),
            pl.BlockSpec((None, M, 2 * gw), lambda b, c: (layer, b, 0)),
            per_layer((4, gw)), per_layer((1, gw)), per_layer((gw, 2 * gw)), per_layer((1, 2 * gw)),
            per_layer((1, gw)), per_layer(w_all.shape[1:]),
            pl.BlockSpec((1, D), lambda b, c: (0, 0)),
        ],
        out_specs=pl.BlockSpec((chunk, D), row_blk),
        out_shape=jax.ShapeDtypeStruct((T, D), F32),
        scratch_shapes=[
            pltpu.VMEM((chunk + SUBLANES, gw), F32),
            pltpu.VMEM((chunk, gw), F32),
            pltpu.VMEM((chunk, gw), F32),
            pltpu.VMEM((chunk, gw), F32),
            pltpu.VMEM((SUBLANES, gw), F32),
        ],
        compiler_params=_params("arbitrary", "arbitrary"),
        name="tail",
    )(rest, rest, rest, rest, y_a, y_b, x, kv, cw, cb, wg, bg, lam, w_all, fg)


def _block_diag(blocks):
    n = blocks.shape[-3]
    eye = jnp.eye(n, dtype=blocks.dtype)
    out = jnp.einsum('...nrc,nm->...nrmc', blocks, eye)
    return out.reshape(blocks.shape[:-3] + (n * blocks.shape[-2], n * blocks.shape[-1]))


def _s5_prep(lam_re, lam_im, log_dt, b_re, b_im, c_re, c_im):
    depth = lam_re.shape[0]
    dt = jnp.exp(log_dt)[..., None]
    mag = jnp.exp(lam_re * dt)
    abar_re = mag * jnp.cos(lam_im * dt)
    abar_im = mag * jnp.sin(lam_im * dt)
    den = lam_re * lam_re + lam_im * lam_im
    nr, ni = abar_re - 1.0, abar_im
    f_re = (nr * lam_re + ni * lam_im) / den
    f_im = (ni * lam_re - nr * lam_im) / den
    bb_re = f_re[..., None] * b_re - f_im[..., None] * b_im
    bb_im = f_re[..., None] * b_im + f_im[..., None] * b_re
    gpb = LANES // S5_CH

    def in_blocks(bb):
        return _block_diag(jnp.swapaxes(bb, -1, -2).reshape(depth, S5_BLOCKS, gpb, S5_CH, S5_STATE))

    def out_blocks(cc):
        return _block_diag(jnp.swapaxes(cc, -1, -2).reshape(depth, S5_BLOCKS, gpb, S5_STATE, S5_CH))

    pr, pi = jnp.ones_like(abar_re), jnp.zeros_like(abar_im)
    wb_d, powers = [], []
    for _ in range(S5_UNROLL):
        wd_re = pr[..., None] * bb_re - pi[..., None] * bb_im
        wd_im = pr[..., None] * bb_im + pi[..., None] * bb_re
        wb_d.append(jnp.concatenate([in_blocks(wd_re), in_blocks(wd_im)], axis=-1))
        pr, pi = pr * abar_re - pi * abar_im, pr * abar_im + pi * abar_re
        powers.append((pr.reshape(depth, 1, -1), pi.reshape(depth, 1, -1)))
    wb = jnp.concatenate(wb_d, axis=-2).astype(BF16)
    wc = jnp.concatenate([out_blocks(c_re), out_blocks(-c_im)], axis=-2).astype(BF16)

    zero = jnp.zeros_like(powers[0][0])
    pad = [zero] * (SUBLANES - S5_UNROLL)
    carry_re = jnp.concatenate([p[0] for p in powers] + pad, axis=1)
    carry_im = jnp.concatenate([p[1] for p in powers] + pad, axis=1)
    step_re = jnp.concatenate([zero] * S5_UNROLL + [powers[-1][0]] * (SUBLANES - S5_UNROLL), axis=1)
    step_im = jnp.concatenate([zero] * S5_UNROLL + [powers[-1][1]] * (SUBLANES - S5_UNROLL), axis=1)
    coef = jnp.stack([carry_re, carry_im, step_re, step_im], axis=1)
    return wb, wc, coef.astype(F32)


def _rope_tables(positions):
    inv_freq = ROPE_THETA ** (-jnp.arange(0, ROPE_DIMS, 2, dtype=F32) / ROPE_DIMS)
    ang = positions.astype(F32).reshape(-1, 1) * inv_freq
    cos, sin = jnp.cos(ang), jnp.sin(ang)
    half = ROPE_DIMS // 2
    pad = DA_HEAD_DIM - ROPE_DIMS
    n = cos.shape[0]
    ones, zeros = jnp.ones((n, pad), F32), jnp.zeros((n, pad), F32)
    zh = jnp.zeros((n, half), F32)
    rc = jnp.concatenate([cos, cos, ones], axis=1)
    rs1 = jnp.concatenate([-sin, zh, zeros], axis=1)
    rs2 = jnp.concatenate([zh, sin, zeros], axis=1)
    rep = LANES // DA_HEAD_DIM
    return jnp.tile(rc, (1, rep)), jnp.tile(rs1, (1, rep)), jnp.tile(rs2, (1, rep))


def _tiles(L):
    return dict(chunk=min(256, L), tq=min(512, L), tk=min(512, L))


def kernel(x, mem, positions, norm_g, w_in, w_out, s5_lambda_re, s5_lambda_im, s5_log_dt, s5_b_re, s5_b_im, s5_c_re, s5_c_im, s5_d, s5_w_glu, da_lambda_q1, da_lambda_k1, da_lambda_q2, da_lambda_k2, da_subln_g, lru_conv_w, lru_conv_b, lru_w_a, lru_b_a, lru_w_x, lru_b_x, lru_lambda, mem_norm_g, w_mem_kv, final_norm_g):
    B, L, D = x.shape
    M = mem.shape[1]
    depth = w_in.shape[0]
    T = B * L
    t = _tiles(L)
    assert L % t["tk"] == 0 and L % t["chunk"] == 0

    gw = GROUP_WIDTH
    w_in_b = w_in.astype(BF16)
    w_out_b = w_out.astype(BF16)
    wglu_b = s5_w_glu.astype(BF16)
    wb, wc, coef = _s5_prep(s5_lambda_re, s5_lambda_im, s5_log_dt, s5_b_re, s5_b_im, s5_c_re, s5_c_im)
    s5_d2 = s5_d.reshape(depth, 1, gw)
    lamv = jnp.stack([da_lambda_q1, da_lambda_k1, da_lambda_q2, da_lambda_k2], axis=1)
    sg = da_subln_g.reshape(depth, 1, LANES)
    wg = jnp.concatenate([_block_diag(lru_w_a), _block_diag(lru_w_x)], axis=-1).astype(BF16)
    bg = jnp.concatenate([lru_b_a, lru_b_x], axis=-1).reshape(depth, 1, 2 * gw)
    lru_cb = lru_conv_b.reshape(depth, 1, gw)
    lru_lam = lru_lambda.reshape(depth, 1, gw)
    norm_g3 = norm_g.reshape(depth, 1, D)
    rc, rs1, rs2 = _rope_tables(positions)

    kv = _memkv(mem.reshape(B * M, D), mem_norm_g.reshape(depth, 1, D), w_mem_kv.astype(BF16), B, M)

    xf = x.reshape(T, D)
    fg = final_norm_g.reshape(1, D)
    for layer in range(depth):
        rest, qkv, y_a = _inproj_s5(xf, norm_g3, w_in_b, layer, rc, rs1, rs2,
                                    wb, coef, wc, s5_d2, wglu_b, B, L, t["chunk"])
        y_b = _attn(qkv, rest, lamv, sg, layer, B, L, t["tq"], t["tk"])
        xf = _tail(rest, y_a, y_b, xf, kv, lru_conv_w, lru_cb, wg, bg, lru_lam, w_out_b, fg,
                   layer, layer == depth - 1, B, L, M, t["chunk"])
    return xf.reshape(B, L, D)
```

```python
import functools
import math

import jax
import jax.numpy as jnp
from jax import lax
from jax.experimental import pallas as pl
from jax.experimental.pallas import tpu as pltpu

F32 = jnp.float32
BF16 = jnp.bfloat16

EPS = 1e-6
GROUP_WIDTH = 512
LANES = 128
SUBLANES = 8
S5_STATE = 64
S5_CH = 16
S5_BLOCKS = 4
S5_BLOCK_STATES = 512
S5_UNROLL = 4
S5_ROW_GROUPS = 2
TAIL_ROW_GROUPS = 1
DA_HEADS = 4
DA_HEAD_DIM = 64
ROPE_DIMS = 16
ROPE_THETA = 500000.0
LRU_BLOCKS = 8
LRU_C = 8.0
MEM_HEADS = 4
MEM_HEAD_DIM = 128
NEG_BIG = -1e30
LOG2E = math.log2(math.e)
Q_SCALE = DA_HEAD_DIM ** -0.5 * LOG2E
VMEM_LIMIT_BYTES = 56 * 1024 * 1024

W_COL_AU = 0
W_COL_Q = 2 * GROUP_WIDTH
W_COL_REST = 5 * GROUP_WIDTH
COL_BG, COL_CX, COL_CG, COL_MQ, COL_MG = range(5)
N_REST = 5 * GROUP_WIDTH


def _params(*sem):
    return pltpu.CompilerParams(dimension_semantics=sem, vmem_limit_bytes=VMEM_LIMIT_BYTES)


def _sigmoid(x):
    return 0.5 * jnp.tanh(0.5 * x) + 0.5


def _silu(x):
    return x * _sigmoid(x)


def _gelu_tanh(x):
    return 0.5 * x * (1.0 + jnp.tanh(math.sqrt(2.0 / math.pi) * (x + 0.044715 * (x * x * x))))


def _dot(a, b):
    return jnp.dot(a, b, preferred_element_type=F32)


def _dot_nt(a, b):
    return lax.dot_general(a, b, (((1,), (1,)), ((), ())), preferred_element_type=F32)


def _rms(x, g):
    ms = jnp.mean(x * x, axis=-1, keepdims=True)
    return x * lax.rsqrt(ms + EPS) * g


def _s5_input(u, wb_ref, us_ref, xs_ref, chunk):
    us_ref[0:SUBLANES, :] = jnp.zeros((SUBLANES, GROUP_WIDTH), F32)
    us_ref[SUBLANES:SUBLANES + chunk, :] = u
    row_in_tile = lax.broadcasted_iota(jnp.int32, (chunk, LANES), 0) % SUBLANES
    blk = 2 * S5_BLOCK_STATES
    for q in range(S5_BLOCKS):
        cols = slice(q * LANES, (q + 1) * LANES)
        parts = [u[:, cols].astype(BF16)]
        for d in range(1, S5_UNROLL):
            shifted = us_ref[SUBLANES - d:SUBLANES - d + chunk, cols]
            parts.append(jnp.where(row_in_tile >= d, shifted, 0.0).astype(BF16))
        xs_ref[:, q * blk:(q + 1) * blk] = _dot(jnp.concatenate(parts, axis=1), wb_ref[q])


def _s5_scan_out(u, gate, coef_ref, wc_ref, d_ref, wglu_ref, xs_ref, carry_ref, chunk):
    blk = 2 * S5_BLOCK_STATES

    def cmul_add(br, bi, ar, ai, sr, si):
        return br + (ar * sr - ai * si), bi + (ar * si + ai * sr)

    for r in range(chunk // SUBLANES):
        rows = slice(r * SUBLANES, (r + 1) * SUBLANES)
        for q in range(S5_BLOCKS):
            for j in range(S5_BLOCK_STATES // LANES):
                lr = q * blk + j * LANES
                li = lr + S5_BLOCK_STATES
                lc = q * S5_BLOCK_STATES + j * LANES
                xr, xi = cmul_add(xs_ref[rows, lr:lr + LANES], xs_ref[rows, li:li + LANES],
                                  coef_ref[0, :, lc:lc + LANES], coef_ref[1, :, lc:lc + LANES],
                                  carry_ref[:, lr:lr + LANES], carry_ref[:, li:li + LANES])
                xr, xi = cmul_add(xr, xi, coef_ref[2, :, lc:lc + LANES], coef_ref[3, :, lc:lc + LANES],
                                  pltpu.roll(xr, S5_UNROLL, axis=0), pltpu.roll(xi, S5_UNROLL, axis=0))
                xs_ref[rows, lr:lr + LANES] = xr
                xs_ref[rows, li:li + LANES] = xi
                carry_ref[:, lr:lr + LANES] = jnp.broadcast_to(xr[SUBLANES - 1:, :], (SUBLANES, LANES))
                carry_ref[:, li:li + LANES] = jnp.broadcast_to(xi[SUBLANES - 1:, :], (SUBLANES, LANES))

    ys = []
    for q in range(S5_BLOCKS):
        xb = xs_ref[:, q * blk:(q + 1) * blk].astype(BF16)
        ys.append(_dot(xb, wc_ref[q]))
    y = jnp.concatenate(ys, axis=1) + d_ref[...] * u
    y = _gelu_tanh(y)
    z = _dot(y.astype(BF16), wglu_ref[...])
    return z[:, :GROUP_WIDTH] * _sigmoid(z[:, GROUP_WIDTH:]) * _silu(gate)


def _inproj_s5_kernel(x_ref, g_ref, w_ref, rc_ref, rs1_ref, rs2_ref,
                      wb_ref, coef_ref, wc_ref, d_ref, wglu_ref,
                      rest_ref, qkv_ref, ya_ref, vt_ref, us_ref, xs_ref, carry_ref, *, chunk, n_sub):
    @pl.when(pl.program_id(1) == 0)
    def _():
        carry_ref[...] = jnp.zeros(carry_ref.shape, F32)

    gw = GROUP_WIDTH
    sub = chunk // n_sub

    def projections(h):
        rows = pl.ds(h * sub, sub)
        hb = _rms(x_ref[rows, :], g_ref[...]).astype(BF16)
        u = _dot(hb, w_ref[:, W_COL_AU:W_COL_AU + gw])
        _s5_input(u, wb_ref, us_ref.at[h], xs_ref.at[rows], sub)
        gate = _dot(hb, w_ref[:, W_COL_AU + gw:W_COL_AU + 2 * gw])
        rest_ref[rows, :] = _dot(hb, w_ref[:, W_COL_REST:W_COL_REST + N_REST])
        qk = _dot(hb, w_ref[:, W_COL_Q:W_COL_Q + 2 * gw])
        c, s1, s2 = rc_ref[rows, :], rs1_ref[rows, :], rs2_ref[rows, :]
        for j in range(2 * gw // LANES):
            t = qk[:, j * LANES:(j + 1) * LANES]
            t = (t * c + pltpu.roll(t, LANES - ROPE_DIMS // 2, axis=1) * s1
                 + pltpu.roll(t, ROPE_DIMS // 2, axis=1) * s2)
            if j < gw // LANES:
                t = t * Q_SCALE
            qkv_ref[rows, j * LANES:(j + 1) * LANES] = t.astype(BF16)
        v = _dot(hb, w_ref[:, W_COL_Q + 2 * gw:W_COL_Q + 3 * gw])
        qkv_ref[rows, 2 * gw:] = v.astype(BF16)
        vt_ref[:, h * sub:(h + 1) * sub] = v.T.astype(BF16)
        return u, gate

    fronts = [projections(h) for h in range(n_sub)]
    for h, (u, gate) in enumerate(fronts):
        rows = pl.ds(h * sub, sub)
        ya_ref[rows, :] = _s5_scan_out(u, gate, coef_ref, wc_ref, d_ref, wglu_ref,
                                       xs_ref.at[rows], carry_ref, sub).astype(BF16)


def _inproj_s5(x, g, w_all, layer, rc, rs1, rs2, wb, coef, wc, d, wglu, B, L, chunk, tk):
    T, D = x.shape
    n_in = w_all.shape[-1]
    nc = L // chunk
    per_kv = tk // chunk
    n_state = S5_BLOCKS * 2 * S5_BLOCK_STATES

    def row_blk(b, c):
        return (b * nc + c, 0)

    return pl.pallas_call(
        functools.partial(_inproj_s5_kernel, chunk=chunk, n_sub=S5_ROW_GROUPS),
        grid=(B, nc),
        in_specs=[
            pl.BlockSpec((chunk, D), row_blk),
            pl.BlockSpec((None, 1, D), lambda b, c: (layer, 0, 0)),
            pl.BlockSpec((None, D, n_in), lambda b, c: (layer, 0, 0), pipeline_mode=pl.Buffered(1)),
            pl.BlockSpec((chunk, LANES), row_blk),
            pl.BlockSpec((chunk, LANES), row_blk),
            pl.BlockSpec((chunk, LANES), row_blk),
            pl.BlockSpec((None,) + wb.shape[1:], lambda b, c: (layer, 0, 0, 0)),
            pl.BlockSpec((None,) + coef.shape[1:], lambda b, c: (layer, 0, 0, 0)),
            pl.BlockSpec((None,) + wc.shape[1:], lambda b, c: (layer, 0, 0, 0)),
            pl.BlockSpec((None, 1, GROUP_WIDTH), lambda b, c: (layer, 0, 0)),
            pl.BlockSpec((None,) + wglu.shape[1:], lambda b, c: (layer, 0, 0)),
        ],
        out_specs=[
            pl.BlockSpec((chunk, N_REST), row_blk),
            pl.BlockSpec((chunk, 3 * GROUP_WIDTH), row_blk),
            pl.BlockSpec((chunk, GROUP_WIDTH), row_blk),
            pl.BlockSpec((None, None, GROUP_WIDTH, chunk), lambda b, c: (b, c // per_kv, 0, c % per_kv)),
        ],
        out_shape=[jax.ShapeDtypeStruct((T, N_REST), F32),
                   jax.ShapeDtypeStruct((T, 3 * GROUP_WIDTH), BF16),
                   jax.ShapeDtypeStruct((T, GROUP_WIDTH), BF16),
                   jax.ShapeDtypeStruct((B, L // tk, GROUP_WIDTH, tk), BF16)],
        scratch_shapes=[pltpu.VMEM((S5_ROW_GROUPS, chunk // S5_ROW_GROUPS + SUBLANES, GROUP_WIDTH), F32),
                        pltpu.VMEM((chunk, n_state), F32),
                        pltpu.VMEM((SUBLANES, n_state), F32)],
        compiler_params=_params("arbitrary", "arbitrary"),
        name="inproj_s5",
    )(x, g, w_all, rc, rs1, rs2, wb, coef, wc, d, wglu)


def _attn_kernel(lamv_ref, q_ref, k_ref, vt_ref, g_ref, sg_ref, o_ref,
                 qs_ref, s_ref, m_ref, acc_ref, *, tq, tk, lam_init):
    i = pl.program_id(2)
    lane = lax.broadcasted_iota(jnp.int32, (tq, LANES), 1)
    for st in range(2):
        q = q_ref[:, st * LANES:(st + 1) * LANES]
        zero = jnp.zeros_like(q)
        qs_ref[st, 0:tq, :] = jnp.where(lane < DA_HEAD_DIM, q, zero)
        qs_ref[st, tq:, :] = jnp.where(lane >= DA_HEAD_DIM, q, zero)
    m_ref[...] = jnp.full(m_ref.shape, NEG_BIG, F32)
    acc_ref[...] = jnp.zeros(acc_ref.shape, F32)
    ones = jnp.ones((2 * SUBLANES, tk), BF16)

    def scores(st, kb):
        k = k_ref[pl.ds(pl.multiple_of(kb * tk, tk), tk), st * LANES:(st + 1) * LANES]
        s_ref[st] = _dot_nt(k, qs_ref[st])

    def softmax_pv(st, kb, masked):
        vt = jnp.concatenate([vt_ref[kb, st * LANES:(st + 1) * LANES, :], ones], axis=0)
        s = s_ref[st]
        if masked:
            key = lax.broadcasted_iota(jnp.int32, s.shape, 0)
            col = lax.broadcasted_iota(jnp.int32, s.shape, 1)
            qpos = i * tq + jnp.where(col >= tq, col - tq, col)
            s = jnp.where(kb * tk + key <= qpos, s, NEG_BIG)
        m_prev = m_ref[st]
        m_new = jnp.maximum(m_prev, jnp.max(s, axis=0, keepdims=True))
        alpha = jnp.exp2(m_prev - m_new)
        p = jnp.exp2(s - m_new[0:1, :])
        acc_ref[st] = alpha[0:1, :] * acc_ref[st] + _dot(vt, p.astype(BF16))
        m_ref[st] = m_new

    scores(0, i)
    scores(1, i)
    softmax_pv(0, i, True)
    scores(0, 0)
    softmax_pv(1, i, True)

    def step(kb):
        scores(1, kb)
        softmax_pv(0, kb, False)
        scores(0, kb + 1)
        softmax_pv(1, kb, False)

    def two_steps(jj, carry):
        step(2 * jj)
        step(2 * jj + 1)
        return carry

    lax.fori_loop(0, lax.div(i, 2), two_steps, 0)

    @pl.when(lax.rem(i, 2) == 1)
    def _():
        step(i - 1)

    lv = lamv_ref[...]
    lam = (jnp.exp(jnp.sum(lv[0:1, :] * lv[1:2, :], axis=-1, keepdims=True))
           - jnp.exp(jnp.sum(lv[2:3, :] * lv[3:4, :], axis=-1, keepdims=True)) + lam_init)
    for st in range(2):
        acc = acc_ref[st]
        inv_l = 1.0 / acc[LANES:LANES + 1, :]
        ot = acc[:LANES, :tq] * inv_l[:, :tq] - lam * (acc[:LANES, tq:] * inv_l[:, tq:])
        o = _rms(ot.T, sg_ref[...]) * (1.0 - lam_init)
        cols = slice(st * LANES, (st + 1) * LANES)
        o_ref[:, cols] = (o * _silu(g_ref[:, cols])).astype(BF16)


def _attn(qkv, vt, rest, lamv, sg, layer, B, L, tq, tk):
    T = B * L
    nq = L // tq
    assert tq == tk, "the causal block bookkeeping assumes square score blocks"
    lam_init = 0.8 - 0.6 * math.exp(-0.3 * layer)
    pair = 2 * LANES
    n_pair = GROUP_WIDTH // pair
    gate_blk = COL_BG * n_pair
    return pl.pallas_call(
        functools.partial(_attn_kernel, tq=tq, tk=tk, lam_init=lam_init),
        grid=(B, n_pair, nq),
        in_specs=[
            pl.BlockSpec((None, 4, DA_HEAD_DIM), lambda b, h, i: (layer, 0, 0)),
            pl.BlockSpec((tq, pair), lambda b, h, i: (b * nq + i, h)),
            pl.BlockSpec((L, pair), lambda b, h, i: (b, n_pair + h)),
            pl.BlockSpec((None, L // tk, pair, tk), lambda b, h, i: (b, 0, h, 0)),
            pl.BlockSpec((tq, pair), lambda b, h, i: (b * nq + i, gate_blk + h)),
            pl.BlockSpec((None, 1, LANES), lambda b, h, i: (layer, 0, 0)),
        ],
        out_specs=pl.BlockSpec((tq, pair), lambda b, h, i: (b * nq + i, h)),
        out_shape=jax.ShapeDtypeStruct((T, GROUP_WIDTH), BF16),
        scratch_shapes=[
            pltpu.VMEM((2, 2 * tq, LANES), BF16),
            pltpu.VMEM((2, tk, 2 * tq), F32),
            pltpu.VMEM((2, SUBLANES, 2 * tq), F32),
            pltpu.VMEM((2, LANES + 2 * SUBLANES, 2 * tq), F32),
        ],
        compiler_params=_params("parallel", "parallel", "arbitrary"),
        name="diffattn",
    )(lamv, qkv, qkv, vt, rest, sg)


def _memkv_kernel(mem_ref, g_ref, w_ref, o_ref):
    o_ref[...] = _dot(_rms(mem_ref[...], g_ref[...]).astype(BF16), w_ref[...]).astype(BF16)


def _memkv(mem2d, g, w, B, M):
    depth, D, n_kv = w.shape
    return pl.pallas_call(
        _memkv_kernel,
        grid=(depth, B),
        in_specs=[
            pl.BlockSpec((M, D), lambda l, b: (b, 0)),
            pl.BlockSpec((None, 1, D), lambda l, b: (l, 0, 0)),
            pl.BlockSpec((None, D, n_kv), lambda l, b: (l, 0, 0)),
        ],
        out_specs=pl.BlockSpec((None, M, n_kv), lambda l, b: (l, b, 0)),
        out_shape=jax.ShapeDtypeStruct((depth, B * M, n_kv), BF16),
        compiler_params=_params("parallel", "parallel"),
        name="memkv",
    )(mem2d, g, w)


def _lru_gates(x, cw_ref, cb_ref, wg_ref, bg_ref, xe_ref, row0, n_rows):
    base = SUBLANES + row0
    xe_ref[base:base + n_rows, :] = x
    xc = (cw_ref[3:4, :] * x
          + cw_ref[2:3, :] * xe_ref[base - 1:base - 1 + n_rows, :]
          + cw_ref[1:2, :] * xe_ref[base - 2:base - 2 + n_rows, :]
          + cw_ref[0:1, :] * xe_ref[base - 3:base - 3 + n_rows, :]
          + cb_ref[...])
    return xc, _dot(xc.astype(BF16), wg_ref[...]) + bg_ref[...]


def _lru_scan(xc, z, gate, lam_ref, a_ref, b_ref, h_ref, carry_ref, chunk):
    r = _sigmoid(z[:, :GROUP_WIDTH])
    gate_i = _sigmoid(z[:, GROUP_WIDTH:])
    nl = -lam_ref[...]
    softplus = jnp.maximum(nl, 0.0) + jnp.log1p(jnp.exp(-jnp.abs(nl)))
    log_a = (-LRU_C) * r * softplus
    a = jnp.exp(log_a)
    a_ref[...] = a
    mult = jnp.sqrt(-jnp.tanh(log_a) * (1.0 + a * a))
    b_ref[...] = mult * (gate_i * xc)

    row = lax.broadcasted_iota(jnp.int32, (SUBLANES, LANES), 0)
    for t in range(chunk // SUBLANES):
        rows = slice(t * SUBLANES, (t + 1) * SUBLANES)
        for j in range(GROUP_WIDTH // LANES):
            cols = slice(j * LANES, (j + 1) * LANES)
            a = a_ref[rows, cols]
            b = b_ref[rows, cols] + jnp.where(row == 0, a, 0.0) * carry_ref[:, cols]
            w = jnp.where(row == 0, 0.0, a)
            for shift in (1, 2, 4):
                b = b + w * pltpu.roll(b, shift, axis=0)
                if shift != 4:
                    w = w * pltpu.roll(w, shift, axis=0)
            h_ref[rows, cols] = b
            carry_ref[:, cols] = jnp.broadcast_to(b[SUBLANES - 1:, :], (SUBLANES, LANES))
    return h_ref[...] * _silu(gate)


def _memattn_body(q, gate, kv_ref):
    qb = q.astype(BF16)
    scale = MEM_HEAD_DIM ** -0.5 * LOG2E
    outs = []
    for h in range(MEM_HEADS):
        cols = slice(h * MEM_HEAD_DIM, (h + 1) * MEM_HEAD_DIM)
        k = kv_ref[:, h * MEM_HEAD_DIM:(h + 1) * MEM_HEAD_DIM]
        v = kv_ref[:, GROUP_WIDTH + h * MEM_HEAD_DIM:GROUP_WIDTH + (h + 1) * MEM_HEAD_DIM]
        s = _dot_nt(qb[:, cols], k) * scale
        e = jnp.exp2(s - jnp.max(s, axis=-1, keepdims=True))
        outs.append(_dot(e.astype(BF16), v) / jnp.sum(e, axis=-1, keepdims=True))
    return jnp.concatenate(outs, axis=1) * _silu(gate)


def _tail_kernel(cx_ref, cg_ref, mq_ref, mg_ref, ya_ref, yb_ref, x_ref, kv_ref,
                 cw_ref, cb_ref, wg_ref, bg_ref, lam_ref, w_ref, fg_ref, o_ref,
                 xe_ref, a_ref, b_ref, h_ref, carry_ref, *, chunk, n_sub, final):
    c = pl.program_id(1)

    @pl.when(c == 0)
    def _():
        xe_ref[0:SUBLANES, :] = jnp.zeros((SUBLANES, GROUP_WIDTH), F32)
        carry_ref[...] = jnp.zeros(carry_ref.shape, F32)

    @pl.when(c > 0)
    def _():
        xe_ref[0:SUBLANES, :] = xe_ref[chunk:chunk + SUBLANES, :]

    gw = GROUP_WIDTH
    sub = chunk // n_sub

    def front(h):
        rows = pl.ds(h * sub, sub)
        xc, z = _lru_gates(cx_ref[rows, :], cw_ref, cb_ref, wg_ref, bg_ref, xe_ref, h * sub, sub)
        acc = (x_ref[rows, :] + _dot(ya_ref[rows, :], w_ref[0:gw, :])
               + _dot(yb_ref[rows, :], w_ref[gw:2 * gw, :]))
        y_m = _memattn_body(mq_ref[rows, :], mg_ref[rows, :], kv_ref)
        return xc, z, acc + _dot(y_m.astype(BF16), w_ref[3 * gw:4 * gw, :])

    fronts = [front(h) for h in range(n_sub)]
    for h, (xc, z, acc) in enumerate(fronts):
        rows = pl.ds(h * sub, sub)
        y_c = _lru_scan(xc, z, cg_ref[rows, :], lam_ref, a_ref.at[rows], b_ref.at[rows],
                        h_ref.at[rows], carry_ref, sub)
        acc = acc + _dot(y_c.astype(BF16), w_ref[2 * gw:3 * gw, :])
        o_ref[rows, :] = _rms(acc, fg_ref[...]) if final else acc


def _tail(rest, y_a, y_b, x, kv, cw, cb, wg, bg, lam, w_all, fg, layer, final, B, L, M, chunk):
    T, D = x.shape
    nc = L // chunk
    gw = GROUP_WIDTH

    def row_blk(b, c):
        return (b * nc + c, 0)

    def rest_spec(col):
        return pl.BlockSpec((chunk, gw), lambda b, c: (b * nc + c, col))

    def per_layer(shape):
        return pl.BlockSpec((None,) + shape, lambda b, c: (layer,) + (0,) * len(shape))

    return pl.pallas_call(
        functools.partial(_tail_kernel, chunk=chunk, n_sub=TAIL_ROW_GROUPS, final=final),
        grid=(B, nc),
        in_specs=[
            rest_spec(COL_CX), rest_spec(COL_CG), rest_spec(COL_MQ), rest_spec(COL_MG),
            pl.BlockSpec((chunk, gw), row_blk),
            pl.BlockSpec((chunk, gw), row_blk),
            pl.BlockSpec((chunk, D), row_blk),
            pl.BlockSpec((None, M, 2 * gw), lambda b, c: (layer, b, 0)),
            per_layer((4, gw)), per_layer((1, gw)), per_layer((gw, 2 * gw)), per_layer((1, 2 * gw)),
            per_layer((1, gw)), per_layer(w_all.shape[1:]),
            pl.BlockSpec((1, D), lambda b, c: (0, 0)),
        ],
        out_specs=pl.BlockSpec((chunk, D), row_blk),
        out_shape=jax.ShapeDtypeStruct((T, D), F32),
        scratch_shapes=[
            pltpu.VMEM((chunk + SUBLANES, gw), F32),
            pltpu.VMEM((chunk, gw), F32),
            pltpu.VMEM((chunk, gw), F32),
            pltpu.VMEM((chunk, gw), F32),
            pltpu.VMEM((SUBLANES, gw), F32),
        ],
        compiler_params=_params("arbitrary", "arbitrary"),
        name="tail",
    )(rest, rest, rest, rest, y_a, y_b, x, kv, cw, cb, wg, bg, lam, w_all, fg)


def _block_diag(blocks):
    n = blocks.shape[-3]
    eye = jnp.eye(n, dtype=blocks.dtype)
    out = jnp.einsum('...nrc,nm->...nrmc', blocks, eye)
    return out.reshape(blocks.shape[:-3] + (n * blocks.shape[-2], n * blocks.shape[-1]))


def _s5_prep(lam_re, lam_im, log_dt, b_re, b_im, c_re, c_im):
    depth = lam_re.shape[0]
    dt = jnp.exp(log_dt)[..., None]
    mag = jnp.exp(lam_re * dt)
    abar_re = mag * jnp.cos(lam_im * dt)
    abar_im = mag * jnp.sin(lam_im * dt)
    den = lam_re * lam_re + lam_im * lam_im
    nr, ni = abar_re - 1.0, abar_im
    f_re = (nr * lam_re + ni * lam_im) / den
    f_im = (ni * lam_re - nr * lam_im) / den
    bb_re = f_re[..., None] * b_re - f_im[..., None] * b_im
    bb_im = f_re[..., None] * b_im + f_im[..., None] * b_re
    gpb = LANES // S5_CH

    def in_blocks(bb):
        return _block_diag(jnp.swapaxes(bb, -1, -2).reshape(depth, S5_BLOCKS, gpb, S5_CH, S5_STATE))

    def out_blocks(cc):
        return _block_diag(jnp.swapaxes(cc, -1, -2).reshape(depth, S5_BLOCKS, gpb, S5_STATE, S5_CH))

    pr, pi = jnp.ones_like(abar_re), jnp.zeros_like(abar_im)
    wb_d, powers = [], []
    for _ in range(S5_UNROLL):
        wd_re = pr[..., None] * bb_re - pi[..., None] * bb_im
        wd_im = pr[..., None] * bb_im + pi[..., None] * bb_re
        wb_d.append(jnp.concatenate([in_blocks(wd_re), in_blocks(wd_im)], axis=-1))
        pr, pi = pr * abar_re - pi * abar_im, pr * abar_im + pi * abar_re
        powers.append((pr.reshape(depth, 1, -1), pi.reshape(depth, 1, -1)))
    wb = jnp.concatenate(wb_d, axis=-2).astype(BF16)
    wc = jnp.concatenate([out_blocks(c_re), out_blocks(-c_im)], axis=-2).astype(BF16)

    zero = jnp.zeros_like(powers[0][0])
    pad = [zero] * (SUBLANES - S5_UNROLL)
    carry_re = jnp.concatenate([p[0] for p in powers] + pad, axis=1)
    carry_im = jnp.concatenate([p[1] for p in powers] + pad, axis=1)
    step_re = jnp.concatenate([zero] * S5_UNROLL + [powers[-1][0]] * (SUBLANES - S5_UNROLL), axis=1)
    step_im = jnp.concatenate([zero] * S5_UNROLL + [powers[-1][1]] * (SUBLANES - S5_UNROLL), axis=1)
    coef = jnp.stack([carry_re, carry_im, step_re, step_im], axis=1)
    return wb, wc, coef.astype(F32)


def _rope_tables(positions):
    inv_freq = ROPE_THETA ** (-jnp.arange(0, ROPE_DIMS, 2, dtype=F32) / ROPE_DIMS)
    ang = positions.astype(F32).reshape(-1, 1) * inv_freq
    cos, sin = jnp.cos(ang), jnp.sin(ang)
    half = ROPE_DIMS // 2
    pad = DA_HEAD_DIM - ROPE_DIMS
    n = cos.shape[0]
    ones, zeros = jnp.ones((n, pad), F32), jnp.zeros((n, pad), F32)
    zh = jnp.zeros((n, half), F32)
    rc = jnp.concatenate([cos, cos, ones], axis=1)
    rs1 = jnp.concatenate([-sin, zh, zeros], axis=1)
    rs2 = jnp.concatenate([zh, sin, zeros], axis=1)
    rep = LANES // DA_HEAD_DIM
    return jnp.tile(rc, (1, rep)), jnp.tile(rs1, (1, rep)), jnp.tile(rs2, (1, rep))


def _tiles(L):
    return dict(chunk=min(256, L), tq=min(512, L), tk=min(512, L))


def kernel(x, mem, positions, norm_g, w_in, w_out, s5_lambda_re, s5_lambda_im, s5_log_dt, s5_b_re, s5_b_im, s5_c_re, s5_c_im, s5_d, s5_w_glu, da_lambda_q1, da_lambda_k1, da_lambda_q2, da_lambda_k2, da_subln_g, lru_conv_w, lru_conv_b, lru_w_a, lru_b_a, lru_w_x, lru_b_x, lru_lambda, mem_norm_g, w_mem_kv, final_norm_g):
    B, L, D = x.shape
    M = mem.shape[1]
    depth = w_in.shape[0]
    T = B * L
    t = _tiles(L)
    assert L % t["tk"] == 0 and L % t["chunk"] == 0

    gw = GROUP_WIDTH
    w_in_b = w_in.astype(BF16)
    w_out_b = w_out.astype(BF16)
    wglu_b = s5_w_glu.astype(BF16)
    wb, wc, coef = _s5_prep(s5_lambda_re, s5_lambda_im, s5_log_dt, s5_b_re, s5_b_im, s5_c_re, s5_c_im)
    s5_d2 = s5_d.reshape(depth, 1, gw)
    lamv = jnp.stack([da_lambda_q1, da_lambda_k1, da_lambda_q2, da_lambda_k2], axis=1)
    sg = da_subln_g.reshape(depth, 1, LANES)
    wg = jnp.concatenate([_block_diag(lru_w_a), _block_diag(lru_w_x)], axis=-1).astype(BF16)
    bg = jnp.concatenate([lru_b_a, lru_b_x], axis=-1).reshape(depth, 1, 2 * gw)
    lru_cb = lru_conv_b.reshape(depth, 1, gw)
    lru_lam = lru_lambda.reshape(depth, 1, gw)
    norm_g3 = norm_g.reshape(depth, 1, D)
    rc, rs1, rs2 = _rope_tables(positions)

    kv = _memkv(mem.reshape(B * M, D), mem_norm_g.reshape(depth, 1, D), w_mem_kv.astype(BF16), B, M)

    xf = x.reshape(T, D)
    fg = final_norm_g.reshape(1, D)
    for layer in range(depth):
        rest, qkv, y_a, vt = _inproj_s5(xf, norm_g3, w_in_b, layer, rc, rs1, rs2,
                                        wb, coef, wc, s5_d2, wglu_b, B, L, t["chunk"], t["tk"])
        y_b = _attn(qkv, vt, rest, lamv, sg, layer, B, L, t["tq"], t["tk"])
        xf = _tail(rest, y_a, y_b, xf, kv, lru_conv_w, lru_cb, wg, bg, lru_lam, w_out_b, fg,
                   layer, layer == depth - 1, B, L, M, t["chunk"])
    return xf.reshape(B, L, D)
```

```python
import functools
import math

import jax
import jax.numpy as jnp
from jax import lax
from jax.experimental import pallas as pl
from jax.experimental.pallas import tpu as pltpu

F32 = jnp.float32
BF16 = jnp.bfloat16

EPS = 1e-6
GROUP_WIDTH = 512
LANES = 128
SUBLANES = 8
S5_STATE = 64
S5_CH = 16
S5_BLOCKS = 4
S5_BLOCK_STATES = 512
S5_UNROLL = 4
S5_ROW_GROUPS = 2
TAIL_ROW_GROUPS = 1
DA_HEADS = 4
DA_HEAD_DIM = 64
ROPE_DIMS = 16
ROPE_THETA = 500000.0
LRU_BLOCKS = 8
LRU_C = 8.0
MEM_HEADS = 4
MEM_HEAD_DIM = 128
NEG_BIG = -1e30
LOG2E = math.log2(math.e)
Q_SCALE = DA_HEAD_DIM ** -0.5 * LOG2E
VMEM_LIMIT_BYTES = 56 * 1024 * 1024

W_COL_AU = 0
W_COL_Q = 2 * GROUP_WIDTH
W_COL_REST = 5 * GROUP_WIDTH
COL_BG, COL_CX, COL_CG, COL_MQ, COL_MG = range(5)
N_REST = 5 * GROUP_WIDTH


def _params(*sem):
    return pltpu.CompilerParams(dimension_semantics=sem, vmem_limit_bytes=VMEM_LIMIT_BYTES)


def _sigmoid(x):
    return 0.5 * jnp.tanh(0.5 * x) + 0.5


def _silu(x):
    return x * _sigmoid(x)


def _gelu_tanh(x):
    return 0.5 * x * (1.0 + jnp.tanh(math.sqrt(2.0 / math.pi) * (x + 0.044715 * (x * x * x))))


def _dot(a, b):
    return jnp.dot(a, b, preferred_element_type=F32)


def _dot_nt(a, b):
    return lax.dot_general(a, b, (((1,), (1,)), ((), ())), preferred_element_type=F32)


def _rms(x, g):
    ms = jnp.mean(x * x, axis=-1, keepdims=True)
    return x * lax.rsqrt(ms + EPS) * g


def _s5_input(u, wb_ref, us_ref, xs_ref, chunk):
    us_ref[0:SUBLANES, :] = jnp.zeros((SUBLANES, GROUP_WIDTH), F32)
    us_ref[SUBLANES:SUBLANES + chunk, :] = u
    row_in_tile = lax.broadcasted_iota(jnp.int32, (chunk, LANES), 0) % SUBLANES
    blk = 2 * S5_BLOCK_STATES
    for q in range(S5_BLOCKS):
        cols = slice(q * LANES, (q + 1) * LANES)
        parts = [u[:, cols].astype(BF16)]
        for d in range(1, S5_UNROLL):
            shifted = us_ref[SUBLANES - d:SUBLANES - d + chunk, cols]
            parts.append(jnp.where(row_in_tile >= d, shifted, 0.0).astype(BF16))
        xs_ref[:, q * blk:(q + 1) * blk] = _dot(jnp.concatenate(parts, axis=1), wb_ref[q])


def _s5_scan_out(u, gate, coef_ref, wc_ref, d_ref, wglu_ref, xs_ref, carry_ref, chunk):
    blk = 2 * S5_BLOCK_STATES

    def cmul_add(br, bi, ar, ai, sr, si):
        return br + (ar * sr - ai * si), bi + (ar * si + ai * sr)

    for r in range(chunk // SUBLANES):
        rows = slice(r * SUBLANES, (r + 1) * SUBLANES)
        for q in range(S5_BLOCKS):
            for j in range(S5_BLOCK_STATES // LANES):
                lr = q * blk + j * LANES
                li = lr + S5_BLOCK_STATES
                lc = q * S5_BLOCK_STATES + j * LANES
                xr, xi = cmul_add(xs_ref[rows, lr:lr + LANES], xs_ref[rows, li:li + LANES],
                                  coef_ref[0, :, lc:lc + LANES], coef_ref[1, :, lc:lc + LANES],
                                  carry_ref[:, lr:lr + LANES], carry_ref[:, li:li + LANES])
                xr, xi = cmul_add(xr, xi, coef_ref[2, :, lc:lc + LANES], coef_ref[3, :, lc:lc + LANES],
                                  pltpu.roll(xr, S5_UNROLL, axis=0), pltpu.roll(xi, S5_UNROLL, axis=0))
                xs_ref[rows, lr:lr + LANES] = xr
                xs_ref[rows, li:li + LANES] = xi
                carry_ref[:, lr:lr + LANES] = jnp.broadcast_to(xr[SUBLANES - 1:, :], (SUBLANES, LANES))
                carry_ref[:, li:li + LANES] = jnp.broadcast_to(xi[SUBLANES - 1:, :], (SUBLANES, LANES))

    ys = []
    for q in range(S5_BLOCKS):
        xb = xs_ref[:, q * blk:(q + 1) * blk].astype(BF16)
        ys.append(_dot(xb, wc_ref[q]))
    y = jnp.concatenate(ys, axis=1) + d_ref[...] * u
    y = _gelu_tanh(y)
    z = _dot(y.astype(BF16), wglu_ref[...])
    return z[:, :GROUP_WIDTH] * _sigmoid(z[:, GROUP_WIDTH:]) * _silu(gate)


def _inproj_s5_kernel(x_ref, g_ref, w_ref, rc_ref, rs1_ref, rs2_ref,
                      wb_ref, coef_ref, wc_ref, d_ref, wglu_ref,
                      rest_ref, qkv_ref, ya_ref, vt_ref, us_ref, xs_ref, carry_ref, *, chunk, n_sub):
    @pl.when(pl.program_id(1) == 0)
    def _():
        carry_ref[...] = jnp.zeros(carry_ref.shape, F32)

    gw = GROUP_WIDTH
    sub = chunk // n_sub

    def projections(h):
        rows = pl.ds(h * sub, sub)
        hb = _rms(x_ref[rows, :], g_ref[...]).astype(BF16)
        u = _dot(hb, w_ref[:, W_COL_AU:W_COL_AU + gw])
        _s5_input(u, wb_ref, us_ref.at[h], xs_ref.at[rows], sub)
        gate = _dot(hb, w_ref[:, W_COL_AU + gw:W_COL_AU + 2 * gw])
        rest_ref[rows, :] = _dot(hb, w_ref[:, W_COL_REST:W_COL_REST + N_REST])
        qk = _dot(hb, w_ref[:, W_COL_Q:W_COL_Q + 2 * gw])
        c, s1, s2 = rc_ref[rows, :], rs1_ref[rows, :], rs2_ref[rows, :]
        for j in range(2 * gw // LANES):
            t = qk[:, j * LANES:(j + 1) * LANES]
            t = (t * c + pltpu.roll(t, LANES - ROPE_DIMS // 2, axis=1) * s1
                 + pltpu.roll(t, ROPE_DIMS // 2, axis=1) * s2)
            if j < gw // LANES:
                t = t * Q_SCALE
            qkv_ref[rows, j * LANES:(j + 1) * LANES] = t.astype(BF16)
        v = _dot(hb, w_ref[:, W_COL_Q + 2 * gw:W_COL_Q + 3 * gw])
        qkv_ref[rows, 2 * gw:] = v.astype(BF16)
        vt_ref[:, h * sub:(h + 1) * sub] = v.T.astype(BF16)
        return u, gate

    fronts = [projections(h) for h in range(n_sub)]
    for h, (u, gate) in enumerate(fronts):
        rows = pl.ds(h * sub, sub)
        ya_ref[rows, :] = _s5_scan_out(u, gate, coef_ref, wc_ref, d_ref, wglu_ref,
                                       xs_ref.at[rows], carry_ref, sub).astype(BF16)


def _inproj_s5(x, g, w_all, layer, rc, rs1, rs2, wb, coef, wc, d, wglu, B, L, chunk, tk):
    T, D = x.shape
    n_in = w_all.shape[-1]
    nc = L // chunk
    per_kv = tk // chunk
    n_state = S5_BLOCKS * 2 * S5_BLOCK_STATES

    def row_blk(b, c):
        return (b * nc + c, 0)

    return pl.pallas_call(
        functools.partial(_inproj_s5_kernel, chunk=chunk, n_sub=S5_ROW_GROUPS),
        grid=(B, nc),
        in_specs=[
            pl.BlockSpec((chunk, D), row_blk),
            pl.BlockSpec((None, 1, D), lambda b, c: (layer, 0, 0)),
            pl.BlockSpec((None, D, n_in), lambda b, c: (layer, 0, 0), pipeline_mode=pl.Buffered(1)),
            pl.BlockSpec((chunk, LANES), row_blk),
            pl.BlockSpec((chunk, LANES), row_blk),
            pl.BlockSpec((chunk, LANES), row_blk),
            pl.BlockSpec((None,) + wb.shape[1:], lambda b, c: (layer, 0, 0, 0)),
            pl.BlockSpec((None,) + coef.shape[1:], lambda b, c: (layer, 0, 0, 0)),
            pl.BlockSpec((None,) + wc.shape[1:], lambda b, c: (layer, 0, 0, 0)),
            pl.BlockSpec((None, 1, GROUP_WIDTH), lambda b, c: (layer, 0, 0)),
            pl.BlockSpec((None,) + wglu.shape[1:], lambda b, c: (layer, 0, 0)),
        ],
        out_specs=[
            pl.BlockSpec((chunk, N_REST), row_blk),
            pl.BlockSpec((chunk, 3 * GROUP_WIDTH), row_blk),
            pl.BlockSpec((chunk, GROUP_WIDTH), row_blk),
            pl.BlockSpec((None, None, GROUP_WIDTH, chunk), lambda b, c: (b, c // per_kv, 0, c % per_kv)),
        ],
        out_shape=[jax.ShapeDtypeStruct((T, N_REST), F32),
                   jax.ShapeDtypeStruct((T, 3 * GROUP_WIDTH), BF16),
                   jax.ShapeDtypeStruct((T, GROUP_WIDTH), BF16),
                   jax.ShapeDtypeStruct((B, L // tk, GROUP_WIDTH, tk), BF16)],
        scratch_shapes=[pltpu.VMEM((S5_ROW_GROUPS, chunk // S5_ROW_GROUPS + SUBLANES, GROUP_WIDTH), F32),
                        pltpu.VMEM((chunk, n_state), F32),
                        pltpu.VMEM((SUBLANES, n_state), F32)],
        compiler_params=_params("arbitrary", "arbitrary"),
        name="inproj_s5",
    )(x, g, w_all, rc, rs1, rs2, wb, coef, wc, d, wglu)


def _attn_kernel(lamv_ref, q_ref, k_ref, vt_ref, g_ref, sg_ref, o_ref,
                 qs_ref, s_ref, smax_ref, m_ref, acc_ref, *, tq, tk, lam_init):
    i = pl.program_id(2)
    lane = lax.broadcasted_iota(jnp.int32, (tq, LANES), 1)
    for st in range(2):
        q = q_ref[:, st * LANES:(st + 1) * LANES]
        zero = jnp.zeros_like(q)
        qs_ref[st, 0:tq, :] = jnp.where(lane < DA_HEAD_DIM, q, zero)
        qs_ref[st, tq:, :] = jnp.where(lane >= DA_HEAD_DIM, q, zero)
    m_ref[...] = jnp.full(m_ref.shape, NEG_BIG, F32)
    acc_ref[...] = jnp.zeros(acc_ref.shape, F32)
    ones = jnp.ones((2 * SUBLANES, tk), BF16)

    def scores(st, kb, masked=False):
        k = k_ref[pl.ds(pl.multiple_of(kb * tk, tk), tk), st * LANES:(st + 1) * LANES]
        s = _dot_nt(k, qs_ref[st])
        if masked:
            key = lax.broadcasted_iota(jnp.int32, s.shape, 0)
            col = lax.broadcasted_iota(jnp.int32, s.shape, 1)
            qpos = i * tq + jnp.where(col >= tq, col - tq, col)
            s = jnp.where(kb * tk + key <= qpos, s, NEG_BIG)
        s_ref[st] = s
        smax_ref[st] = jnp.broadcast_to(jnp.max(s, axis=0, keepdims=True), smax_ref.shape[1:])

    def softmax_pv(st, kb):
        vt = jnp.concatenate([vt_ref[kb, st * LANES:(st + 1) * LANES, :], ones], axis=0)
        m_prev = m_ref[st]
        m_new = jnp.maximum(m_prev, smax_ref[st])
        alpha = jnp.exp2(m_prev - m_new)
        p = jnp.exp2(s_ref[st] - m_new[0:1, :])
        acc_ref[st] = alpha[0:1, :] * acc_ref[st] + _dot(vt, p.astype(BF16))
        m_ref[st] = m_new

    scores(0, i, masked=True)
    scores(1, i, masked=True)
    softmax_pv(0, i)
    scores(0, 0)
    softmax_pv(1, i)

    def step(kb):
        scores(1, kb)
        softmax_pv(0, kb)
        scores(0, kb + 1)
        softmax_pv(1, kb)

    def two_steps(jj, carry):
        step(2 * jj)
        step(2 * jj + 1)
        return carry

    lax.fori_loop(0, lax.div(i, 2), two_steps, 0)

    @pl.when(lax.rem(i, 2) == 1)
    def _():
        step(i - 1)

    lv = lamv_ref[...]
    lam = (jnp.exp(jnp.sum(lv[0:1, :] * lv[1:2, :], axis=-1, keepdims=True))
           - jnp.exp(jnp.sum(lv[2:3, :] * lv[3:4, :], axis=-1, keepdims=True)) + lam_init)
    for st in range(2):
        acc = acc_ref[st]
        inv_l = 1.0 / acc[LANES:LANES + 1, :]
        ot = acc[:LANES, :tq] * inv_l[:, :tq] - lam * (acc[:LANES, tq:] * inv_l[:, tq:])
        o = _rms(ot.T, sg_ref[...]) * (1.0 - lam_init)
        cols = slice(st * LANES, (st + 1) * LANES)
        o_ref[:, cols] = (o * _silu(g_ref[:, cols])).astype(BF16)


def _attn(qkv, vt, rest, lamv, sg, layer, B, L, tq, tk):
    T = B * L
    nq = L // tq
    assert tq == tk, "the causal block bookkeeping assumes square score blocks"
    lam_init = 0.8 - 0.6 * math.exp(-0.3 * layer)
    pair = 2 * LANES
    n_pair = GROUP_WIDTH // pair
    gate_blk = COL_BG * n_pair
    return pl.pallas_call(
        functools.partial(_attn_kernel, tq=tq, tk=tk, lam_init=lam_init),
        grid=(B, n_pair, nq),
        in_specs=[
            pl.BlockSpec((None, 4, DA_HEAD_DIM), lambda b, h, i: (layer, 0, 0)),
            pl.BlockSpec((tq, pair), lambda b, h, i: (b * nq + i, h)),
            pl.BlockSpec((L, pair), lambda b, h, i: (b, n_pair + h)),
            pl.BlockSpec((None, L // tk, pair, tk), lambda b, h, i: (b, 0, h, 0)),
            pl.BlockSpec((tq, pair), lambda b, h, i: (b * nq + i, gate_blk + h)),
            pl.BlockSpec((None, 1, LANES), lambda b, h, i: (layer, 0, 0)),
        ],
        out_specs=pl.BlockSpec((tq, pair), lambda b, h, i: (b * nq + i, h)),
        out_shape=jax.ShapeDtypeStruct((T, GROUP_WIDTH), BF16),
        scratch_shapes=[
            pltpu.VMEM((2, 2 * tq, LANES), BF16),
            pltpu.VMEM((2, tk, 2 * tq), F32),
            pltpu.VMEM((2, SUBLANES, 2 * tq), F32),
            pltpu.VMEM((2, SUBLANES, 2 * tq), F32),
            pltpu.VMEM((2, LANES + 2 * SUBLANES, 2 * tq), F32),
        ],
        compiler_params=_params("parallel", "parallel", "arbitrary"),
        name="diffattn",
    )(lamv, qkv, qkv, vt, rest, sg)


def _memkv_kernel(mem_ref, g_ref, w_ref, o_ref):
    o_ref[...] = _dot(_rms(mem_ref[...], g_ref[...]).astype(BF16), w_ref[...]).astype(BF16)


def _memkv(mem2d, g, w, B, M):
    depth, D, n_kv = w.shape
    return pl.pallas_call(
        _memkv_kernel,
        grid=(depth, B),
        in_specs=[
            pl.BlockSpec((M, D), lambda l, b: (b, 0)),
            pl.BlockSpec((None, 1, D), lambda l, b: (l, 0, 0)),
            pl.BlockSpec((None, D, n_kv), lambda l, b: (l, 0, 0)),
        ],
        out_specs=pl.BlockSpec((None, M, n_kv), lambda l, b: (l, b, 0)),
        out_shape=jax.ShapeDtypeStruct((depth, B * M, n_kv), BF16),
        compiler_params=_params("parallel", "parallel"),
        name="memkv",
    )(mem2d, g, w)


def _lru_gates(x, cw_ref, cb_ref, wg_ref, bg_ref, xe_ref, row0, n_rows):
    base = SUBLANES + row0
    xe_ref[base:base + n_rows, :] = x
    xc = (cw_ref[3:4, :] * x
          + cw_ref[2:3, :] * xe_ref[base - 1:base - 1 + n_rows, :]
          + cw_ref[1:2, :] * xe_ref[base - 2:base - 2 + n_rows, :]
          + cw_ref[0:1, :] * xe_ref[base - 3:base - 3 + n_rows, :]
          + cb_ref[...])
    return xc, _dot(xc.astype(BF16), wg_ref[...]) + bg_ref[...]


def _lru_scan(xc, z, gate, lam_ref, a_ref, b_ref, h_ref, carry_ref, chunk):
    r = _sigmoid(z[:, :GROUP_WIDTH])
    gate_i = _sigmoid(z[:, GROUP_WIDTH:])
    nl = -lam_ref[...]
    softplus = jnp.maximum(nl, 0.0) + jnp.log1p(jnp.exp(-jnp.abs(nl)))
    log_a = (-LRU_C) * r * softplus
    a = jnp.exp(log_a)
    a_ref[...] = a
    mult = jnp.sqrt(-jnp.tanh(log_a) * (1.0 + a * a))
    b_ref[...] = mult * (gate_i * xc)

    row = lax.broadcasted_iota(jnp.int32, (SUBLANES, LANES), 0)
    for t in range(chunk // SUBLANES):
        rows = slice(t * SUBLANES, (t + 1) * SUBLANES)
        for j in range(GROUP_WIDTH // LANES):
            cols = slice(j * LANES, (j + 1) * LANES)
            a = a_ref[rows, cols]
            b = b_ref[rows, cols] + jnp.where(row == 0, a, 0.0) * carry_ref[:, cols]
            w = jnp.where(row == 0, 0.0, a)
            for shift in (1, 2, 4):
                b = b + w * pltpu.roll(b, shift, axis=0)
                if shift != 4:
                    w = w * pltpu.roll(w, shift, axis=0)
            h_ref[rows, cols] = b
            carry_ref[:, cols] = jnp.broadcast_to(b[SUBLANES - 1:, :], (SUBLANES, LANES))
    return h_ref[...] * _silu(gate)


def _memattn_body(q, gate, kv_ref):
    qb = q.astype(BF16)
    scale = MEM_HEAD_DIM ** -0.5 * LOG2E
    outs = []
    for h in range(MEM_HEADS):
        cols = slice(h * MEM_HEAD_DIM, (h + 1) * MEM_HEAD_DIM)
        k = kv_ref[:, h * MEM_HEAD_DIM:(h + 1) * MEM_HEAD_DIM]
        v = kv_ref[:, GROUP_WIDTH + h * MEM_HEAD_DIM:GROUP_WIDTH + (h + 1) * MEM_HEAD_DIM]
        s = _dot_nt(qb[:, cols], k) * scale
        e = jnp.exp2(s - jnp.max(s, axis=-1, keepdims=True))
        outs.append(_dot(e.astype(BF16), v) / jnp.sum(e, axis=-1, keepdims=True))
    return jnp.concatenate(outs, axis=1) * _silu(gate)


def _tail_kernel(cx_ref, cg_ref, mq_ref, mg_ref, ya_ref, yb_ref, x_ref, kv_ref,
                 cw_ref, cb_ref, wg_ref, bg_ref, lam_ref, w_ref, fg_ref, o_ref,
                 xe_ref, a_ref, b_ref, h_ref, carry_ref, *, chunk, n_sub, final):
    c = pl.program_id(1)

    @pl.when(c == 0)
    def _():
        xe_ref[0:SUBLANES, :] = jnp.zeros((SUBLANES, GROUP_WIDTH), F32)
        carry_ref[...] = jnp.zeros(carry_ref.shape, F32)

    @pl.when(c > 0)
    def _():
        xe_ref[0:SUBLANES, :] = xe_ref[chunk:chunk + SUBLANES, :]

    gw = GROUP_WIDTH
    sub = chunk // n_sub

    def front(h):
        rows = pl.ds(h * sub, sub)
        xc, z = _lru_gates(cx_ref[rows, :], cw_ref, cb_ref, wg_ref, bg_ref, xe_ref, h * sub, sub)
        acc = (x_ref[rows, :] + _dot(ya_ref[rows, :], w_ref[0:gw, :])
               + _dot(yb_ref[rows, :], w_ref[gw:2 * gw, :]))
        y_m = _memattn_body(mq_ref[rows, :], mg_ref[rows, :], kv_ref)
        return xc, z, acc + _dot(y_m.astype(BF16), w_ref[3 * gw:4 * gw, :])

    fronts = [front(h) for h in range(n_sub)]
    for h, (xc, z, acc) in enumerate(fronts):
        rows = pl.ds(h * sub, sub)
        y_c = _lru_scan(xc, z, cg_ref[rows, :], lam_ref, a_ref.at[rows], b_ref.at[rows],
                        h_ref.at[rows], carry_ref, sub)
        acc = acc + _dot(y_c.astype(BF16), w_ref[2 * gw:3 * gw, :])
        o_ref[rows, :] = _rms(acc, fg_ref[...]) if final else acc


def _tail(rest, y_a, y_b, x, kv, cw, cb, wg, bg, lam, w_all, fg, layer, final, B, L, M, chunk):
    T, D = x.shape
    nc = L // chunk
    gw = GROUP_WIDTH

    def row_blk(b, c):
        return (b * nc + c, 0)

    def rest_spec(col):
        return pl.BlockSpec((chunk, gw), lambda b, c: (b * nc + c, col))

    def per_layer(shape):
        return pl.BlockSpec((None,) + shape, lambda b, c: (layer,) + (0,) * len(shape))

    return pl.pallas_call(
        functools.partial(_tail_kernel, chunk=chunk, n_sub=TAIL_ROW_GROUPS, final=final),
        grid=(B, nc),
        in_specs=[
            rest_spec(COL_CX), rest_spec(COL_CG), rest_spec(COL_MQ), rest_spec(COL_MG),
            pl.BlockSpec((chunk, gw), row_blk),
            pl.BlockSpec((chunk, gw), row_blk),
            pl.BlockSpec((chunk, D), row_blk),
            pl.BlockSpec((None, M, 2 * gw), lambda b, c: (layer, b, 0)),
            per_layer((4, gw)), per_layer((1, gw)), per_layer((gw, 2 * gw)), per_layer((1, 2 * gw)),
            per_layer((1, gw)), per_layer(w_all.shape[1:]),
            pl.BlockSpec((1, D), lambda b, c: (0, 0)),
        ],
        out_specs=pl.BlockSpec((chunk, D), row_blk),
        out_shape=jax.ShapeDtypeStruct((T, D), F32),
        scratch_shapes=[
            pltpu.VMEM((chunk + SUBLANES, gw), F32),
            pltpu.VMEM((chunk, gw), F32),
            pltpu.VMEM((chunk, gw), F32),
            pltpu.VMEM((chunk, gw), F32),
            pltpu.VMEM((SUBLANES, gw), F32),
        ],
        compiler_params=_params("arbitrary", "arbitrary"),
        name="tail",
    )(rest, rest, rest, rest, y_a, y_b, x, kv, cw, cb, wg, bg, lam, w_all, fg)


def _block_diag(blocks):
    n = blocks.shape[-3]
    eye = jnp.eye(n, dtype=blocks.dtype)
    out = jnp.einsum('...nrc,nm->...nrmc', blocks, eye)
    return out.reshape(blocks.shape[:-3] + (n * blocks.shape[-2], n * blocks.shape[-1]))


def _s5_prep(lam_re, lam_im, log_dt, b_re, b_im, c_re, c_im):
    depth = lam_re.shape[0]
    dt = jnp.exp(log_dt)[..., None]
    mag = jnp.exp(lam_re * dt)
    abar_re = mag * jnp.cos(lam_im * dt)
    abar_im = mag * jnp.sin(lam_im * dt)
    den = lam_re * lam_re + lam_im * lam_im
    nr, ni = abar_re - 1.0, abar_im
    f_re = (nr * lam_re + ni * lam_im) / den
    f_im = (ni * lam_re - nr * lam_im) / den
    bb_re = f_re[..., None] * b_re - f_im[..., None] * b_im
    bb_im = f_re[..., None] * b_im + f_im[..., None] * b_re
    gpb = LANES // S5_CH

    def in_blocks(bb):
        return _block_diag(jnp.swapaxes(bb, -1, -2).reshape(depth, S5_BLOCKS, gpb, S5_CH, S5_STATE))

    def out_blocks(cc):
        return _block_diag(jnp.swapaxes(cc, -1, -2).reshape(depth, S5_BLOCKS, gpb, S5_STATE, S5_CH))

    pr, pi = jnp.ones_like(abar_re), jnp.zeros_like(abar_im)
    wb_d, powers = [], []
    for _ in range(S5_UNROLL):
        wd_re = pr[..., None] * bb_re - pi[..., None] * bb_im
        wd_im = pr[..., None] * bb_im + pi[..., None] * bb_re
        wb_d.append(jnp.concatenate([in_blocks(wd_re), in_blocks(wd_im)], axis=-1))
        pr, pi = pr * abar_re - pi * abar_im, pr * abar_im + pi * abar_re
        powers.append((pr.reshape(depth, 1, -1), pi.reshape(depth, 1, -1)))
    wb = jnp.concatenate(wb_d, axis=-2).astype(BF16)
    wc = jnp.concatenate([out_blocks(c_re), out_blocks(-c_im)], axis=-2).astype(BF16)

    zero = jnp.zeros_like(powers[0][0])
    pad = [zero] * (SUBLANES - S5_UNROLL)
    carry_re = jnp.concatenate([p[0] for p in powers] + pad, axis=1)
    carry_im = jnp.concatenate([p[1] for p in powers] + pad, axis=1)
    step_re = jnp.concatenate([zero] * S5_UNROLL + [powers[-1][0]] * (SUBLANES - S5_UNROLL), axis=1)
    step_im = jnp.concatenate([zero] * S5_UNROLL + [powers[-1][1]] * (SUBLANES - S5_UNROLL), axis=1)
    coef = jnp.stack([carry_re, carry_im, step_re, step_im], axis=1)
    return wb, wc, coef.astype(F32)


def _rope_tables(positions):
    inv_freq = ROPE_THETA ** (-jnp.arange(0, ROPE_DIMS, 2, dtype=F32) / ROPE_DIMS)
    ang = positions.astype(F32).reshape(-1, 1) * inv_freq
    cos, sin = jnp.cos(ang), jnp.sin(ang)
    half = ROPE_DIMS // 2
    pad = DA_HEAD_DIM - ROPE_DIMS
    n = cos.shape[0]
    ones, zeros = jnp.ones((n, pad), F32), jnp.zeros((n, pad), F32)
    zh = jnp.zeros((n, half), F32)
    rc = jnp.concatenate([cos, cos, ones], axis=1)
    rs1 = jnp.concatenate([-sin, zh, zeros], axis=1)
    rs2 = jnp.concatenate([zh, sin, zeros], axis=1)
    rep = LANES // DA_HEAD_DIM
    return jnp.tile(rc, (1, rep)), jnp.tile(rs1, (1, rep)), jnp.tile(rs2, (1, rep))


def _tiles(L):
    return dict(chunk=min(256, L), tq=min(512, L), tk=min(512, L))


def kernel(x, mem, positions, norm_g, w_in, w_out, s5_lambda_re, s5_lambda_im, s5_log_dt, s5_b_re, s5_b_im, s5_c_re, s5_c_im, s5_d, s5_w_glu, da_lambda_q1, da_lambda_k1, da_lambda_q2, da_lambda_k2, da_subln_g, lru_conv_w, lru_conv_b, lru_w_a, lru_b_a, lru_w_x, lru_b_x, lru_lambda, mem_norm_g, w_mem_kv, final_norm_g):
    B, L, D = x.shape
    M = mem.shape[1]
    depth = w_in.shape[0]
    T = B * L
    t = _tiles(L)
    assert L % t["tk"] == 0 and L % t["chunk"] == 0

    gw = GROUP_WIDTH
    w_in_b = w_in.astype(BF16)
    w_out_b = w_out.astype(BF16)
    wglu_b = s5_w_glu.astype(BF16)
    wb, wc, coef = _s5_prep(s5_lambda_re, s5_lambda_im, s5_log_dt, s5_b_re, s5_b_im, s5_c_re, s5_c_im)
    s5_d2 = s5_d.reshape(depth, 1, gw)
    lamv = jnp.stack([da_lambda_q1, da_lambda_k1, da_lambda_q2, da_lambda_k2], axis=1)
    sg = da_subln_g.reshape(depth, 1, LANES)
    wg = jnp.concatenate([_block_diag(lru_w_a), _block_diag(lru_w_x)], axis=-1).astype(BF16)
    bg = jnp.concatenate([lru_b_a, lru_b_x], axis=-1).reshape(depth, 1, 2 * gw)
    lru_cb = lru_conv_b.reshape(depth, 1, gw)
    lru_lam = lru_lambda.reshape(depth, 1, gw)
    norm_g3 = norm_g.reshape(depth, 1, D)
    rc, rs1, rs2 = _rope_tables(positions)

    kv = _memkv(mem.reshape(B * M, D), mem_norm_g.reshape(depth, 1, D), w_mem_kv.astype(BF16), B, M)

    xf = x.reshape(T, D)
    fg = final_norm_g.reshape(1, D)
    for layer in range(depth):
        rest, qkv, y_a, vt = _inproj_s5(xf, norm_g3, w_in_b, layer, rc, rs1, rs2,
                                        wb, coef, wc, s5_d2, wglu_b, B, L, t["chunk"], t["tk"])
        y_b = _attn(qkv, vt, rest, lamv, sg, layer, B, L, t["tq"], t["tk"])
        xf = _tail(rest, y_a, y_b, xf, kv, lru_conv_w, lru_cb, wg, bg, lru_lam, w_out_b, fg,
                   layer, layer == depth - 1, B, L, M, t["chunk"])
    return xf.reshape(B, L, D)
```

```python
import functools
import math

import jax
import jax.numpy as jnp
from jax import lax
from jax.experimental import pallas as pl
from jax.experimental.pallas import tpu as pltpu

F32 = jnp.float32
BF16 = jnp.bfloat16

EPS = 1e-6
GROUP_WIDTH = 512
LANES = 128
SUBLANES = 8
S5_STATE = 64
S5_CH = 16
S5_BLOCKS = 4
S5_BLOCK_STATES = 512
S5_UNROLL = 4
S5_ROW_GROUPS = 2
TAIL_ROW_GROUPS = 1
ATTN_UNROLL = 2
ATTN_STREAMS = 4
DA_HEADS = 4
DA_HEAD_DIM = 64
ROPE_DIMS = 16
ROPE_THETA = 500000.0
LRU_BLOCKS = 8
LRU_C = 8.0
MEM_HEADS = 4
MEM_HEAD_DIM = 128
NEG_BIG = -1e30
LOG2E = math.log2(math.e)
Q_SCALE = DA_HEAD_DIM ** -0.5 * LOG2E
VMEM_LIMIT_BYTES = 56 * 1024 * 1024

W_COL_AU = 0
W_COL_Q = 2 * GROUP_WIDTH
W_COL_REST = 5 * GROUP_WIDTH
COL_BG, COL_CX, COL_CG, COL_MQ, COL_MG = range(5)
N_REST = 5 * GROUP_WIDTH


def _params(*sem):
    return pltpu.CompilerParams(dimension_semantics=sem, vmem_limit_bytes=VMEM_LIMIT_BYTES)


def _sigmoid(x):
    return 0.5 * jnp.tanh(0.5 * x) + 0.5


def _silu(x):
    return x * _sigmoid(x)


def _gelu_tanh(x):
    return 0.5 * x * (1.0 + jnp.tanh(math.sqrt(2.0 / math.pi) * (x + 0.044715 * (x * x * x))))


def _dot(a, b):
    return jnp.dot(a, b, preferred_element_type=F32)


def _dot_nt(a, b):
    return lax.dot_general(a, b, (((1,), (1,)), ((), ())), preferred_element_type=F32)


def _rms(x, g):
    ms = jnp.mean(x * x, axis=-1, keepdims=True)
    return x * lax.rsqrt(ms + EPS) * g


def _s5_input(u, wb_ref, us_ref, xs_ref, chunk):
    us_ref[0:SUBLANES, :] = jnp.zeros((SUBLANES, GROUP_WIDTH), F32)
    us_ref[SUBLANES:SUBLANES + chunk, :] = u
    row_in_tile = lax.broadcasted_iota(jnp.int32, (chunk, LANES), 0) % SUBLANES
    blk = 2 * S5_BLOCK_STATES
    for q in range(S5_BLOCKS):
        cols = slice(q * LANES, (q + 1) * LANES)
        parts = [u[:, cols].astype(BF16)]
        for d in range(1, S5_UNROLL):
            shifted = us_ref[SUBLANES - d:SUBLANES - d + chunk, cols]
            parts.append(jnp.where(row_in_tile >= d, shifted, 0.0).astype(BF16))
        xs_ref[:, q * blk:(q + 1) * blk] = _dot(jnp.concatenate(parts, axis=1), wb_ref[q])


def _s5_scan_out(u, gate, coef_ref, wc_ref, d_ref, wglu_ref, xs_ref, carry_ref, chunk):
    blk = 2 * S5_BLOCK_STATES

    def cmul_add(br, bi, ar, ai, sr, si):
        return br + (ar * sr - ai * si), bi + (ar * si + ai * sr)

    for r in range(chunk // SUBLANES):
        rows = slice(r * SUBLANES, (r + 1) * SUBLANES)
        for q in range(S5_BLOCKS):
            for j in range(S5_BLOCK_STATES // LANES):
                lr = q * blk + j * LANES
                li = lr + S5_BLOCK_STATES
                lc = q * S5_BLOCK_STATES + j * LANES
                xr, xi = cmul_add(xs_ref[rows, lr:lr + LANES], xs_ref[rows, li:li + LANES],
                                  coef_ref[0, :, lc:lc + LANES], coef_ref[1, :, lc:lc + LANES],
                                  carry_ref[:, lr:lr + LANES], carry_ref[:, li:li + LANES])
                xr, xi = cmul_add(xr, xi, coef_ref[2, :, lc:lc + LANES], coef_ref[3, :, lc:lc + LANES],
                                  pltpu.roll(xr, S5_UNROLL, axis=0), pltpu.roll(xi, S5_UNROLL, axis=0))
                xs_ref[rows, lr:lr + LANES] = xr
                xs_ref[rows, li:li + LANES] = xi
                carry_ref[:, lr:lr + LANES] = jnp.broadcast_to(xr[SUBLANES - 1:, :], (SUBLANES, LANES))
                carry_ref[:, li:li + LANES] = jnp.broadcast_to(xi[SUBLANES - 1:, :], (SUBLANES, LANES))

    ys = []
    for q in range(S5_BLOCKS):
        xb = xs_ref[:, q * blk:(q + 1) * blk].astype(BF16)
        ys.append(_dot(xb, wc_ref[q]))
    y = jnp.concatenate(ys, axis=1) + d_ref[...] * u
    y = _gelu_tanh(y)
    z = _dot(y.astype(BF16), wglu_ref[...])
    return z[:, :GROUP_WIDTH] * _sigmoid(z[:, GROUP_WIDTH:]) * _silu(gate)


def _inproj_s5_kernel(x_ref, g_ref, w_ref, rc_ref, rs1_ref, rs2_ref,
                      wb_ref, coef_ref, wc_ref, d_ref, wglu_ref,
                      rest_ref, qkv_ref, ya_ref, vt_ref, us_ref, xs_ref, carry_ref, *, chunk, n_sub):
    @pl.when(pl.program_id(1) == 0)
    def _():
        carry_ref[...] = jnp.zeros(carry_ref.shape, F32)

    gw = GROUP_WIDTH
    sub = chunk // n_sub

    def projections(h):
        rows = pl.ds(h * sub, sub)
        hb = _rms(x_ref[rows, :], g_ref[...]).astype(BF16)
        u = _dot(hb, w_ref[:, W_COL_AU:W_COL_AU + gw])
        _s5_input(u, wb_ref, us_ref.at[h], xs_ref.at[rows], sub)
        gate = _dot(hb, w_ref[:, W_COL_AU + gw:W_COL_AU + 2 * gw])
        rest_ref[rows, :] = _dot(hb, w_ref[:, W_COL_REST:W_COL_REST + N_REST])
        qk = _dot(hb, w_ref[:, W_COL_Q:W_COL_Q + 2 * gw])
        c, s1, s2 = rc_ref[rows, :], rs1_ref[rows, :], rs2_ref[rows, :]
        for j in range(2 * gw // LANES):
            t = qk[:, j * LANES:(j + 1) * LANES]
            t = (t * c + pltpu.roll(t, LANES - ROPE_DIMS // 2, axis=1) * s1
                 + pltpu.roll(t, ROPE_DIMS // 2, axis=1) * s2)
            if j < gw // LANES:
                t = t * Q_SCALE
            qkv_ref[rows, j * LANES:(j + 1) * LANES] = t.astype(BF16)
        v = _dot(hb, w_ref[:, W_COL_Q + 2 * gw:W_COL_Q + 3 * gw])
        qkv_ref[rows, 2 * gw:] = v.astype(BF16)
        vt_ref[:, h * sub:(h + 1) * sub] = v.T.astype(BF16)
        return u, gate

    fronts = [projections(h) for h in range(n_sub)]
    for h, (u, gate) in enumerate(fronts):
        rows = pl.ds(h * sub, sub)
        ya_ref[rows, :] = _s5_scan_out(u, gate, coef_ref, wc_ref, d_ref, wglu_ref,
                                       xs_ref.at[rows], carry_ref, sub).astype(BF16)


def _inproj_s5(x, g, w_all, layer, rc, rs1, rs2, wb, coef, wc, d, wglu, B, L, chunk, tk):
    T, D = x.shape
    n_in = w_all.shape[-1]
    nc = L // chunk
    per_kv = tk // chunk
    n_state = S5_BLOCKS * 2 * S5_BLOCK_STATES

    def row_blk(b, c):
        return (b * nc + c, 0)

    return pl.pallas_call(
        functools.partial(_inproj_s5_kernel, chunk=chunk, n_sub=S5_ROW_GROUPS),
        grid=(B, nc),
        in_specs=[
            pl.BlockSpec((chunk, D), row_blk),
            pl.BlockSpec((None, 1, D), lambda b, c: (layer, 0, 0)),
            pl.BlockSpec((None, D, n_in), lambda b, c: (layer, 0, 0), pipeline_mode=pl.Buffered(1)),
            pl.BlockSpec((chunk, LANES), row_blk),
            pl.BlockSpec((chunk, LANES), row_blk),
            pl.BlockSpec((chunk, LANES), row_blk),
            pl.BlockSpec((None,) + wb.shape[1:], lambda b, c: (layer, 0, 0, 0)),
            pl.BlockSpec((None,) + coef.shape[1:], lambda b, c: (layer, 0, 0, 0)),
            pl.BlockSpec((None,) + wc.shape[1:], lambda b, c: (layer, 0, 0, 0)),
            pl.BlockSpec((None, 1, GROUP_WIDTH), lambda b, c: (layer, 0, 0)),
            pl.BlockSpec((None,) + wglu.shape[1:], lambda b, c: (layer, 0, 0)),
        ],
        out_specs=[
            pl.BlockSpec((chunk, N_REST), row_blk),
            pl.BlockSpec((chunk, 3 * GROUP_WIDTH), row_blk),
            pl.BlockSpec((chunk, GROUP_WIDTH), row_blk),
            pl.BlockSpec((None, None, GROUP_WIDTH, chunk), lambda b, c: (b, c // per_kv, 0, c % per_kv)),
        ],
        out_shape=[jax.ShapeDtypeStruct((T, N_REST), F32),
                   jax.ShapeDtypeStruct((T, 3 * GROUP_WIDTH), BF16),
                   jax.ShapeDtypeStruct((T, GROUP_WIDTH), BF16),
                   jax.ShapeDtypeStruct((B, L // tk, GROUP_WIDTH, tk), BF16)],
        scratch_shapes=[pltpu.VMEM((S5_ROW_GROUPS, chunk // S5_ROW_GROUPS + SUBLANES, GROUP_WIDTH), F32),
                        pltpu.VMEM((chunk, n_state), F32),
                        pltpu.VMEM((SUBLANES, n_state), F32)],
        compiler_params=_params("arbitrary", "arbitrary"),
        name="inproj_s5",
    )(x, g, w_all, rc, rs1, rs2, wb, coef, wc, d, wglu)


def _attn_kernel(lamv_ref, q_ref, k_ref, vt_ref, g_ref, sg_ref, o_ref,
                 qs_ref, s_ref, smax_ref, m_ref, acc_ref, *, tq, tk, n_streams, lam_init):
    i = pl.program_id(2)
    lane = lax.broadcasted_iota(jnp.int32, (tq, LANES), 1)
    for st in range(n_streams):
        q = q_ref[:, st * LANES:(st + 1) * LANES]
        zero = jnp.zeros_like(q)
        qs_ref[st, 0:tq, :] = jnp.where(lane < DA_HEAD_DIM, q, zero)
        qs_ref[st, tq:, :] = jnp.where(lane >= DA_HEAD_DIM, q, zero)
    m_ref[...] = jnp.full(m_ref.shape, NEG_BIG, F32)
    acc_ref[...] = jnp.zeros(acc_ref.shape, F32)
    ones = jnp.ones((2 * SUBLANES, tk), BF16)

    def scores(st, kb, masked=False):
        k = k_ref[pl.ds(pl.multiple_of(kb * tk, tk), tk), st * LANES:(st + 1) * LANES]
        s = _dot_nt(k, qs_ref[st])
        if masked:
            key = lax.broadcasted_iota(jnp.int32, s.shape, 0)
            col = lax.broadcasted_iota(jnp.int32, s.shape, 1)
            qpos = i * tq + jnp.where(col >= tq, col - tq, col)
            s = jnp.where(kb * tk + key <= qpos, s, NEG_BIG)
        s_ref[st] = s
        smax_ref[st] = jnp.broadcast_to(jnp.max(s, axis=0, keepdims=True), smax_ref.shape[1:])

    def softmax_pv(st, kb):
        vt = jnp.concatenate([vt_ref[kb, st * LANES:(st + 1) * LANES, :], ones], axis=0)
        m_prev = m_ref[st]
        m_new = jnp.maximum(m_prev, smax_ref[st])
        alpha = jnp.exp2(m_prev - m_new)
        p = jnp.exp2(s_ref[st] - m_new[0:1, :])
        acc_ref[st] = alpha[0:1, :] * acc_ref[st] + _dot(vt, p.astype(BF16))
        m_ref[st] = m_new

    last = n_streams - 1
    scores(0, i, masked=True)
    for st in range(n_streams):
        if st < last:
            scores(st + 1, i, masked=True)
        else:
            scores(0, 0)
        softmax_pv(st, i)

    def step(kb):
        for st in range(n_streams):
            if st < last:
                scores(st + 1, kb)
            else:
                scores(0, kb + 1)
            softmax_pv(st, kb)

    def unrolled_steps(jj, carry):
        for n in range(ATTN_UNROLL):
            step(ATTN_UNROLL * jj + n)
        return carry

    def single_step(kb, carry):
        step(kb)
        return carry

    n_main = lax.div(i, ATTN_UNROLL)
    lax.fori_loop(0, n_main, unrolled_steps, 0)
    lax.fori_loop(n_main * ATTN_UNROLL, i, single_step, 0)

    lv = lamv_ref[...]
    lam = (jnp.exp(jnp.sum(lv[0:1, :] * lv[1:2, :], axis=-1, keepdims=True))
           - jnp.exp(jnp.sum(lv[2:3, :] * lv[3:4, :], axis=-1, keepdims=True)) + lam_init)
    for st in range(n_streams):
        acc = acc_ref[st]
        inv_l = 1.0 / acc[LANES:LANES + 1, :]
        ot = acc[:LANES, :tq] * inv_l[:, :tq] - lam * (acc[:LANES, tq:] * inv_l[:, tq:])
        o = _rms(ot.T, sg_ref[...]) * (1.0 - lam_init)
        cols = slice(st * LANES, (st + 1) * LANES)
        o_ref[:, cols] = (o * _silu(g_ref[:, cols])).astype(BF16)


def _attn(qkv, vt, rest, lamv, sg, layer, B, L, tq, tk):
    T = B * L
    nq = L // tq
    assert tq == tk, "the causal block bookkeeping assumes square score blocks"
    lam_init = 0.8 - 0.6 * math.exp(-0.3 * layer)
    ns = ATTN_STREAMS
    width = ns * LANES
    n_grp = GROUP_WIDTH // width
    gate_blk = COL_BG * n_grp
    resident = pl.Buffered(1)
    return pl.pallas_call(
        functools.partial(_attn_kernel, tq=tq, tk=tk, n_streams=ns, lam_init=lam_init),
        grid=(B, n_grp, nq),
        in_specs=[
            pl.BlockSpec((None, 4, DA_HEAD_DIM), lambda b, h, i: (layer, 0, 0)),
            pl.BlockSpec((tq, width), lambda b, h, i: (b * nq + i, h)),
            pl.BlockSpec((L, width), lambda b, h, i: (b, n_grp + h), pipeline_mode=resident),
            pl.BlockSpec((None, L // tk, width, tk), lambda b, h, i: (b, 0, h, 0), pipeline_mode=resident),
            pl.BlockSpec((tq, width), lambda b, h, i: (b * nq + i, gate_blk + h)),
            pl.BlockSpec((None, 1, LANES), lambda b, h, i: (layer, 0, 0)),
        ],
        out_specs=pl.BlockSpec((tq, width), lambda b, h, i: (b * nq + i, h)),
        out_shape=jax.ShapeDtypeStruct((T, GROUP_WIDTH), BF16),
        scratch_shapes=[
            pltpu.VMEM((ns, 2 * tq, LANES), BF16),
            pltpu.VMEM((ns, tk, 2 * tq), F32),
            pltpu.VMEM((ns, SUBLANES, 2 * tq), F32),
            pltpu.VMEM((ns, SUBLANES, 2 * tq), F32),
            pltpu.VMEM((ns, LANES + 2 * SUBLANES, 2 * tq), F32),
        ],
        compiler_params=_params("parallel", "parallel", "arbitrary"),
        name="diffattn",
    )(lamv, qkv, qkv, vt, rest, sg)


def _memkv_kernel(mem_ref, g_ref, w_ref, o_ref):
    o_ref[...] = _dot(_rms(mem_ref[...], g_ref[...]).astype(BF16), w_ref[...]).astype(BF16)


def _memkv(mem2d, g, w, B, M):
    depth, D, n_kv = w.shape
    return pl.pallas_call(
        _memkv_kernel,
        grid=(depth, B),
        in_specs=[
            pl.BlockSpec((M, D), lambda l, b: (b, 0)),
            pl.BlockSpec((None, 1, D), lambda l, b: (l, 0, 0)),
            pl.BlockSpec((None, D, n_kv), lambda l, b: (l, 0, 0)),
        ],
        out_specs=pl.BlockSpec((None, M, n_kv), lambda l, b: (l, b, 0)),
        out_shape=jax.ShapeDtypeStruct((depth, B * M, n_kv), BF16),
        compiler_params=_params("parallel", "parallel"),
        name="memkv",
    )(mem2d, g, w)


def _lru_gates(x, cw_ref, cb_ref, wg_ref, bg_ref, xe_ref, row0, n_rows):
    base = SUBLANES + row0
    xe_ref[base:base + n_rows, :] = x
    xc = (cw_ref[3:4, :] * x
          + cw_ref[2:3, :] * xe_ref[base - 1:base - 1 + n_rows, :]
          + cw_ref[1:2, :] * xe_ref[base - 2:base - 2 + n_rows, :]
          + cw_ref[0:1, :] * xe_ref[base - 3:base - 3 + n_rows, :]
          + cb_ref[...])
    return xc, _dot(xc.astype(BF16), wg_ref[...]) + bg_ref[...]


def _lru_scan(xc, z, gate, lam_ref, a_ref, b_ref, h_ref, carry_ref, chunk):
    r = _sigmoid(z[:, :GROUP_WIDTH])
    gate_i = _sigmoid(z[:, GROUP_WIDTH:])
    nl = -lam_ref[...]
    softplus = jnp.maximum(nl, 0.0) + jnp.log1p(jnp.exp(-jnp.abs(nl)))
    log_a = (-LRU_C) * r * softplus
    a = jnp.exp(log_a)
    a_ref[...] = a
    mult = jnp.sqrt(-jnp.tanh(log_a) * (1.0 + a * a))
    b_ref[...] = mult * (gate_i * xc)

    row = lax.broadcasted_iota(jnp.int32, (SUBLANES, LANES), 0)
    for t in range(chunk // SUBLANES):
        rows = slice(t * SUBLANES, (t + 1) * SUBLANES)
        for j in range(GROUP_WIDTH // LANES):
            cols = slice(j * LANES, (j + 1) * LANES)
            a = a_ref[rows, cols]
            b = b_ref[rows, cols] + jnp.where(row == 0, a, 0.0) * carry_ref[:, cols]
            w = jnp.where(row == 0, 0.0, a)
            for shift in (1, 2, 4):
                b = b + w * pltpu.roll(b, shift, axis=0)
                if shift != 4:
                    w = w * pltpu.roll(w, shift, axis=0)
            h_ref[rows, cols] = b
            carry_ref[:, cols] = jnp.broadcast_to(b[SUBLANES - 1:, :], (SUBLANES, LANES))
    return h_ref[...] * _silu(gate)


def _memattn_body(q, gate, kv_ref):
    qb = q.astype(BF16)
    scale = MEM_HEAD_DIM ** -0.5 * LOG2E
    outs = []
    for h in range(MEM_HEADS):
        cols = slice(h * MEM_HEAD_DIM, (h + 1) * MEM_HEAD_DIM)
        k = kv_ref[:, h * MEM_HEAD_DIM:(h + 1) * MEM_HEAD_DIM]
        v = kv_ref[:, GROUP_WIDTH + h * MEM_HEAD_DIM:GROUP_WIDTH + (h + 1) * MEM_HEAD_DIM]
        s = _dot_nt(qb[:, cols], k) * scale
        e = jnp.exp2(s - jnp.max(s, axis=-1, keepdims=True))
        outs.append(_dot(e.astype(BF16), v) / jnp.sum(e, axis=-1, keepdims=True))
    return jnp.concatenate(outs, axis=1) * _silu(gate)


def _tail_kernel(cx_ref, cg_ref, mq_ref, mg_ref, ya_ref, yb_ref, x_ref, kv_ref,
                 cw_ref, cb_ref, wg_ref, bg_ref, lam_ref, w_ref, fg_ref, o_ref,
                 xe_ref, a_ref, b_ref, h_ref, carry_ref, *, chunk, n_sub, final):
    c = pl.program_id(1)

    @pl.when(c == 0)
    def _():
        xe_ref[0:SUBLANES, :] = jnp.zeros((SUBLANES, GROUP_WIDTH), F32)
        carry_ref[...] = jnp.zeros(carry_ref.shape, F32)

    @pl.when(c > 0)
    def _():
        xe_ref[0:SUBLANES, :] = xe_ref[chunk:chunk + SUBLANES, :]

    gw = GROUP_WIDTH
    sub = chunk // n_sub

    def front(h):
        rows = pl.ds(h * sub, sub)
        xc, z = _lru_gates(cx_ref[rows, :], cw_ref, cb_ref, wg_ref, bg_ref, xe_ref, h * sub, sub)
        acc = (x_ref[rows, :] + _dot(ya_ref[rows, :], w_ref[0:gw, :])
               + _dot(yb_ref[rows, :], w_ref[gw:2 * gw, :]))
        y_m = _memattn_body(mq_ref[rows, :], mg_ref[rows, :], kv_ref)
        return xc, z, acc + _dot(y_m.astype(BF16), w_ref[3 * gw:4 * gw, :])

    fronts = [front(h) for h in range(n_sub)]
    for h, (xc, z, acc) in enumerate(fronts):
        rows = pl.ds(h * sub, sub)
        y_c = _lru_scan(xc, z, cg_ref[rows, :], lam_ref, a_ref.at[rows], b_ref.at[rows],
                        h_ref.at[rows], carry_ref, sub)
        acc = acc + _dot(y_c.astype(BF16), w_ref[2 * gw:3 * gw, :])
        o_ref[rows, :] = _rms(acc, fg_ref[...]) if final else acc


def _tail(rest, y_a, y_b, x, kv, cw, cb, wg, bg, lam, w_all, fg, layer, final, B, L, M, chunk):
    T, D = x.shape
    nc = L // chunk
    gw = GROUP_WIDTH

    def row_blk(b, c):
        return (b * nc + c, 0)

    def rest_spec(col):
        return pl.BlockSpec((chunk, gw), lambda b, c: (b * nc + c, col))

    def per_layer(shape):
        return pl.BlockSpec((None,) + shape, lambda b, c: (layer,) + (0,) * len(shape))

    return pl.pallas_call(
        functools.partial(_tail_kernel, chunk=chunk, n_sub=TAIL_ROW_GROUPS, final=final),
        grid=(B, nc),
        in_specs=[
            rest_spec(COL_CX), rest_spec(COL_CG), rest_spec(COL_MQ), rest_spec(COL_MG),
            pl.BlockSpec((chunk, gw), row_blk),
            pl.BlockSpec((chunk, gw), row_blk),
            pl.BlockSpec((chunk, D), row_blk),
            pl.BlockSpec((None, M, 2 * gw), lambda b, c: (layer, b, 0)),
            per_layer((4, gw)), per_layer((1, gw)), per_layer((gw, 2 * gw)), per_layer((1, 2 * gw)),
            per_layer((1, gw)), per_layer(w_all.shape[1:]),
            pl.BlockSpec((1, D), lambda b, c: (0, 0)),
        ],
        out_specs=pl.BlockSpec((chunk, D), row_blk),
        out_shape=jax.ShapeDtypeStruct((T, D), F32),
        scratch_shapes=[
            pltpu.VMEM((chunk + SUBLANES, gw), F32),
            pltpu.VMEM((chunk, gw), F32),
            pltpu.VMEM((chunk, gw), F32),
            pltpu.VMEM((chunk, gw), F32),
            pltpu.VMEM((SUBLANES, gw), F32),
        ],
        compiler_params=_params("arbitrary", "arbitrary"),
        name="tail",
    )(rest, rest, rest, rest, y_a, y_b, x, kv, cw, cb, wg, bg, lam, w_all, fg)


def _block_diag(blocks):
    n = blocks.shape[-3]
    eye = jnp.eye(n, dtype=blocks.dtype)
    out = jnp.einsum('...nrc,nm->...nrmc', blocks, eye)
    return out.reshape(blocks.shape[:-3] + (n * blocks.shape[-2], n * blocks.shape[-1]))


def _s5_prep(lam_re, lam_im, log_dt, b_re, b_im, c_re, c_im):
    depth = lam_re.shape[0]
    dt = jnp.exp(log_dt)[..., None]
    mag = jnp.exp(lam_re * dt)
    abar_re = mag * jnp.cos(lam_im * dt)
    abar_im = mag * jnp.sin(lam_im * dt)
    den = lam_re * lam_re + lam_im * lam_im
    nr, ni = abar_re - 1.0, abar_im
    f_re = (nr * lam_re + ni * lam_im) / den
    f_im = (ni * lam_re - nr * lam_im) / den
    bb_re = f_re[..., None] * b_re - f_im[..., None] * b_im
    bb_im = f_re[..., None] * b_im + f_im[..., None] * b_re
    gpb = LANES // S5_CH

    def in_blocks(bb):
        return _block_diag(jnp.swapaxes(bb, -1, -2).reshape(depth, S5_BLOCKS, gpb, S5_CH, S5_STATE))

    def out_blocks(cc):
        return _block_diag(jnp.swapaxes(cc, -1, -2).reshape(depth, S5_BLOCKS, gpb, S5_STATE, S5_CH))

    pr, pi = jnp.ones_like(abar_re), jnp.zeros_like(abar_im)
    wb_d, powers = [], []
    for _ in range(S5_UNROLL):
        wd_re = pr[..., None] * bb_re - pi[..., None] * bb_im
        wd_im = pr[..., None] * bb_im + pi[..., None] * bb_re
        wb_d.append(jnp.concatenate([in_blocks(wd_re), in_blocks(wd_im)], axis=-1))
        pr, pi = pr * abar_re - pi * abar_im, pr * abar_im + pi * abar_re
        powers.append((pr.reshape(depth, 1, -1), pi.reshape(depth, 1, -1)))
    wb = jnp.concatenate(wb_d, axis=-2).astype(BF16)
    wc = jnp.concatenate([out_blocks(c_re), out_blocks(-c_im)], axis=-2).astype(BF16)

    zero = jnp.zeros_like(powers[0][0])
    pad = [zero] * (SUBLANES - S5_UNROLL)
    carry_re = jnp.concatenate([p[0] for p in powers] + pad, axis=1)
    carry_im = jnp.concatenate([p[1] for p in powers] + pad, axis=1)
    step_re = jnp.concatenate([zero] * S5_UNROLL + [powers[-1][0]] * (SUBLANES - S5_UNROLL), axis=1)
    step_im = jnp.concatenate([zero] * S5_UNROLL + [powers[-1][1]] * (SUBLANES - S5_UNROLL), axis=1)
    coef = jnp.stack([carry_re, carry_im, step_re, step_im], axis=1)
    return wb, wc, coef.astype(F32)


def _rope_tables(positions):
    inv_freq = ROPE_THETA ** (-jnp.arange(0, ROPE_DIMS, 2, dtype=F32) / ROPE_DIMS)
    ang = positions.astype(F32).reshape(-1, 1) * inv_freq
    cos, sin = jnp.cos(ang), jnp.sin(ang)
    half = ROPE_DIMS // 2
    pad = DA_HEAD_DIM - ROPE_DIMS
    n = cos.shape[0]
    ones, zeros = jnp.ones((n, pad), F32), jnp.zeros((n, pad), F32)
    zh = jnp.zeros((n, half), F32)
    rc = jnp.concatenate([cos, cos, ones], axis=1)
    rs1 = jnp.concatenate([-sin, zh, zeros], axis=1)
    rs2 = jnp.concatenate([zh, sin, zeros], axis=1)
    rep = LANES // DA_HEAD_DIM
    return jnp.tile(rc, (1, rep)), jnp.tile(rs1, (1, rep)), jnp.tile(rs2, (1, rep))


def _tiles(L):
    return dict(chunk=min(256, L), tq=min(512, L), tk=min(512, L))


def kernel(x, mem, positions, norm_g, w_in, w_out, s5_lambda_re, s5_lambda_im, s5_log_dt, s5_b_re, s5_b_im, s5_c_re, s5_c_im, s5_d, s5_w_glu, da_lambda_q1, da_lambda_k1, da_lambda_q2, da_lambda_k2, da_subln_g, lru_conv_w, lru_conv_b, lru_w_a, lru_b_a, lru_w_x, lru_b_x, lru_lambda, mem_norm_g, w_mem_kv, final_norm_g):
    B, L, D = x.shape
    M = mem.shape[1]
    depth = w_in.shape[0]
    T = B * L
    t = _tiles(L)
    assert L % t["tk"] == 0 and L % t["chunk"] == 0

    gw = GROUP_WIDTH
    w_in_b = w_in.astype(BF16)
    w_out_b = w_out.astype(BF16)
    wglu_b = s5_w_glu.astype(BF16)
    wb, wc, coef = _s5_prep(s5_lambda_re, s5_lambda_im, s5_log_dt, s5_b_re, s5_b_im, s5_c_re, s5_c_im)
    s5_d2 = s5_d.reshape(depth, 1, gw)
    lamv = jnp.stack([da_lambda_q1, da_lambda_k1, da_lambda_q2, da_lambda_k2], axis=1)
    sg = da_subln_g.reshape(depth, 1, LANES)
    wg = jnp.concatenate([_block_diag(lru_w_a), _block_diag(lru_w_x)], axis=-1).astype(BF16)
    bg = jnp.concatenate([lru_b_a, lru_b_x], axis=-1).reshape(depth, 1, 2 * gw)
    lru_cb = lru_conv_b.reshape(depth, 1, gw)
    lru_lam = lru_lambda.reshape(depth, 1, gw)
    norm_g3 = norm_g.reshape(depth, 1, D)
    rc, rs1, rs2 = _rope_tables(positions)

    kv = _memkv(mem.reshape(B * M, D), mem_norm_g.reshape(depth, 1, D), w_mem_kv.astype(BF16), B, M)

    xf = x.reshape(T, D)
    fg = final_norm_g.reshape(1, D)
    for layer in range(depth):
        rest, qkv, y_a, vt = _inproj_s5(xf, norm_g3, w_in_b, layer, rc, rs1, rs2,
                                        wb, coef, wc, s5_d2, wglu_b, B, L, t["chunk"], t["tk"])
        y_b = _attn(qkv, vt, rest, lamv, sg, layer, B, L, t["tq"], t["tk"])
        xf = _tail(rest, y_a, y_b, xf, kv, lru_conv_w, lru_cb, wg, bg, lru_lam, w_out_b, fg,
                   layer, layer == depth - 1, B, L, M, t["chunk"])
    return xf.reshape(B, L, D)
```

```python
import functools
import math

import jax
import jax.numpy as jnp
from jax import lax
from jax.experimental import pallas as pl
from jax.experimental.pallas import tpu as pltpu

F32 = jnp.float32
BF16 = jnp.bfloat16

EPS = 1e-6
GROUP_WIDTH = 512
LANES = 128
SUBLANES = 8
S5_STATE = 64
S5_CH = 16
S5_BLOCKS = 4
S5_BLOCK_STATES = 512
S5_UNROLL = 4
S5_ROW_GROUPS = 4
TAIL_ROW_GROUPS = 1
ATTN_UNROLL = 2
ATTN_STREAMS = 4
DA_HEADS = 4
DA_HEAD_DIM = 64
ROPE_DIMS = 16
ROPE_THETA = 500000.0
LRU_BLOCKS = 8
LRU_C = 8.0
MEM_HEADS = 4
MEM_HEAD_DIM = 128
NEG_BIG = -1e30
LOG2E = math.log2(math.e)
Q_SCALE = DA_HEAD_DIM ** -0.5 * LOG2E
VMEM_LIMIT_BYTES = 56 * 1024 * 1024

W_COL_AU = 0
W_COL_Q = 2 * GROUP_WIDTH
W_COL_REST = 5 * GROUP_WIDTH
COL_BG, COL_CX, COL_CG, COL_MQ, COL_MG = range(5)
N_REST = 5 * GROUP_WIDTH


def _params(*sem):
    return pltpu.CompilerParams(dimension_semantics=sem, vmem_limit_bytes=VMEM_LIMIT_BYTES)


def _sigmoid(x):
    return 0.5 * jnp.tanh(0.5 * x) + 0.5


def _silu(x):
    return x * _sigmoid(x)


def _gelu_tanh(x):
    return 0.5 * x * (1.0 + jnp.tanh(math.sqrt(2.0 / math.pi) * (x + 0.044715 * (x * x * x))))


def _dot(a, b):
    return jnp.dot(a, b, preferred_element_type=F32)


def _dot_nt(a, b):
    return lax.dot_general(a, b, (((1,), (1,)), ((), ())), preferred_element_type=F32)


def _rms(x, g):
    ms = jnp.mean(x * x, axis=-1, keepdims=True)
    return x * lax.rsqrt(ms + EPS) * g


def _s5_input(u, wb_ref, us_ref, xs_ref, chunk):
    us_ref[0:SUBLANES, :] = jnp.zeros((SUBLANES, GROUP_WIDTH), F32)
    us_ref[SUBLANES:SUBLANES + chunk, :] = u
    row_in_tile = lax.broadcasted_iota(jnp.int32, (chunk, LANES), 0) % SUBLANES
    blk = 2 * S5_BLOCK_STATES
    for q in range(S5_BLOCKS):
        cols = slice(q * LANES, (q + 1) * LANES)
        parts = [u[:, cols].astype(BF16)]
        for d in range(1, S5_UNROLL):
            shifted = us_ref[SUBLANES - d:SUBLANES - d + chunk, cols]
            parts.append(jnp.where(row_in_tile >= d, shifted, 0.0).astype(BF16))
        xs_ref[:, q * blk:(q + 1) * blk] = _dot(jnp.concatenate(parts, axis=1), wb_ref[q])


def _s5_scan_out(u, gate, coef_ref, wc_ref, d_ref, wglu_ref, xs_ref, carry_ref, chunk):
    blk = 2 * S5_BLOCK_STATES

    def cmul_add(br, bi, ar, ai, sr, si):
        return br + (ar * sr - ai * si), bi + (ar * si + ai * sr)

    for r in range(chunk // SUBLANES):
        rows = slice(r * SUBLANES, (r + 1) * SUBLANES)
        for q in range(S5_BLOCKS):
            for j in range(S5_BLOCK_STATES // LANES):
                lr = q * blk + j * LANES
                li = lr + S5_BLOCK_STATES
                lc = q * S5_BLOCK_STATES + j * LANES
                xr, xi = cmul_add(xs_ref[rows, lr:lr + LANES], xs_ref[rows, li:li + LANES],
                                  coef_ref[0, :, lc:lc + LANES], coef_ref[1, :, lc:lc + LANES],
                                  carry_ref[:, lr:lr + LANES], carry_ref[:, li:li + LANES])
                xr, xi = cmul_add(xr, xi, coef_ref[2, :, lc:lc + LANES], coef_ref[3, :, lc:lc + LANES],
                                  pltpu.roll(xr, S5_UNROLL, axis=0), pltpu.roll(xi, S5_UNROLL, axis=0))
                xs_ref[rows, lr:lr + LANES] = xr
                xs_ref[rows, li:li + LANES] = xi
                carry_ref[:, lr:lr + LANES] = jnp.broadcast_to(xr[SUBLANES - 1:, :], (SUBLANES, LANES))
                carry_ref[:, li:li + LANES] = jnp.broadcast_to(xi[SUBLANES - 1:, :], (SUBLANES, LANES))

    ys = []
    for q in range(S5_BLOCKS):
        xb = xs_ref[:, q * blk:(q + 1) * blk].astype(BF16)
        ys.append(_dot(xb, wc_ref[q]))
    y = jnp.concatenate(ys, axis=1) + d_ref[...] * u
    y = _gelu_tanh(y)
    z = _dot(y.astype(BF16), wglu_ref[...])
    return z[:, :GROUP_WIDTH] * _sigmoid(z[:, GROUP_WIDTH:]) * _silu(gate)


def _inproj_s5_kernel(x_ref, g_ref, w_ref, rc_ref, rs1_ref, rs2_ref,
                      wb_ref, coef_ref, wc_ref, d_ref, wglu_ref,
                      rest_ref, qkv_ref, ya_ref, vt_ref, us_ref, xs_ref, carry_ref, *, chunk, n_sub):
    @pl.when(pl.program_id(1) == 0)
    def _():
        carry_ref[...] = jnp.zeros(carry_ref.shape, F32)

    gw = GROUP_WIDTH
    sub = chunk // n_sub

    def projections(h):
        rows = pl.ds(h * sub, sub)
        hb = _rms(x_ref[rows, :], g_ref[...]).astype(BF16)
        u = _dot(hb, w_ref[:, W_COL_AU:W_COL_AU + gw])
        _s5_input(u, wb_ref, us_ref.at[h], xs_ref.at[rows], sub)
        gate = _dot(hb, w_ref[:, W_COL_AU + gw:W_COL_AU + 2 * gw])
        rest_ref[rows, :] = _dot(hb, w_ref[:, W_COL_REST:W_COL_REST + N_REST])
        qk = _dot(hb, w_ref[:, W_COL_Q:W_COL_Q + 2 * gw])
        c, s1, s2 = rc_ref[rows, :], rs1_ref[rows, :], rs2_ref[rows, :]
        for j in range(2 * gw // LANES):
            t = qk[:, j * LANES:(j + 1) * LANES]
            t = (t * c + pltpu.roll(t, LANES - ROPE_DIMS // 2, axis=1) * s1
                 + pltpu.roll(t, ROPE_DIMS // 2, axis=1) * s2)
            if j < gw // LANES:
                t = t * Q_SCALE
            qkv_ref[rows, j * LANES:(j + 1) * LANES] = t.astype(BF16)
        v = _dot(hb, w_ref[:, W_COL_Q + 2 * gw:W_COL_Q + 3 * gw])
        qkv_ref[rows, 2 * gw:] = v.astype(BF16)
        vt_ref[:, h * sub:(h + 1) * sub] = v.T.astype(BF16)
        return u, gate

    fronts = [projections(h) for h in range(n_sub)]
    for h, (u, gate) in enumerate(fronts):
        rows = pl.ds(h * sub, sub)
        ya_ref[rows, :] = _s5_scan_out(u, gate, coef_ref, wc_ref, d_ref, wglu_ref,
                                       xs_ref.at[rows], carry_ref, sub).astype(BF16)


def _inproj_s5(x, g, w_all, layer, rc, rs1, rs2, wb, coef, wc, d, wglu, B, L, chunk, tk):
    T, D = x.shape
    n_in = w_all.shape[-1]
    nc = L // chunk
    per_kv = tk // chunk
    n_state = S5_BLOCKS * 2 * S5_BLOCK_STATES

    def row_blk(b, c):
        return (b * nc + c, 0)

    return pl.pallas_call(
        functools.partial(_inproj_s5_kernel, chunk=chunk, n_sub=S5_ROW_GROUPS),
        grid=(B, nc),
        in_specs=[
            pl.BlockSpec((chunk, D), row_blk),
            pl.BlockSpec((None, 1, D), lambda b, c: (layer, 0, 0)),
            pl.BlockSpec((None, D, n_in), lambda b, c: (layer, 0, 0), pipeline_mode=pl.Buffered(1)),
            pl.BlockSpec((chunk, LANES), row_blk),
            pl.BlockSpec((chunk, LANES), row_blk),
            pl.BlockSpec((chunk, LANES), row_blk),
            pl.BlockSpec((None,) + wb.shape[1:], lambda b, c: (layer, 0, 0, 0)),
            pl.BlockSpec((None,) + coef.shape[1:], lambda b, c: (layer, 0, 0, 0)),
            pl.BlockSpec((None,) + wc.shape[1:], lambda b, c: (layer, 0, 0, 0)),
            pl.BlockSpec((None, 1, GROUP_WIDTH), lambda b, c: (layer, 0, 0)),
            pl.BlockSpec((None,) + wglu.shape[1:], lambda b, c: (layer, 0, 0)),
        ],
        out_specs=[
            pl.BlockSpec((chunk, N_REST), row_blk),
            pl.BlockSpec((chunk, 3 * GROUP_WIDTH), row_blk),
            pl.BlockSpec((chunk, GROUP_WIDTH), row_blk),
            pl.BlockSpec((None, None, GROUP_WIDTH, chunk), lambda b, c: (b, c // per_kv, 0, c % per_kv)),
        ],
        out_shape=[jax.ShapeDtypeStruct((T, N_REST), F32),
                   jax.ShapeDtypeStruct((T, 3 * GROUP_WIDTH), BF16),
                   jax.ShapeDtypeStruct((T, GROUP_WIDTH), BF16),
                   jax.ShapeDtypeStruct((B, L // tk, GROUP_WIDTH, tk), BF16)],
        scratch_shapes=[pltpu.VMEM((S5_ROW_GROUPS, chunk // S5_ROW_GROUPS + SUBLANES, GROUP_WIDTH), F32),
                        pltpu.VMEM((chunk, n_state), F32),
                        pltpu.VMEM((SUBLANES, n_state), F32)],
        compiler_params=_params("arbitrary", "arbitrary"),
        name="inproj_s5",
    )(x, g, w_all, rc, rs1, rs2, wb, coef, wc, d, wglu)


def _attn_kernel(lamv_ref, q_ref, k_ref, vt_ref, g_ref, sg_ref, o_ref,
                 qs_ref, s_ref, smax_ref, m_ref, acc_ref, *, tq, tk, n_streams, lam_init):
    i = pl.program_id(2)
    lane = lax.broadcasted_iota(jnp.int32, (tq, LANES), 1)
    for st in range(n_streams):
        q = q_ref[:, st * LANES:(st + 1) * LANES]
        zero = jnp.zeros_like(q)
        qs_ref[st, 0:tq, :] = jnp.where(lane < DA_HEAD_DIM, q, zero)
        qs_ref[st, tq:, :] = jnp.where(lane >= DA_HEAD_DIM, q, zero)
    m_ref[...] = jnp.full(m_ref.shape, NEG_BIG, F32)
    acc_ref[...] = jnp.zeros(acc_ref.shape, F32)
    ones = jnp.ones((2 * SUBLANES, tk), BF16)

    def scores(st, kb, masked=False):
        k = k_ref[pl.ds(pl.multiple_of(kb * tk, tk), tk), st * LANES:(st + 1) * LANES]
        s = _dot_nt(k, qs_ref[st])
        if masked:
            key = lax.broadcasted_iota(jnp.int32, s.shape, 0)
            col = lax.broadcasted_iota(jnp.int32, s.shape, 1)
            qpos = i * tq + jnp.where(col >= tq, col - tq, col)
            s = jnp.where(kb * tk + key <= qpos, s, NEG_BIG)
        s_ref[st] = s
        smax_ref[st] = jnp.broadcast_to(jnp.max(s, axis=0, keepdims=True), smax_ref.shape[1:])

    def softmax_pv(st, kb):
        vt = jnp.concatenate([vt_ref[kb, st * LANES:(st + 1) * LANES, :], ones], axis=0)
        m_prev = m_ref[st]
        m_new = jnp.maximum(m_prev, smax_ref[st])
        alpha = jnp.exp2(m_prev - m_new)
        p = jnp.exp2(s_ref[st] - m_new[0:1, :])
        acc_ref[st] = alpha[0:1, :] * acc_ref[st] + _dot(vt, p.astype(BF16))
        m_ref[st] = m_new

    last = n_streams - 1
    scores(0, i, masked=True)
    for st in range(n_streams):
        if st < last:
            scores(st + 1, i, masked=True)
        else:
            scores(0, 0)
        softmax_pv(st, i)

    def step(kb):
        for st in range(n_streams):
            if st < last:
                scores(st + 1, kb)
            else:
                scores(0, kb + 1)
            softmax_pv(st, kb)

    def unrolled_steps(jj, carry):
        for n in range(ATTN_UNROLL):
            step(ATTN_UNROLL * jj + n)
        return carry

    def single_step(kb, carry):
        step(kb)
        return carry

    n_main = lax.div(i, ATTN_UNROLL)
    lax.fori_loop(0, n_main, unrolled_steps, 0)
    lax.fori_loop(n_main * ATTN_UNROLL, i, single_step, 0)

    lv = lamv_ref[...]
    lam = (jnp.exp(jnp.sum(lv[0:1, :] * lv[1:2, :], axis=-1, keepdims=True))
           - jnp.exp(jnp.sum(lv[2:3, :] * lv[3:4, :], axis=-1, keepdims=True)) + lam_init)
    for st in range(n_streams):
        acc = acc_ref[st]
        inv_l = 1.0 / acc[LANES:LANES + 1, :]
        ot = acc[:LANES, :tq] * inv_l[:, :tq] - lam * (acc[:LANES, tq:] * inv_l[:, tq:])
        o = _rms(ot.T, sg_ref[...]) * (1.0 - lam_init)
        cols = slice(st * LANES, (st + 1) * LANES)
        o_ref[:, cols] = (o * _silu(g_ref[:, cols])).astype(BF16)


def _attn(qkv, vt, rest, lamv, sg, layer, B, L, tq, tk):
    T = B * L
    nq = L // tq
    assert tq == tk, "the causal block bookkeeping assumes square score blocks"
    lam_init = 0.8 - 0.6 * math.exp(-0.3 * layer)
    ns = ATTN_STREAMS
    width = ns * LANES
    n_grp = GROUP_WIDTH // width
    gate_blk = COL_BG * n_grp
    resident = pl.Buffered(1)
    return pl.pallas_call(
        functools.partial(_attn_kernel, tq=tq, tk=tk, n_streams=ns, lam_init=lam_init),
        grid=(B, n_grp, nq),
        in_specs=[
            pl.BlockSpec((None, 4, DA_HEAD_DIM), lambda b, h, i: (layer, 0, 0)),
            pl.BlockSpec((tq, width), lambda b, h, i: (b * nq + i, h)),
            pl.BlockSpec((L, width), lambda b, h, i: (b, n_grp + h), pipeline_mode=resident),
            pl.BlockSpec((None, L // tk, width, tk), lambda b, h, i: (b, 0, h, 0), pipeline_mode=resident),
            pl.BlockSpec((tq, width), lambda b, h, i: (b * nq + i, gate_blk + h)),
            pl.BlockSpec((None, 1, LANES), lambda b, h, i: (layer, 0, 0)),
        ],
        out_specs=pl.BlockSpec((tq, width), lambda b, h, i: (b * nq + i, h)),
        out_shape=jax.ShapeDtypeStruct((T, GROUP_WIDTH), BF16),
        scratch_shapes=[
            pltpu.VMEM((ns, 2 * tq, LANES), BF16),
            pltpu.VMEM((ns, tk, 2 * tq), F32),
            pltpu.VMEM((ns, SUBLANES, 2 * tq), F32),
            pltpu.VMEM((ns, SUBLANES, 2 * tq), F32),
            pltpu.VMEM((ns, LANES + 2 * SUBLANES, 2 * tq), F32),
        ],
        compiler_params=_params("parallel", "parallel", "arbitrary"),
        name="diffattn",
    )(lamv, qkv, qkv, vt, rest, sg)


def _memkv_kernel(mem_ref, g_ref, w_ref, o_ref):
    o_ref[...] = _dot(_rms(mem_ref[...], g_ref[...]).astype(BF16), w_ref[...]).astype(BF16)


def _memkv(mem2d, g, w, B, M):
    depth, D, n_kv = w.shape
    return pl.pallas_call(
        _memkv_kernel,
        grid=(depth, B),
        in_specs=[
            pl.BlockSpec((M, D), lambda l, b: (b, 0)),
            pl.BlockSpec((None, 1, D), lambda l, b: (l, 0, 0)),
            pl.BlockSpec((None, D, n_kv), lambda l, b: (l, 0, 0)),
        ],
        out_specs=pl.BlockSpec((None, M, n_kv), lambda l, b: (l, b, 0)),
        out_shape=jax.ShapeDtypeStruct((depth, B * M, n_kv), BF16),
        compiler_params=_params("parallel", "parallel"),
        name="memkv",
    )(mem2d, g, w)


def _lru_gates(x, cw_ref, cb_ref, wg_ref, bg_ref, xe_ref, row0, n_rows):
    base = SUBLANES + row0
    xe_ref[base:base + n_rows, :] = x
    xc = (cw_ref[3:4, :] * x
          + cw_ref[2:3, :] * xe_ref[base - 1:base - 1 + n_rows, :]
          + cw_ref[1:2, :] * xe_ref[base - 2:base - 2 + n_rows, :]
          + cw_ref[0:1, :] * xe_ref[base - 3:base - 3 + n_rows, :]
          + cb_ref[...])
    return xc, _dot(xc.astype(BF16), wg_ref[...]) + bg_ref[...]


def _lru_scan(xc, z, gate, lam_ref, a_ref, b_ref, h_ref, carry_ref, chunk):
    r = _sigmoid(z[:, :GROUP_WIDTH])
    gate_i = _sigmoid(z[:, GROUP_WIDTH:])
    nl = -lam_ref[...]
    softplus = jnp.maximum(nl, 0.0) + jnp.log1p(jnp.exp(-jnp.abs(nl)))
    log_a = (-LRU_C) * r * softplus
    a = jnp.exp(log_a)
    a_ref[...] = a
    mult = jnp.sqrt(-jnp.tanh(log_a) * (1.0 + a * a))
    b_ref[...] = mult * (gate_i * xc)

    row = lax.broadcasted_iota(jnp.int32, (SUBLANES, LANES), 0)
    for t in range(chunk // SUBLANES):
        rows = slice(t * SUBLANES, (t + 1) * SUBLANES)
        for j in range(GROUP_WIDTH // LANES):
            cols = slice(j * LANES, (j + 1) * LANES)
            a = a_ref[rows, cols]
            b = b_ref[rows, cols] + jnp.where(row == 0, a, 0.0) * carry_ref[:, cols]
            w = jnp.where(row == 0, 0.0, a)
            for shift in (1, 2, 4):
                b = b + w * pltpu.roll(b, shift, axis=0)
                if shift != 4:
                    w = w * pltpu.roll(w, shift, axis=0)
            h_ref[rows, cols] = b
            carry_ref[:, cols] = jnp.broadcast_to(b[SUBLANES - 1:, :], (SUBLANES, LANES))
    return h_ref[...] * _silu(gate)


def _memattn_body(q, gate, kv_ref):
    qb = q.astype(BF16)
    scale = MEM_HEAD_DIM ** -0.5 * LOG2E
    outs = []
    for h in range(MEM_HEADS):
        cols = slice(h * MEM_HEAD_DIM, (h + 1) * MEM_HEAD_DIM)
        k = kv_ref[:, h * MEM_HEAD_DIM:(h + 1) * MEM_HEAD_DIM]
        v = kv_ref[:, GROUP_WIDTH + h * MEM_HEAD_DIM:GROUP_WIDTH + (h + 1) * MEM_HEAD_DIM]
        s = _dot_nt(qb[:, cols], k) * scale
        e = jnp.exp2(s - jnp.max(s, axis=-1, keepdims=True))
        outs.append(_dot(e.astype(BF16), v) / jnp.sum(e, axis=-1, keepdims=True))
    return jnp.concatenate(outs, axis=1) * _silu(gate)


def _tail_kernel(cx_ref, cg_ref, mq_ref, mg_ref, ya_ref, yb_ref, x_ref, kv_ref,
                 cw_ref, cb_ref, wg_ref, bg_ref, lam_ref, w_ref, fg_ref, o_ref,
                 xe_ref, a_ref, b_ref, h_ref, carry_ref, *, chunk, n_sub, final):
    c = pl.program_id(1)

    @pl.when(c == 0)
    def _():
        xe_ref[0:SUBLANES, :] = jnp.zeros((SUBLANES, GROUP_WIDTH), F32)
        carry_ref[...] = jnp.zeros(carry_ref.shape, F32)

    @pl.when(c > 0)
    def _():
        xe_ref[0:SUBLANES, :] = xe_ref[chunk:chunk + SUBLANES, :]

    gw = GROUP_WIDTH
    sub = chunk // n_sub

    def front(h):
        rows = pl.ds(h * sub, sub)
        xc, z = _lru_gates(cx_ref[rows, :], cw_ref, cb_ref, wg_ref, bg_ref, xe_ref, h * sub, sub)
        acc = (x_ref[rows, :] + _dot(ya_ref[rows, :], w_ref[0:gw, :])
               + _dot(yb_ref[rows, :], w_ref[gw:2 * gw, :]))
        y_m = _memattn_body(mq_ref[rows, :], mg_ref[rows, :], kv_ref)
        return xc, z, acc + _dot(y_m.astype(BF16), w_ref[3 * gw:4 * gw, :])

    fronts = [front(h) for h in range(n_sub)]
    for h, (xc, z, acc) in enumerate(fronts):
        rows = pl.ds(h * sub, sub)
        y_c = _lru_scan(xc, z, cg_ref[rows, :], lam_ref, a_ref.at[rows], b_ref.at[rows],
                        h_ref.at[rows], carry_ref, sub)
        acc = acc + _dot(y_c.astype(BF16), w_ref[2 * gw:3 * gw, :])
        o_ref[rows, :] = _rms(acc, fg_ref[...]) if final else acc


def _tail(rest, y_a, y_b, x, kv, cw, cb, wg, bg, lam, w_all, fg, layer, final, B, L, M, chunk):
    T, D = x.shape
    nc = L // chunk
    gw = GROUP_WIDTH

    def row_blk(b, c):
        return (b * nc + c, 0)

    def rest_spec(col):
        return pl.BlockSpec((chunk, gw), lambda b, c: (b * nc + c, col))

    def per_layer(shape):
        return pl.BlockSpec((None,) + shape, lambda b, c: (layer,) + (0,) * len(shape))

    return pl.pallas_call(
        functools.partial(_tail_kernel, chunk=chunk, n_sub=TAIL_ROW_GROUPS, final=final),
        grid=(B, nc),
        in_specs=[
            rest_spec(COL_CX), rest_spec(COL_CG), rest_spec(COL_MQ), rest_spec(COL_MG),
            pl.BlockSpec((chunk, gw), row_blk),
            pl.BlockSpec((chunk, gw), row_blk),
            pl.BlockSpec((chunk, D), row_blk),
            pl.BlockSpec((None, M, 2 * gw), lambda b, c: (layer, b, 0)),
            per_layer((4, gw)), per_layer((1, gw)), per_layer((gw, 2 * gw)), per_layer((1, 2 * gw)),
            per_layer((1, gw)), per_layer(w_all.shape[1:]),
            pl.BlockSpec((1, D), lambda b, c: (0, 0)),
        ],
        out_specs=pl.BlockSpec((chunk, D), row_blk),
        out_shape=jax.ShapeDtypeStruct((T, D), F32),
        scratch_shapes=[
            pltpu.VMEM((chunk + SUBLANES, gw), F32),
            pltpu.VMEM((chunk, gw), F32),
            pltpu.VMEM((chunk, gw), F32),
            pltpu.VMEM((chunk, gw), F32),
            pltpu.VMEM((SUBLANES, gw), F32),
        ],
        compiler_params=_params("arbitrary", "arbitrary"),
        name="tail",
    )(rest, rest, rest, rest, y_a, y_b, x, kv, cw, cb, wg, bg, lam, w_all, fg)


def _block_diag(blocks):
    n = blocks.shape[-3]
    eye = jnp.eye(n, dtype=blocks.dtype)
    out = jnp.einsum('...nrc,nm->...nrmc', blocks, eye)
    return out.reshape(blocks.shape[:-3] + (n * blocks.shape[-2], n * blocks.shape[-1]))


def _s5_prep(lam_re, lam_im, log_dt, b_re, b_im, c_re, c_im):
    depth = lam_re.shape[0]
    dt = jnp.exp(log_dt)[..., None]
    mag = jnp.exp(lam_re * dt)
    abar_re = mag * jnp.cos(lam_im * dt)
    abar_im = mag * jnp.sin(lam_im * dt)
    den = lam_re * lam_re + lam_im * lam_im
    nr, ni = abar_re - 1.0, abar_im
    f_re = (nr * lam_re + ni * lam_im) / den
    f_im = (ni * lam_re - nr * lam_im) / den
    bb_re = f_re[..., None] * b_re - f_im[..., None] * b_im
    bb_im = f_re[..., None] * b_im + f_im[..., None] * b_re
    gpb = LANES // S5_CH

    def in_blocks(bb):
        return _block_diag(jnp.swapaxes(bb, -1, -2).reshape(depth, S5_BLOCKS, gpb, S5_CH, S5_STATE))

    def out_blocks(cc):
        return _block_diag(jnp.swapaxes(cc, -1, -2).reshape(depth, S5_BLOCKS, gpb, S5_STATE, S5_CH))

    pr, pi = jnp.ones_like(abar_re), jnp.zeros_like(abar_im)
    wb_d, powers = [], []
    for _ in range(S5_UNROLL):
        wd_re = pr[..., None] * bb_re - pi[..., None] * bb_im
        wd_im = pr[..., None] * bb_im + pi[..., None] * bb_re
        wb_d.append(jnp.concatenate([in_blocks(wd_re), in_blocks(wd_im)], axis=-1))
        pr, pi = pr * abar_re - pi * abar_im, pr * abar_im + pi * abar_re
        powers.append((pr.reshape(depth, 1, -1), pi.reshape(depth, 1, -1)))
    wb = jnp.concatenate(wb_d, axis=-2).astype(BF16)
    wc = jnp.concatenate([out_blocks(c_re), out_blocks(-c_im)], axis=-2).astype(BF16)

    zero = jnp.zeros_like(powers[0][0])
    pad = [zero] * (SUBLANES - S5_UNROLL)
    carry_re = jnp.concatenate([p[0] for p in powers] + pad, axis=1)
    carry_im = jnp.concatenate([p[1] for p in powers] + pad, axis=1)
    step_re = jnp.concatenate([zero] * S5_UNROLL + [powers[-1][0]] * (SUBLANES - S5_UNROLL), axis=1)
    step_im = jnp.concatenate([zero] * S5_UNROLL + [powers[-1][1]] * (SUBLANES - S5_UNROLL), axis=1)
    coef = jnp.stack([carry_re, carry_im, step_re, step_im], axis=1)
    return wb, wc, coef.astype(F32)


def _rope_tables(positions):
    inv_freq = ROPE_THETA ** (-jnp.arange(0, ROPE_DIMS, 2, dtype=F32) / ROPE_DIMS)
    ang = positions.astype(F32).reshape(-1, 1) * inv_freq
    cos, sin = jnp.cos(ang), jnp.sin(ang)
    half = ROPE_DIMS // 2
    pad = DA_HEAD_DIM - ROPE_DIMS
    n = cos.shape[0]
    ones, zeros = jnp.ones((n, pad), F32), jnp.zeros((n, pad), F32)
    zh = jnp.zeros((n, half), F32)
    rc = jnp.concatenate([cos, cos, ones], axis=1)
    rs1 = jnp.concatenate([-sin, zh, zeros], axis=1)
    rs2 = jnp.concatenate([zh, sin, zeros], axis=1)
    rep = LANES // DA_HEAD_DIM
    return jnp.tile(rc, (1, rep)), jnp.tile(rs1, (1, rep)), jnp.tile(rs2, (1, rep))


def _tiles(L):
    return dict(chunk=min(512, L), tq=min(512, L), tk=min(512, L))


def kernel(x, mem, positions, norm_g, w_in, w_out, s5_lambda_re, s5_lambda_im, s5_log_dt, s5_b_re, s5_b_im, s5_c_re, s5_c_im, s5_d, s5_w_glu, da_lambda_q1, da_lambda_k1, da_lambda_q2, da_lambda_k2, da_subln_g, lru_conv_w, lru_conv_b, lru_w_a, lru_b_a, lru_w_x, lru_b_x, lru_lambda, mem_norm_g, w_mem_kv, final_norm_g):
    B, L, D = x.shape
    M = mem.shape[1]
    depth = w_in.shape[0]
    T = B * L
    t = _tiles(L)
    assert L % t["tk"] == 0 and L % t["chunk"] == 0

    gw = GROUP_WIDTH
    w_in_b = w_in.astype(BF16)
    w_out_b = w_out.astype(BF16)
    wglu_b = s5_w_glu.astype(BF16)
    wb, wc, coef = _s5_prep(s5_lambda_re, s5_lambda_im, s5_log_dt, s5_b_re, s5_b_im, s5_c_re, s5_c_im)
    s5_d2 = s5_d.reshape(depth, 1, gw)
    lamv = jnp.stack([da_lambda_q1, da_lambda_k1, da_lambda_q2, da_lambda_k2], axis=1)
    sg = da_subln_g.reshape(depth, 1, LANES)
    wg = jnp.concatenate([_block_diag(lru_w_a), _block_diag(lru_w_x)], axis=-1).astype(BF16)
    bg = jnp.concatenate([lru_b_a, lru_b_x], axis=-1).reshape(depth, 1, 2 * gw)
    lru_cb = lru_conv_b.reshape(depth, 1, gw)
    lru_lam = lru_lambda.reshape(depth, 1, gw)
    norm_g3 = norm_g.reshape(depth, 1, D)
    rc, rs1, rs2 = _rope_tables(positions)

    kv = _memkv(mem.reshape(B * M, D), mem_norm_g.reshape(depth, 1, D), w_mem_kv.astype(BF16), B, M)

    xf = x.reshape(T, D)
    fg = final_norm_g.reshape(1, D)
    for layer in range(depth):
        rest, qkv, y_a, vt = _inproj_s5(xf, norm_g3, w_in_b, layer, rc, rs1, rs2,
                                        wb, coef, wc, s5_d2, wglu_b, B, L, t["chunk"], t["tk"])
        y_b = _attn(qkv, vt, rest, lamv, sg, layer, B, L, t["tq"], t["tk"])
        xf = _tail(rest, y_a, y_b, xf, kv, lru_conv_w, lru_cb, wg, bg, lru_lam, w_out_b, fg,
                   layer, layer == depth - 1, B, L, M, t["chunk"])
    return xf.reshape(B, L, D)
```

```python
import functools
import math

import jax
import jax.numpy as jnp
from jax import lax
from jax.experimental import pallas as pl
from jax.experimental.pallas import tpu as pltpu

F32 = jnp.float32
BF16 = jnp.bfloat16

EPS = 1e-6
GROUP_WIDTH = 512
LANES = 128
SUBLANES = 8
S5_STATE = 64
S5_CH = 16
S5_BLOCKS = 4
S5_BLOCK_STATES = 512
S5_UNROLL = 4
S5_ROW_GROUPS = 2
TAIL_ROW_GROUPS = 1
ATTN_UNROLL = 2
ATTN_STREAMS = 4
DA_HEADS = 4
DA_HEAD_DIM = 64
ROPE_DIMS = 16
ROPE_THETA = 500000.0
LRU_BLOCKS = 8
LRU_C = 8.0
MEM_HEADS = 4
MEM_HEAD_DIM = 128
NEG_BIG = -1e30
LOG2E = math.log2(math.e)
Q_SCALE = DA_HEAD_DIM ** -0.5 * LOG2E
VMEM_LIMIT_BYTES = 56 * 1024 * 1024

W_COL_AU = 0
W_COL_Q = 2 * GROUP_WIDTH
W_COL_REST = 5 * GROUP_WIDTH
COL_BG, COL_CX, COL_CG, COL_MQ, COL_MG = range(5)
N_REST = 5 * GROUP_WIDTH


def _params(*sem):
    return pltpu.CompilerParams(dimension_semantics=sem, vmem_limit_bytes=VMEM_LIMIT_BYTES)


def _sigmoid(x):
    return 0.5 * jnp.tanh(0.5 * x) + 0.5


def _silu(x):
    return x * _sigmoid(x)


def _gelu_tanh(x):
    return 0.5 * x * (1.0 + jnp.tanh(math.sqrt(2.0 / math.pi) * (x + 0.044715 * (x * x * x))))


def _dot(a, b):
    return jnp.dot(a, b, preferred_element_type=F32)


def _dot_nt(a, b):
    return lax.dot_general(a, b, (((1,), (1,)), ((), ())), preferred_element_type=F32)


def _rms(x, g):
    ms = jnp.mean(x * x, axis=-1, keepdims=True)
    return x * lax.rsqrt(ms + EPS) * g


def _s5_input(u, wb_ref, us_ref, xs_ref, chunk):
    us_ref[0:SUBLANES, :] = jnp.zeros((SUBLANES, GROUP_WIDTH), F32)
    us_ref[SUBLANES:SUBLANES + chunk, :] = u
    row_in_tile = lax.broadcasted_iota(jnp.int32, (chunk, LANES), 0) % SUBLANES
    blk = 2 * S5_BLOCK_STATES
    for q in range(S5_BLOCKS):
        cols = slice(q * LANES, (q + 1) * LANES)
        parts = [u[:, cols].astype(BF16)]
        for d in range(1, S5_UNROLL):
            shifted = us_ref[SUBLANES - d:SUBLANES - d + chunk, cols]
            parts.append(jnp.where(row_in_tile >= d, shifted, 0.0).astype(BF16))
        xs_ref[:, q * blk:(q + 1) * blk] = _dot(jnp.concatenate(parts, axis=1), wb_ref[q])


def _s5_scan_out(u, gate, coef_ref, wc_ref, d_ref, wglu_ref, xs_ref, carry_ref, chunk):
    blk = 2 * S5_BLOCK_STATES

    def cmul_add(br, bi, ar, ai, sr, si):
        return br + (ar * sr - ai * si), bi + (ar * si + ai * sr)

    for r in range(chunk // SUBLANES):
        rows = slice(r * SUBLANES, (r + 1) * SUBLANES)
        for q in range(S5_BLOCKS):
            for j in range(S5_BLOCK_STATES // LANES):
                lr = q * blk + j * LANES
                li = lr + S5_BLOCK_STATES
                lc = q * S5_BLOCK_STATES + j * LANES
                xr, xi = cmul_add(xs_ref[rows, lr:lr + LANES], xs_ref[rows, li:li + LANES],
                                  coef_ref[0, :, lc:lc + LANES], coef_ref[1, :, lc:lc + LANES],
                                  carry_ref[:, lr:lr + LANES], carry_ref[:, li:li + LANES])
                xr, xi = cmul_add(xr, xi, coef_ref[2, :, lc:lc + LANES], coef_ref[3, :, lc:lc + LANES],
                                  pltpu.roll(xr, S5_UNROLL, axis=0), pltpu.roll(xi, S5_UNROLL, axis=0))
                xs_ref[rows, lr:lr + LANES] = xr
                xs_ref[rows, li:li + LANES] = xi
                carry_ref[:, lr:lr + LANES] = jnp.broadcast_to(xr[SUBLANES - 1:, :], (SUBLANES, LANES))
                carry_ref[:, li:li + LANES] = jnp.broadcast_to(xi[SUBLANES - 1:, :], (SUBLANES, LANES))

    ys = []
    for q in range(S5_BLOCKS):
        xb = xs_ref[:, q * blk:(q + 1) * blk].astype(BF16)
        ys.append(_dot(xb, wc_ref[q]))
    y = jnp.concatenate(ys, axis=1) + d_ref[...] * u
    y = _gelu_tanh(y)
    z = _dot(y.astype(BF16), wglu_ref[...])
    return z[:, :GROUP_WIDTH] * _sigmoid(z[:, GROUP_WIDTH:]) * _silu(gate)


def _inproj_s5_kernel(x_ref, g_ref, w_ref, rc_ref, rs1_ref, rs2_ref,
                      wb_ref, coef_ref, wc_ref, d_ref, wglu_ref,
                      rest_ref, qkv_ref, ya_ref, vt_ref, us_ref, xs_ref, carry_ref, *, chunk, n_sub):
    @pl.when(pl.program_id(1) == 0)
    def _():
        carry_ref[...] = jnp.zeros(carry_ref.shape, F32)

    gw = GROUP_WIDTH
    sub = chunk // n_sub

    def projections(h):
        rows = pl.ds(h * sub, sub)
        hb = _rms(x_ref[rows, :], g_ref[...]).astype(BF16)
        u = _dot(hb, w_ref[:, W_COL_AU:W_COL_AU + gw])
        _s5_input(u, wb_ref, us_ref.at[h], xs_ref.at[rows], sub)
        gate = _dot(hb, w_ref[:, W_COL_AU + gw:W_COL_AU + 2 * gw])
        rest_ref[rows, :] = _dot(hb, w_ref[:, W_COL_REST:W_COL_REST + N_REST])
        qk = _dot(hb, w_ref[:, W_COL_Q:W_COL_Q + 2 * gw])
        c, s1, s2 = rc_ref[rows, :], rs1_ref[rows, :], rs2_ref[rows, :]
        for j in range(2 * gw // LANES):
            t = qk[:, j * LANES:(j + 1) * LANES]
            t = (t * c + pltpu.roll(t, LANES - ROPE_DIMS // 2, axis=1) * s1
                 + pltpu.roll(t, ROPE_DIMS // 2, axis=1) * s2)
            if j < gw // LANES:
                t = t * Q_SCALE
            qkv_ref[rows, j * LANES:(j + 1) * LANES] = t.astype(BF16)
        v = _dot(hb, w_ref[:, W_COL_Q + 2 * gw:W_COL_Q + 3 * gw])
        qkv_ref[rows, 2 * gw:] = v.astype(BF16)
        vt_ref[:, h * sub:(h + 1) * sub] = v.T.astype(BF16)
        return u, gate

    fronts = [projections(h) for h in range(n_sub)]
    for h, (u, gate) in enumerate(fronts):
        rows = pl.ds(h * sub, sub)
        ya_ref[rows, :] = _s5_scan_out(u, gate, coef_ref, wc_ref, d_ref, wglu_ref,
                                       xs_ref.at[rows], carry_ref, sub).astype(BF16)


def _inproj_s5(x, g, w_all, layer, rc, rs1, rs2, wb, coef, wc, d, wglu, B, L, chunk, tk):
    T, D = x.shape
    n_in = w_all.shape[-1]
    nc = L // chunk
    per_kv = tk // chunk
    n_state = S5_BLOCKS * 2 * S5_BLOCK_STATES

    def row_blk(b, c):
        return (b * nc + c, 0)

    return pl.pallas_call(
        functools.partial(_inproj_s5_kernel, chunk=chunk, n_sub=S5_ROW_GROUPS),
        grid=(B, nc),
        in_specs=[
            pl.BlockSpec((chunk, D), row_blk),
            pl.BlockSpec((None, 1, D), lambda b, c: (layer, 0, 0)),
            pl.BlockSpec((None, D, n_in), lambda b, c: (layer, 0, 0), pipeline_mode=pl.Buffered(1)),
            pl.BlockSpec((chunk, LANES), row_blk),
            pl.BlockSpec((chunk, LANES), row_blk),
            pl.BlockSpec((chunk, LANES), row_blk),
            pl.BlockSpec((None,) + wb.shape[1:], lambda b, c: (layer, 0, 0, 0)),
            pl.BlockSpec((None,) + coef.shape[1:], lambda b, c: (layer, 0, 0, 0)),
            pl.BlockSpec((None,) + wc.shape[1:], lambda b, c: (layer, 0, 0, 0)),
            pl.BlockSpec((None, 1, GROUP_WIDTH), lambda b, c: (layer, 0, 0)),
            pl.BlockSpec((None,) + wglu.shape[1:], lambda b, c: (layer, 0, 0)),
        ],
        out_specs=[
            pl.BlockSpec((chunk, N_REST), row_blk),
            pl.BlockSpec((chunk, 3 * GROUP_WIDTH), row_blk),
            pl.BlockSpec((chunk, GROUP_WIDTH), row_blk),
            pl.BlockSpec((None, None, GROUP_WIDTH, chunk), lambda b, c: (b, c // per_kv, 0, c % per_kv)),
        ],
        out_shape=[jax.ShapeDtypeStruct((T, N_REST), F32),
                   jax.ShapeDtypeStruct((T, 3 * GROUP_WIDTH), BF16),
                   jax.ShapeDtypeStruct((T, GROUP_WIDTH), BF16),
                   jax.ShapeDtypeStruct((B, L // tk, GROUP_WIDTH, tk), BF16)],
        scratch_shapes=[pltpu.VMEM((S5_ROW_GROUPS, chunk // S5_ROW_GROUPS + SUBLANES, GROUP_WIDTH), F32),
                        pltpu.VMEM((chunk, n_state), F32),
                        pltpu.VMEM((SUBLANES, n_state), F32)],
        compiler_params=_params("arbitrary", "arbitrary"),
        name="inproj_s5",
    )(x, g, w_all, rc, rs1, rs2, wb, coef, wc, d, wglu)


def _attn_kernel(lamv_ref, q_ref, k_ref, vt_ref, g_ref, sg_ref, o_ref,
                 qs_ref, s_ref, smax_ref, m_ref, acc_ref, *, tq, tk, n_streams, lam_init):
    i = pl.program_id(2)
    lane = lax.broadcasted_iota(jnp.int32, (tq, LANES), 1)
    for st in range(n_streams):
        q = q_ref[:, st * LANES:(st + 1) * LANES]
        zero = jnp.zeros_like(q)
        qs_ref[st, 0:tq, :] = jnp.where(lane < DA_HEAD_DIM, q, zero)
        qs_ref[st, tq:, :] = jnp.where(lane >= DA_HEAD_DIM, q, zero)
    m_ref[...] = jnp.full(m_ref.shape, NEG_BIG, F32)
    acc_ref[...] = jnp.zeros(acc_ref.shape, F32)
    ones = jnp.ones((2 * SUBLANES, tk), BF16)

    def scores(st, kb, masked=False):
        k = k_ref[pl.ds(pl.multiple_of(kb * tk, tk), tk), st * LANES:(st + 1) * LANES]
        s = _dot_nt(k, qs_ref[st])
        if masked:
            key = lax.broadcasted_iota(jnp.int32, s.shape, 0)
            col = lax.broadcasted_iota(jnp.int32, s.shape, 1)
            qpos = i * tq + jnp.where(col >= tq, col - tq, col)
            s = jnp.where(kb * tk + key <= qpos, s, NEG_BIG)
        s_ref[st] = s
        smax_ref[st] = jnp.broadcast_to(jnp.max(s, axis=0, keepdims=True), smax_ref.shape[1:])

    def softmax_pv(st, kb):
        vt = jnp.concatenate([vt_ref[kb, st * LANES:(st + 1) * LANES, :], ones], axis=0)
        m_prev = m_ref[st]
        m_new = jnp.maximum(m_prev, smax_ref[st])
        alpha = jnp.exp2(m_prev - m_new)
        p = jnp.exp2(s_ref[st] - m_new[0:1, :])
        acc_ref[st] = alpha[0:1, :] * acc_ref[st] + _dot(vt, p.astype(BF16))
        m_ref[st] = m_new

    last = n_streams - 1
    scores(0, i, masked=True)
    for st in range(n_streams):
        if st < last:
            scores(st + 1, i, masked=True)
        else:
            scores(0, 0)
        softmax_pv(st, i)

    def step(kb):
        for st in range(n_streams):
            if st < last:
                scores(st + 1, kb)
            else:
                scores(0, kb + 1)
            softmax_pv(st, kb)

    def unrolled_steps(jj, carry):
        for n in range(ATTN_UNROLL):
            step(ATTN_UNROLL * jj + n)
        return carry

    def single_step(kb, carry):
        step(kb)
        return carry

    n_main = lax.div(i, ATTN_UNROLL)
    lax.fori_loop(0, n_main, unrolled_steps, 0)
    lax.fori_loop(n_main * ATTN_UNROLL, i, single_step, 0)

    lv = lamv_ref[...]
    lam = (jnp.exp(jnp.sum(lv[0:1, :] * lv[1:2, :], axis=-1, keepdims=True))
           - jnp.exp(jnp.sum(lv[2:3, :] * lv[3:4, :], axis=-1, keepdims=True)) + lam_init)
    for st in range(n_streams):
        acc = acc_ref[st]
        inv_l = 1.0 / acc[LANES:LANES + 1, :]
        ot = acc[:LANES, :tq] * inv_l[:, :tq] - lam * (acc[:LANES, tq:] * inv_l[:, tq:])
        o = _rms(ot.T, sg_ref[...]) * (1.0 - lam_init)
        cols = slice(st * LANES, (st + 1) * LANES)
        o_ref[:, cols] = (o * _silu(g_ref[:, cols])).astype(BF16)


def _attn(qkv, vt, rest, lamv, sg, layer, B, L, tq, tk):
    T = B * L
    nq = L // tq
    assert tq == tk, "the causal block bookkeeping assumes square score blocks"
    lam_init = 0.8 - 0.6 * math.exp(-0.3 * layer)
    ns = ATTN_STREAMS
    width = ns * LANES
    n_grp = GROUP_WIDTH // width
    gate_blk = COL_BG * n_grp
    resident = pl.Buffered(1)
    return pl.pallas_call(
        functools.partial(_attn_kernel, tq=tq, tk=tk, n_streams=ns, lam_init=lam_init),
        grid=(B, n_grp, nq),
        in_specs=[
            pl.BlockSpec((None, 4, DA_HEAD_DIM), lambda b, h, i: (layer, 0, 0)),
            pl.BlockSpec((tq, width), lambda b, h, i: (b * nq + i, h)),
            pl.BlockSpec((L, width), lambda b, h, i: (b, n_grp + h), pipeline_mode=resident),
            pl.BlockSpec((None, L // tk, width, tk), lambda b, h, i: (b, 0, h, 0), pipeline_mode=resident),
            pl.BlockSpec((tq, width), lambda b, h, i: (b * nq + i, gate_blk + h)),
            pl.BlockSpec((None, 1, LANES), lambda b, h, i: (layer, 0, 0)),
        ],
        out_specs=pl.BlockSpec((tq, width), lambda b, h, i: (b * nq + i, h)),
        out_shape=jax.ShapeDtypeStruct((T, GROUP_WIDTH), BF16),
        scratch_shapes=[
            pltpu.VMEM((ns, 2 * tq, LANES), BF16),
            pltpu.VMEM((ns, tk, 2 * tq), F32),
            pltpu.VMEM((ns, SUBLANES, 2 * tq), F32),
            pltpu.VMEM((ns, SUBLANES, 2 * tq), F32),
            pltpu.VMEM((ns, LANES + 2 * SUBLANES, 2 * tq), F32),
        ],
        compiler_params=_params("parallel", "parallel", "arbitrary"),
        name="diffattn",
    )(lamv, qkv, qkv, vt, rest, sg)


def _memkv_kernel(mem_ref, g_ref, w_ref, o_ref):
    o_ref[...] = _dot(_rms(mem_ref[...], g_ref[...]).astype(BF16), w_ref[...]).astype(BF16)


def _memkv(mem2d, g, w, B, M):
    depth, D, n_kv = w.shape
    return pl.pallas_call(
        _memkv_kernel,
        grid=(depth, B),
        in_specs=[
            pl.BlockSpec((M, D), lambda l, b: (b, 0)),
            pl.BlockSpec((None, 1, D), lambda l, b: (l, 0, 0)),
            pl.BlockSpec((None, D, n_kv), lambda l, b: (l, 0, 0)),
        ],
        out_specs=pl.BlockSpec((None, M, n_kv), lambda l, b: (l, b, 0)),
        out_shape=jax.ShapeDtypeStruct((depth, B * M, n_kv), BF16),
        compiler_params=_params("parallel", "parallel"),
        name="memkv",
    )(mem2d, g, w)


def _lru_gates(x, cw_ref, cb_ref, wg_ref, bg_ref, xe_ref, row0, n_rows):
    base = SUBLANES + row0
    xe_ref[base:base + n_rows, :] = x
    xc = (cw_ref[3:4, :] * x
          + cw_ref[2:3, :] * xe_ref[base - 1:base - 1 + n_rows, :]
          + cw_ref[1:2, :] * xe_ref[base - 2:base - 2 + n_rows, :]
          + cw_ref[0:1, :] * xe_ref[base - 3:base - 3 + n_rows, :]
          + cb_ref[...])
    return xc, _dot(xc.astype(BF16), wg_ref[...]) + bg_ref[...]


def _lru_scan(xc, z, gate, lam_ref, a_ref, b_ref, h_ref, carry_ref, chunk):
    r = _sigmoid(z[:, :GROUP_WIDTH])
    gate_i = _sigmoid(z[:, GROUP_WIDTH:])
    nl = -lam_ref[...]
    softplus = jnp.maximum(nl, 0.0) + jnp.log1p(jnp.exp(-jnp.abs(nl)))
    log_a = (-LRU_C) * r * softplus
    a = jnp.exp(log_a)
    a_ref[...] = a
    mult = jnp.sqrt(-jnp.tanh(log_a) * (1.0 + a * a))
    b_ref[...] = mult * (gate_i * xc)

    row = lax.broadcasted_iota(jnp.int32, (SUBLANES, LANES), 0)
    for t in range(chunk // SUBLANES):
        rows = slice(t * SUBLANES, (t + 1) * SUBLANES)
        for j in range(GROUP_WIDTH // LANES):
            cols = slice(j * LANES, (j + 1) * LANES)
            a = a_ref[rows, cols]
            b = b_ref[rows, cols] + jnp.where(row == 0, a, 0.0) * carry_ref[:, cols]
            w = jnp.where(row == 0, 0.0, a)
            for shift in (1, 2, 4):
                b = b + w * pltpu.roll(b, shift, axis=0)
                if shift != 4:
                    w = w * pltpu.roll(w, shift, axis=0)
            h_ref[rows, cols] = b
            carry_ref[:, cols] = jnp.broadcast_to(b[SUBLANES - 1:, :], (SUBLANES, LANES))
    return h_ref[...] * _silu(gate)


def _memattn_body(q, gate, kv_ref):
    qb = q.astype(BF16)
    scale = MEM_HEAD_DIM ** -0.5 * LOG2E
    outs = []
    for h in range(MEM_HEADS):
        cols = slice(h * MEM_HEAD_DIM, (h + 1) * MEM_HEAD_DIM)
        k = kv_ref[:, h * MEM_HEAD_DIM:(h + 1) * MEM_HEAD_DIM]
        v = kv_ref[:, GROUP_WIDTH + h * MEM_HEAD_DIM:GROUP_WIDTH + (h + 1) * MEM_HEAD_DIM]
        s = _dot_nt(qb[:, cols], k) * scale
        e = jnp.exp2(s - jnp.max(s, axis=-1, keepdims=True))
        outs.append(_dot(e.astype(BF16), v) / jnp.sum(e, axis=-1, keepdims=True))
    return jnp.concatenate(outs, axis=1) * _silu(gate)


def _tail_kernel(cx_ref, cg_ref, mq_ref, mg_ref, ya_ref, yb_ref, x_ref, kv_ref,
                 cw_ref, cb_ref, wg_ref, bg_ref, lam_ref, w_ref, fg_ref, o_ref,
                 xe_ref, a_ref, b_ref, h_ref, carry_ref, *, chunk, n_sub, final):
    c = pl.program_id(1)

    @pl.when(c == 0)
    def _():
        xe_ref[0:SUBLANES, :] = jnp.zeros((SUBLANES, GROUP_WIDTH), F32)
        carry_ref[...] = jnp.zeros(carry_ref.shape, F32)

    @pl.when(c > 0)
    def _():
        xe_ref[0:SUBLANES, :] = xe_ref[chunk:chunk + SUBLANES, :]

    gw = GROUP_WIDTH
    sub = chunk // n_sub

    def front(h):
        rows = pl.ds(h * sub, sub)
        xc, z = _lru_gates(cx_ref[rows, :], cw_ref, cb_ref, wg_ref, bg_ref, xe_ref, h * sub, sub)
        acc = (x_ref[rows, :] + _dot(ya_ref[rows, :], w_ref[0:gw, :])
               + _dot(yb_ref[rows, :], w_ref[gw:2 * gw, :]))
        y_m = _memattn_body(mq_ref[rows, :], mg_ref[rows, :], kv_ref)
        return xc, z, acc + _dot(y_m.astype(BF16), w_ref[3 * gw:4 * gw, :])

    fronts = [front(h) for h in range(n_sub)]
    for h, (xc, z, acc) in enumerate(fronts):
        rows = pl.ds(h * sub, sub)
        y_c = _lru_scan(xc, z, cg_ref[rows, :], lam_ref, a_ref.at[rows], b_ref.at[rows],
                        h_ref.at[rows], carry_ref, sub)
        acc = acc + _dot(y_c.astype(BF16), w_ref[2 * gw:3 * gw, :])
        o_ref[rows, :] = _rms(acc, fg_ref[...]) if final else acc


def _tail(rest, y_a, y_b, x, kv, cw, cb, wg, bg, lam, w_all, fg, layer, final, B, L, M, chunk):
    T, D = x.shape
    nc = L // chunk
    gw = GROUP_WIDTH

    def row_blk(b, c):
        return (b * nc + c, 0)

    def rest_spec(col):
        return pl.BlockSpec((chunk, gw), lambda b, c: (b * nc + c, col))

    def per_layer(shape):
        return pl.BlockSpec((None,) + shape, lambda b, c: (layer,) + (0,) * len(shape))

    return pl.pallas_call(
        functools.partial(_tail_kernel, chunk=chunk, n_sub=TAIL_ROW_GROUPS, final=final),
        grid=(B, nc),
        in_specs=[
            rest_spec(COL_CX), rest_spec(COL_CG), rest_spec(COL_MQ), rest_spec(COL_MG),
            pl.BlockSpec((chunk, gw), row_blk),
            pl.BlockSpec((chunk, gw), row_blk),
            pl.BlockSpec((chunk, D), row_blk),
            pl.BlockSpec((None, M, 2 * gw), lambda b, c: (layer, b, 0)),
            per_layer((4, gw)), per_layer((1, gw)), per_layer((gw, 2 * gw)), per_layer((1, 2 * gw)),
            per_layer((1, gw)), per_layer(w_all.shape[1:]),
            pl.BlockSpec((1, D), lambda b, c: (0, 0)),
        ],
        out_specs=pl.BlockSpec((chunk, D), row_blk),
        out_shape=jax.ShapeDtypeStruct((T, D), F32),
        scratch_shapes=[
            pltpu.VMEM((chunk + SUBLANES, gw), F32),
            pltpu.VMEM((chunk, gw), F32),
            pltpu.VMEM((chunk, gw), F32),
            pltpu.VMEM((chunk, gw), F32),
            pltpu.VMEM((SUBLANES, gw), F32),
        ],
        compiler_params=_params("arbitrary", "arbitrary"),
        name="tail",
    )(rest, rest, rest, rest, y_a, y_b, x, kv, cw, cb, wg, bg, lam, w_all, fg)


def _block_diag(blocks):
    n = blocks.shape[-3]
    eye = jnp.eye(n, dtype=blocks.dtype)
    out = jnp.einsum('...nrc,nm->...nrmc', blocks, eye)
    return out.reshape(blocks.shape[:-3] + (n * blocks.shape[-2], n * blocks.shape[-1]))


def _s5_prep(lam_re, lam_im, log_dt, b_re, b_im, c_re, c_im):
    depth = lam_re.shape[0]
    dt = jnp.exp(log_dt)[..., None]
    mag = jnp.exp(lam_re * dt)
    abar_re = mag * jnp.cos(lam_im * dt)
    abar_im = mag * jnp.sin(lam_im * dt)
    den = lam_re * lam_re + lam_im * lam_im
    nr, ni = abar_re - 1.0, abar_im
    f_re = (nr * lam_re + ni * lam_im) / den
    f_im = (ni * lam_re - nr * lam_im) / den
    bb_re = f_re[..., None] * b_re - f_im[..., None] * b_im
    bb_im = f_re[..., None] * b_im + f_im[..., None] * b_re
    gpb = LANES // S5_CH

    def in_blocks(bb):
        return _block_diag(jnp.swapaxes(bb, -1, -2).reshape(depth, S5_BLOCKS, gpb, S5_CH, S5_STATE))

    def out_blocks(cc):
        return _block_diag(jnp.swapaxes(cc, -1, -2).reshape(depth, S5_BLOCKS, gpb, S5_STATE, S5_CH))

    pr, pi = jnp.ones_like(abar_re), jnp.zeros_like(abar_im)
    wb_d, powers = [], []
    for _ in range(S5_UNROLL):
        wd_re = pr[..., None] * bb_re - pi[..., None] * bb_im
        wd_im = pr[..., None] * bb_im + pi[..., None] * bb_re
        wb_d.append(jnp.concatenate([in_blocks(wd_re), in_blocks(wd_im)], axis=-1))
        pr, pi = pr * abar_re - pi * abar_im, pr * abar_im + pi * abar_re
        powers.append((pr.reshape(depth, 1, -1), pi.reshape(depth, 1, -1)))
    wb = jnp.concatenate(wb_d, axis=-2).astype(BF16)
    wc = jnp.concatenate([out_blocks(c_re), out_blocks(-c_im)], axis=-2).astype(BF16)

    zero = jnp.zeros_like(powers[0][0])
    pad = [zero] * (SUBLANES - S5_UNROLL)
    carry_re = jnp.concatenate([p[0] for p in powers] + pad, axis=1)
    carry_im = jnp.concatenate([p[1] for p in powers] + pad, axis=1)
    step_re = jnp.concatenate([zero] * S5_UNROLL + [powers[-1][0]] * (SUBLANES - S5_UNROLL), axis=1)
    step_im = jnp.concatenate([zero] * S5_UNROLL + [powers[-1][1]] * (SUBLANES - S5_UNROLL), axis=1)
    coef = jnp.stack([carry_re, carry_im, step_re, step_im], axis=1)
    return wb, wc, coef.astype(F32)


def _rope_tables(positions):
    inv_freq = ROPE_THETA ** (-jnp.arange(0, ROPE_DIMS, 2, dtype=F32) / ROPE_DIMS)
    ang = positions.astype(F32).reshape(-1, 1) * inv_freq
    cos, sin = jnp.cos(ang), jnp.sin(ang)
    half = ROPE_DIMS // 2
    pad = DA_HEAD_DIM - ROPE_DIMS
    n = cos.shape[0]
    ones, zeros = jnp.ones((n, pad), F32), jnp.zeros((n, pad), F32)
    zh = jnp.zeros((n, half), F32)
    rc = jnp.concatenate([cos, cos, ones], axis=1)
    rs1 = jnp.concatenate([-sin, zh, zeros], axis=1)
    rs2 = jnp.concatenate([zh, sin, zeros], axis=1)
    rep = LANES // DA_HEAD_DIM
    return jnp.tile(rc, (1, rep)), jnp.tile(rs1, (1, rep)), jnp.tile(rs2, (1, rep))


def _tiles(L):
    return dict(chunk=min(512, L), tq=min(512, L), tk=min(512, L))


def kernel(x, mem, positions, norm_g, w_in, w_out, s5_lambda_re, s5_lambda_im, s5_log_dt, s5_b_re, s5_b_im, s5_c_re, s5_c_im, s5_d, s5_w_glu, da_lambda_q1, da_lambda_k1, da_lambda_q2, da_lambda_k2, da_subln_g, lru_conv_w, lru_conv_b, lru_w_a, lru_b_a, lru_w_x, lru_b_x, lru_lambda, mem_norm_g, w_mem_kv, final_norm_g):
    B, L, D = x.shape
    M = mem.shape[1]
    depth = w_in.shape[0]
    T = B * L
    t = _tiles(L)
    assert L % t["tk"] == 0 and L % t["chunk"] == 0

    gw = GROUP_WIDTH
    w_in_b = w_in.astype(BF16)
    w_out_b = w_out.astype(BF16)
    wglu_b = s5_w_glu.astype(BF16)
    wb, wc, coef = _s5_prep(s5_lambda_re, s5_lambda_im, s5_log_dt, s5_b_re, s5_b_im, s5_c_re, s5_c_im)
    s5_d2 = s5_d.reshape(depth, 1, gw)
    lamv = jnp.stack([da_lambda_q1, da_lambda_k1, da_lambda_q2, da_lambda_k2], axis=1)
    sg = da_subln_g.reshape(depth, 1, LANES)
    wg = jnp.concatenate([_block_diag(lru_w_a), _block_diag(lru_w_x)], axis=-1).astype(BF16)
    bg = jnp.concatenate([lru_b_a, lru_b_x], axis=-1).reshape(depth, 1, 2 * gw)
    lru_cb = lru_conv_b.reshape(depth, 1, gw)
    lru_lam = lru_lambda.reshape(depth, 1, gw)
    norm_g3 = norm_g.reshape(depth, 1, D)
    rc, rs1, rs2 = _rope_tables(positions)

    kv = _memkv(mem.reshape(B * M, D), mem_norm_g.reshape(depth, 1, D), w_mem_kv.astype(BF16), B, M)

    xf = x.reshape(T, D)
    fg = final_norm_g.reshape(1, D)
    for layer in range(depth):
        rest, qkv, y_a, vt = _inproj_s5(xf, norm_g3, w_in_b, layer, rc, rs1, rs2,
                                        wb, coef, wc, s5_d2, wglu_b, B, L, t["chunk"], t["tk"])
        y_b = _attn(qkv, vt, rest, lamv, sg, layer, B, L, t["tq"], t["tk"])
        xf = _tail(rest, y_a, y_b, xf, kv, lru_conv_w, lru_cb, wg, bg, lru_lam, w_out_b, fg,
                   layer, layer == depth - 1, B, L, M, t["chunk"])
    return xf.reshape(B, L, D)
```

```python
import functools
import math

import jax
import jax.numpy as jnp
from jax import lax
from jax.experimental import pallas as pl
from jax.experimental.pallas import tpu as pltpu

F32 = jnp.float32
BF16 = jnp.bfloat16

EPS = 1e-6
GROUP_WIDTH = 512
LANES = 128
SUBLANES = 8
S5_STATE = 64
S5_CH = 16
S5_BLOCKS = 4
S5_BLOCK_STATES = 512
S5_UNROLL = 4
S5_ROW_GROUPS = 2
TAIL_ROW_GROUPS = 1
ATTN_UNROLL = 2
ATTN_STREAMS = 4
DA_HEADS = 4
DA_HEAD_DIM = 64
ROPE_DIMS = 16
ROPE_THETA = 500000.0
LRU_BLOCKS = 8
LRU_C = 8.0
MEM_HEADS = 4
MEM_HEAD_DIM = 128
NEG_BIG = -1e30
LOG2E = math.log2(math.e)
Q_SCALE = DA_HEAD_DIM ** -0.5 * LOG2E
VMEM_LIMIT_BYTES = 56 * 1024 * 1024

W_COL_AU = 0
W_COL_Q = 2 * GROUP_WIDTH
W_COL_REST = 5 * GROUP_WIDTH
COL_BG, COL_CX, COL_CG, COL_MQ, COL_MG = range(5)
N_REST = 5 * GROUP_WIDTH


def _params(*sem):
    return pltpu.CompilerParams(dimension_semantics=sem, vmem_limit_bytes=VMEM_LIMIT_BYTES)


def _sigmoid(x):
    return 0.5 * jnp.tanh(0.5 * x) + 0.5


def _silu(x):
    return x * _sigmoid(x)


def _gelu_tanh(x):
    return 0.5 * x * (1.0 + jnp.tanh(math.sqrt(2.0 / math.pi) * (x + 0.044715 * (x * x * x))))


def _dot(a, b):
    return jnp.dot(a, b, preferred_element_type=F32)


def _dot_nt(a, b):
    return lax.dot_general(a, b, (((1,), (1,)), ((), ())), preferred_element_type=F32)


def _rms(x, g):
    ms = jnp.mean(x * x, axis=-1, keepdims=True)
    return x * lax.rsqrt(ms + EPS) * g


def _s5_input(u, wb_ref, us_ref, xs_ref, chunk):
    us_ref[0:SUBLANES, :] = jnp.zeros((SUBLANES, GROUP_WIDTH), F32)
    us_ref[SUBLANES:SUBLANES + chunk, :] = u
    row_in_tile = lax.broadcasted_iota(jnp.int32, (chunk, LANES), 0) % SUBLANES
    blk = 2 * S5_BLOCK_STATES
    for q in range(S5_BLOCKS):
        cols = slice(q * LANES, (q + 1) * LANES)
        parts = [u[:, cols].astype(BF16)]
        for d in range(1, S5_UNROLL):
            shifted = us_ref[SUBLANES - d:SUBLANES - d + chunk, cols]
            parts.append(jnp.where(row_in_tile >= d, shifted, 0.0).astype(BF16))
        xs_ref[:, q * blk:(q + 1) * blk] = _dot(jnp.concatenate(parts, axis=1), wb_ref[q])


def _s5_scan_out(u, gate, coef_ref, wc_ref, d_ref, wglu_ref, xs_ref, carry_ref, chunk):
    blk = 2 * S5_BLOCK_STATES

    def cmul_add(br, bi, ar, ai, sr, si):
        return br + (ar * sr - ai * si), bi + (ar * si + ai * sr)

    for r in range(chunk // SUBLANES):
        rows = slice(r * SUBLANES, (r + 1) * SUBLANES)
        for q in range(S5_BLOCKS):
            for j in range(S5_BLOCK_STATES // LANES):
                lr = q * blk + j * LANES
                li = lr + S5_BLOCK_STATES
                lc = q * S5_BLOCK_STATES + j * LANES
                xr, xi = cmul_add(xs_ref[rows, lr:lr + LANES], xs_ref[rows, li:li + LANES],
                                  coef_ref[0, :, lc:lc + LANES], coef_ref[1, :, lc:lc + LANES],
                                  carry_ref[:, lr:lr + LANES], carry_ref[:, li:li + LANES])
                xr, xi = cmul_add(xr, xi, coef_ref[2, :, lc:lc + LANES], coef_ref[3, :, lc:lc + LANES],
                                  pltpu.roll(xr, S5_UNROLL, axis=0), pltpu.roll(xi, S5_UNROLL, axis=0))
                xs_ref[rows, lr:lr + LANES] = xr
                xs_ref[rows, li:li + LANES] = xi
                carry_ref[:, lr:lr + LANES] = jnp.broadcast_to(xr[SUBLANES - 1:, :], (SUBLANES, LANES))
                carry_ref[:, li:li + LANES] = jnp.broadcast_to(xi[SUBLANES - 1:, :], (SUBLANES, LANES))

    ys = []
    for q in range(S5_BLOCKS):
        xb = xs_ref[:, q * blk:(q + 1) * blk].astype(BF16)
        ys.append(_dot(xb, wc_ref[q]))
    y = jnp.concatenate(ys, axis=1) + d_ref[...] * u
    y = _gelu_tanh(y)
    z = _dot(y.astype(BF16), wglu_ref[...])
    return z[:, :GROUP_WIDTH] * _sigmoid(z[:, GROUP_WIDTH:]) * _silu(gate)


def _inproj_s5_kernel(x_ref, g_ref, w_ref, rc_ref, rs1_ref, rs2_ref,
                      wb_ref, coef_ref, wc_ref, d_ref, wglu_ref,
                      rest_ref, qkv_ref, ya_ref, vt_ref, us_ref, xs_ref, carry_ref, *, chunk, n_sub):
    @pl.when(pl.program_id(1) == 0)
    def _():
        carry_ref[...] = jnp.zeros(carry_ref.shape, F32)

    gw = GROUP_WIDTH
    sub = chunk // n_sub

    def projections(h):
        rows = pl.ds(h * sub, sub)
        hb = _rms(x_ref[rows, :], g_ref[...]).astype(BF16)
        u = _dot(hb, w_ref[:, W_COL_AU:W_COL_AU + gw])
        _s5_input(u, wb_ref, us_ref.at[h], xs_ref.at[rows], sub)
        gate = _dot(hb, w_ref[:, W_COL_AU + gw:W_COL_AU + 2 * gw])
        rest_ref[rows, :] = _dot(hb, w_ref[:, W_COL_REST:W_COL_REST + N_REST])
        qk = _dot(hb, w_ref[:, W_COL_Q:W_COL_Q + 2 * gw])
        c, s1, s2 = rc_ref[rows, :], rs1_ref[rows, :], rs2_ref[rows, :]
        for j in range(2 * gw // LANES):
            t = qk[:, j * LANES:(j + 1) * LANES]
            t = (t * c + pltpu.roll(t, LANES - ROPE_DIMS // 2, axis=1) * s1
                 + pltpu.roll(t, ROPE_DIMS // 2, axis=1) * s2)
            if j < gw // LANES:
                t = t * Q_SCALE
            qkv_ref[rows, j * LANES:(j + 1) * LANES] = t.astype(BF16)
        v = _dot(hb, w_ref[:, W_COL_Q + 2 * gw:W_COL_Q + 3 * gw])
        qkv_ref[rows, 2 * gw:] = v.astype(BF16)
        vt_ref[:, h * sub:(h + 1) * sub] = v.T.astype(BF16)
        return u, gate

    fronts = [projections(h) for h in range(n_sub)]
    for h, (u, gate) in enumerate(fronts):
        rows = pl.ds(h * sub, sub)
        ya_ref[rows, :] = _s5_scan_out(u, gate, coef_ref, wc_ref, d_ref, wglu_ref,
                                       xs_ref.at[rows], carry_ref, sub).astype(BF16)


def _inproj_s5(x, g, w_all, layer, rc, rs1, rs2, wb, coef, wc, d, wglu, B, L, chunk, tk):
    T, D = x.shape
    n_in = w_all.shape[-1]
    nc = L // chunk
    per_kv = tk // chunk
    n_state = S5_BLOCKS * 2 * S5_BLOCK_STATES

    def row_blk(b, c):
        return (b * nc + c, 0)

    return pl.pallas_call(
        functools.partial(_inproj_s5_kernel, chunk=chunk, n_sub=S5_ROW_GROUPS),
        grid=(B, nc),
        in_specs=[
            pl.BlockSpec((chunk, D), row_blk),
            pl.BlockSpec((None, 1, D), lambda b, c: (layer, 0, 0)),
            pl.BlockSpec((None, D, n_in), lambda b, c: (layer, 0, 0), pipeline_mode=pl.Buffered(1)),
            pl.BlockSpec((chunk, LANES), row_blk),
            pl.BlockSpec((chunk, LANES), row_blk),
            pl.BlockSpec((chunk, LANES), row_blk),
            pl.BlockSpec((None,) + wb.shape[1:], lambda b, c: (layer, 0, 0, 0)),
            pl.BlockSpec((None,) + coef.shape[1:], lambda b, c: (layer, 0, 0, 0)),
            pl.BlockSpec((None,) + wc.shape[1:], lambda b, c: (layer, 0, 0, 0)),
            pl.BlockSpec((None, 1, GROUP_WIDTH), lambda b, c: (layer, 0, 0)),
            pl.BlockSpec((None,) + wglu.shape[1:], lambda b, c: (layer, 0, 0)),
        ],
        out_specs=[
            pl.BlockSpec((chunk, N_REST), row_blk),
            pl.BlockSpec((chunk, 3 * GROUP_WIDTH), row_blk),
            pl.BlockSpec((chunk, GROUP_WIDTH), row_blk),
            pl.BlockSpec((None, None, GROUP_WIDTH, chunk), lambda b, c: (b, c // per_kv, 0, c % per_kv)),
        ],
        out_shape=[jax.ShapeDtypeStruct((T, N_REST), F32),
                   jax.ShapeDtypeStruct((T, 3 * GROUP_WIDTH), BF16),
                   jax.ShapeDtypeStruct((T, GROUP_WIDTH), BF16),
                   jax.ShapeDtypeStruct((B, L // tk, GROUP_WIDTH, tk), BF16)],
        scratch_shapes=[pltpu.VMEM((S5_ROW_GROUPS, chunk // S5_ROW_GROUPS + SUBLANES, GROUP_WIDTH), F32),
                        pltpu.VMEM((chunk, n_state), F32),
                        pltpu.VMEM((SUBLANES, n_state), F32)],
        compiler_params=_params("arbitrary", "arbitrary"),
        name="inproj_s5",
    )(x, g, w_all, rc, rs1, rs2, wb, coef, wc, d, wglu)


def _attn_kernel(lamv_ref, q_ref, k_ref, vt_ref, g_ref, sg_ref, o_ref,
                 qs_ref, s_ref, smax_ref, m_ref, acc_ref, *, tq, tk, n_streams, lam_init):
    i = pl.program_id(2)
    lane = lax.broadcasted_iota(jnp.int32, (tq, LANES), 1)
    for st in range(n_streams):
        q = q_ref[:, st * LANES:(st + 1) * LANES]
        zero = jnp.zeros_like(q)
        qs_ref[st, 0:tq, :] = jnp.where(lane < DA_HEAD_DIM, q, zero)
        qs_ref[st, tq:, :] = jnp.where(lane >= DA_HEAD_DIM, q, zero)
    m_ref[...] = jnp.full(m_ref.shape, NEG_BIG, F32)
    acc_ref[...] = jnp.zeros(acc_ref.shape, F32)
    ones = jnp.ones((2 * SUBLANES, tk), BF16)

    def scores(st, kb, masked=False):
        k = k_ref[pl.ds(pl.multiple_of(kb * tk, tk), tk), st * LANES:(st + 1) * LANES]
        s = _dot_nt(k, qs_ref[st])
        if masked:
            key = lax.broadcasted_iota(jnp.int32, s.shape, 0)
            col = lax.broadcasted_iota(jnp.int32, s.shape, 1)
            qpos = i * tq + jnp.where(col >= tq, col - tq, col)
            s = jnp.where(kb * tk + key <= qpos, s, NEG_BIG)
        s_ref[st] = s
        smax_ref[st] = jnp.broadcast_to(jnp.max(s, axis=0, keepdims=True), smax_ref.shape[1:])

    def softmax_pv(st, kb):
        vt = jnp.concatenate([vt_ref[kb, st * LANES:(st + 1) * LANES, :], ones], axis=0)
        m_prev = m_ref[st]
        m_new = jnp.maximum(m_prev, smax_ref[st])
        alpha = jnp.exp2(m_prev - m_new)
        p = jnp.exp2(s_ref[st] - m_new[0:1, :])
        acc_ref[st] = alpha[0:1, :] * acc_ref[st] + _dot(vt, p.astype(BF16))
        m_ref[st] = m_new

    last = n_streams - 1
    scores(0, i, masked=True)
    for st in range(n_streams):
        if st < last:
            scores(st + 1, i, masked=True)
        else:
            scores(0, 0)
        softmax_pv(st, i)

    def step(kb):
        for st in range(n_streams):
            if st < last:
                scores(st + 1, kb)
            else:
                scores(0, kb + 1)
            softmax_pv(st, kb)

    def unrolled_steps(jj, carry):
        for n in range(ATTN_UNROLL):
            step(ATTN_UNROLL * jj + n)
        return carry

    def single_step(kb, carry):
        step(kb)
        return carry

    n_main = lax.div(i, ATTN_UNROLL)
    lax.fori_loop(0, n_main, unrolled_steps, 0)
    lax.fori_loop(n_main * ATTN_UNROLL, i, single_step, 0)

    lv = lamv_ref[...]
    lam = (jnp.exp(jnp.sum(lv[0:1, :] * lv[1:2, :], axis=-1, keepdims=True))
           - jnp.exp(jnp.sum(lv[2:3, :] * lv[3:4, :], axis=-1, keepdims=True)) + lam_init)
    for st in range(n_streams):
        acc = acc_ref[st]
        inv_l = 1.0 / acc[LANES:LANES + 1, :]
        ot = acc[:LANES, :tq] * inv_l[:, :tq] - lam * (acc[:LANES, tq:] * inv_l[:, tq:])
        o = _rms(ot.T, sg_ref[...]) * (1.0 - lam_init)
        cols = slice(st * LANES, (st + 1) * LANES)
        o_ref[:, cols] = (o * _silu(g_ref[:, cols])).astype(BF16)


def _attn(qkv, vt, rest, lamv, sg, layer, B, L, tq, tk):
    T = B * L
    nq = L // tq
    assert tq == tk, "the causal block bookkeeping assumes square score blocks"
    lam_init = 0.8 - 0.6 * math.exp(-0.3 * layer)
    ns = ATTN_STREAMS
    width = ns * LANES
    n_grp = GROUP_WIDTH // width
    gate_blk = COL_BG * n_grp
    resident = pl.Buffered(1)
    return pl.pallas_call(
        functools.partial(_attn_kernel, tq=tq, tk=tk, n_streams=ns, lam_init=lam_init),
        grid=(B, n_grp, nq),
        in_specs=[
            pl.BlockSpec((None, 4, DA_HEAD_DIM), lambda b, h, i: (layer, 0, 0)),
            pl.BlockSpec((tq, width), lambda b, h, i: (b * nq + i, h)),
            pl.BlockSpec((L, width), lambda b, h, i: (b, n_grp + h), pipeline_mode=resident),
            pl.BlockSpec((None, L // tk, width, tk), lambda b, h, i: (b, 0, h, 0), pipeline_mode=resident),
            pl.BlockSpec((tq, width), lambda b, h, i: (b * nq + i, gate_blk + h)),
            pl.BlockSpec((None, 1, LANES), lambda b, h, i: (layer, 0, 0)),
        ],
        out_specs=pl.BlockSpec((tq, width), lambda b, h, i: (b * nq + i, h)),
        out_shape=jax.ShapeDtypeStruct((T, GROUP_WIDTH), BF16),
        scratch_shapes=[
            pltpu.VMEM((ns, 2 * tq, LANES), BF16),
            pltpu.VMEM((ns, tk, 2 * tq), F32),
            pltpu.VMEM((ns, SUBLANES, 2 * tq), F32),
            pltpu.VMEM((ns, SUBLANES, 2 * tq), F32),
            pltpu.VMEM((ns, LANES + 2 * SUBLANES, 2 * tq), F32),
        ],
        compiler_params=_params("parallel", "parallel", "arbitrary"),
        name="diffattn",
    )(lamv, qkv, qkv, vt, rest, sg)


def _memkv_kernel(mem_ref, g_ref, w_ref, o_ref):
    o_ref[...] = _dot(_rms(mem_ref[...], g_ref[...]).astype(BF16), w_ref[...]).astype(BF16)


def _memkv(mem2d, g, w, B, M):
    depth, D, n_kv = w.shape
    return pl.pallas_call(
        _memkv_kernel,
        grid=(depth, B),
        in_specs=[
            pl.BlockSpec((M, D), lambda l, b: (b, 0)),
            pl.BlockSpec((None, 1, D), lambda l, b: (l, 0, 0)),
            pl.BlockSpec((None, D, n_kv), lambda l, b: (l, 0, 0)),
        ],
        out_specs=pl.BlockSpec((None, M, n_kv), lambda l, b: (l, b, 0)),
        out_shape=jax.ShapeDtypeStruct((depth, B * M, n_kv), BF16),
        compiler_params=_params("parallel", "parallel"),
        name="memkv",
    )(mem2d, g, w)


def _lru_gates(x, cw_ref, cb_ref, wg_ref, bg_ref, xe_ref, row0, n_rows):
    base = SUBLANES + row0
    xe_ref[base:base + n_rows, :] = x
    xc = (cw_ref[3:4, :] * x
          + cw_ref[2:3, :] * xe_ref[base - 1:base - 1 + n_rows, :]
          + cw_ref[1:2, :] * xe_ref[base - 2:base - 2 + n_rows, :]
          + cw_ref[0:1, :] * xe_ref[base - 3:base - 3 + n_rows, :]
          + cb_ref[...])
    return xc, _dot(xc.astype(BF16), wg_ref[...]) + bg_ref[...]


def _lru_scan(xc, z, gate, lam_ref, a_ref, b_ref, h_ref, carry_ref, chunk):
    r = _sigmoid(z[:, :GROUP_WIDTH])
    gate_i = _sigmoid(z[:, GROUP_WIDTH:])
    nl = -lam_ref[...]
    softplus = jnp.maximum(nl, 0.0) + jnp.log1p(jnp.exp(-jnp.abs(nl)))
    log_a = (-LRU_C) * r * softplus
    a = jnp.exp(log_a)
    a_ref[...] = a
    mult = jnp.sqrt(-jnp.tanh(log_a) * (1.0 + a * a))
    b_ref[...] = mult * (gate_i * xc)

    row = lax.broadcasted_iota(jnp.int32, (SUBLANES, LANES), 0)
    for t in range(chunk // SUBLANES):
        rows = slice(t * SUBLANES, (t + 1) * SUBLANES)
        for j in range(GROUP_WIDTH // LANES):
            cols = slice(j * LANES, (j + 1) * LANES)
            a = a_ref[rows, cols]
            b = b_ref[rows, cols] + jnp.where(row == 0, a, 0.0) * carry_ref[:, cols]
            w = jnp.where(row == 0, 0.0, a)
            for shift in (1, 2, 4):
                b = b + w * pltpu.roll(b, shift, axis=0)
                if shift != 4:
                    w = w * pltpu.roll(w, shift, axis=0)
            h_ref[rows, cols] = b
            carry_ref[:, cols] = jnp.broadcast_to(b[SUBLANES - 1:, :], (SUBLANES, LANES))
    return h_ref[...] * _silu(gate)


def _memattn_body(q, gate, kv_ref):
    qb = q.astype(BF16)
    scale = MEM_HEAD_DIM ** -0.5 * LOG2E
    outs = []
    for h in range(MEM_HEADS):
        cols = slice(h * MEM_HEAD_DIM, (h + 1) * MEM_HEAD_DIM)
        k = kv_ref[:, h * MEM_HEAD_DIM:(h + 1) * MEM_HEAD_DIM]
        v = kv_ref[:, GROUP_WIDTH + h * MEM_HEAD_DIM:GROUP_WIDTH + (h + 1) * MEM_HEAD_DIM]
        s = _dot_nt(qb[:, cols], k) * scale
        e = jnp.exp2(s - jnp.max(s, axis=-1, keepdims=True))
        outs.append(_dot(e.astype(BF16), v) / jnp.sum(e, axis=-1, keepdims=True))
    return jnp.concatenate(outs, axis=1) * _silu(gate)


def _tail_kernel(cx_ref, cg_ref, mq_ref, mg_ref, ya_ref, yb_ref, x_ref, kv_ref,
                 cw_ref, cb_ref, wg_ref, bg_ref, lam_ref, w_ref, fg_ref, o_ref,
                 xe_ref, a_ref, b_ref, h_ref, carry_ref, *, chunk, n_sub, final):
    c = pl.program_id(1)

    @pl.when(c == 0)
    def _():
        xe_ref[0:SUBLANES, :] = jnp.zeros((SUBLANES, GROUP_WIDTH), F32)
        carry_ref[...] = jnp.zeros(carry_ref.shape, F32)

    @pl.when(c > 0)
    def _():
        xe_ref[0:SUBLANES, :] = xe_ref[chunk:chunk + SUBLANES, :]

    gw = GROUP_WIDTH
    sub = chunk // n_sub

    def front(h):
        rows = pl.ds(h * sub, sub)
        xc, z = _lru_gates(cx_ref[rows, :], cw_ref, cb_ref, wg_ref, bg_ref, xe_ref, h * sub, sub)
        acc = (x_ref[rows, :] + _dot(ya_ref[rows, :], w_ref[0:gw, :])
               + _dot(yb_ref[rows, :], w_ref[gw:2 * gw, :]))
        y_m = _memattn_body(mq_ref[rows, :], mg_ref[rows, :], kv_ref)
        return xc, z, acc + _dot(y_m.astype(BF16), w_ref[3 * gw:4 * gw, :])

    fronts = [front(h) for h in range(n_sub)]
    for h, (xc, z, acc) in enumerate(fronts):
        rows = pl.ds(h * sub, sub)
        y_c = _lru_scan(xc, z, cg_ref[rows, :], lam_ref, a_ref.at[rows], b_ref.at[rows],
                        h_ref.at[rows], carry_ref, sub)
        acc = acc + _dot(y_c.astype(BF16), w_ref[2 * gw:3 * gw, :])
        o_ref[rows, :] = _rms(acc, fg_ref[...]) if final else acc


def _tail(rest, y_a, y_b, x, kv, cw, cb, wg, bg, lam, w_all, fg, layer, final, B, L, M, chunk):
    T, D = x.shape
    nc = L // chunk
    gw = GROUP_WIDTH

    def row_blk(b, c):
        return (b * nc + c, 0)

    def rest_spec(col):
        return pl.BlockSpec((chunk, gw), lambda b, c: (b * nc + c, col))

    def per_layer(shape):
        return pl.BlockSpec((None,) + shape, lambda b, c: (layer,) + (0,) * len(shape))

    return pl.pallas_call(
        functools.partial(_tail_kernel, chunk=chunk, n_sub=TAIL_ROW_GROUPS, final=final),
        grid=(B, nc),
        in_specs=[
            rest_spec(COL_CX), rest_spec(COL_CG), rest_spec(COL_MQ), rest_spec(COL_MG),
            pl.BlockSpec((chunk, gw), row_blk),
            pl.BlockSpec((chunk, gw), row_blk),
            pl.BlockSpec((chunk, D), row_blk),
            pl.BlockSpec((None, M, 2 * gw), lambda b, c: (layer, b, 0)),
            per_layer((4, gw)), per_layer((1, gw)), per_layer((gw, 2 * gw)), per_layer((1, 2 * gw)),
            per_layer((1, gw)), per_layer(w_all.shape[1:]),
            pl.BlockSpec((1, D), lambda b, c: (0, 0)),
        ],
        out_specs=pl.BlockSpec((chunk, D), row_blk),
        out_shape=jax.ShapeDtypeStruct((T, D), F32),
        scratch_shapes=[
            pltpu.VMEM((chunk + SUBLANES, gw), F32),
            pltpu.VMEM((chunk, gw), F32),
            pltpu.VMEM((chunk, gw), F32),
            pltpu.VMEM((chunk, gw), F32),
            pltpu.VMEM((SUBLANES, gw), F32),
        ],
        compiler_params=_params("arbitrary", "arbitrary"),
        name="tail",
    )(rest, rest, rest, rest, y_a, y_b, x, kv, cw, cb, wg, bg, lam, w_all, fg)


def _block_diag(blocks):
    n = blocks.shape[-3]
    eye = jnp.eye(n, dtype=blocks.dtype)
    out = blocks[..., :, :, None, :] * eye[:, None, :, None]
    return out.reshape(blocks.shape[:-3] + (n * blocks.shape[-2], n * blocks.shape[-1]))


def _s5_prep(lam_re, lam_im, log_dt, b_re, b_im, c_re, c_im):
    depth = lam_re.shape[0]
    dt = jnp.exp(log_dt)[..., None]
    mag = jnp.exp(lam_re * dt)
    abar_re = mag * jnp.cos(lam_im * dt)
    abar_im = mag * jnp.sin(lam_im * dt)
    den = lam_re * lam_re + lam_im * lam_im
    nr, ni = abar_re - 1.0, abar_im
    f_re = (nr * lam_re + ni * lam_im) / den
    f_im = (ni * lam_re - nr * lam_im) / den
    bb_re = f_re[..., None] * b_re - f_im[..., None] * b_im
    bb_im = f_re[..., None] * b_im + f_im[..., None] * b_re
    gpb = LANES // S5_CH

    def in_blocks(bb):
        return _block_diag(jnp.swapaxes(bb, -1, -2).reshape(depth, S5_BLOCKS, gpb, S5_CH, S5_STATE))

    def out_blocks(cc):
        return _block_diag(jnp.swapaxes(cc, -1, -2).reshape(depth, S5_BLOCKS, gpb, S5_STATE, S5_CH))

    pr, pi = jnp.ones_like(abar_re), jnp.zeros_like(abar_im)
    wb_d, powers = [], []
    for _ in range(S5_UNROLL):
        wd_re = pr[..., None] * bb_re - pi[..., None] * bb_im
        wd_im = pr[..., None] * bb_im + pi[..., None] * bb_re
        wb_d.append(jnp.concatenate([in_blocks(wd_re), in_blocks(wd_im)], axis=-1))
        pr, pi = pr * abar_re - pi * abar_im, pr * abar_im + pi * abar_re
        powers.append((pr.reshape(depth, 1, -1), pi.reshape(depth, 1, -1)))
    wb = jnp.concatenate(wb_d, axis=-2).astype(BF16)
    wc = jnp.concatenate([out_blocks(c_re), out_blocks(-c_im)], axis=-2).astype(BF16)

    zero = jnp.zeros_like(powers[0][0])
    pad = [zero] * (SUBLANES - S5_UNROLL)
    carry_re = jnp.concatenate([p[0] for p in powers] + pad, axis=1)
    carry_im = jnp.concatenate([p[1] for p in powers] + pad, axis=1)
    step_re = jnp.concatenate([zero] * S5_UNROLL + [powers[-1][0]] * (SUBLANES - S5_UNROLL), axis=1)
    step_im = jnp.concatenate([zero] * S5_UNROLL + [powers[-1][1]] * (SUBLANES - S5_UNROLL), axis=1)
    coef = jnp.stack([carry_re, carry_im, step_re, step_im], axis=1)
    return wb, wc, coef.astype(F32)


def _rope_tables(positions):
    inv_freq = ROPE_THETA ** (-jnp.arange(0, ROPE_DIMS, 2, dtype=F32) / ROPE_DIMS)
    ang = positions.astype(F32).reshape(-1, 1) * inv_freq
    cos, sin = jnp.cos(ang), jnp.sin(ang)
    half = ROPE_DIMS // 2
    pad = DA_HEAD_DIM - ROPE_DIMS
    n = cos.shape[0]
    ones, zeros = jnp.ones((n, pad), F32), jnp.zeros((n, pad), F32)
    zh = jnp.zeros((n, half), F32)
    rep = LANES // DA_HEAD_DIM
    rc = jnp.concatenate([cos, cos, ones] * rep, axis=1)
    rs1 = jnp.concatenate([-sin, zh, zeros] * rep, axis=1)
    rs2 = jnp.concatenate([zh, sin, zeros] * rep, axis=1)
    return rc, rs1, rs2


def _tiles(L):
    return dict(chunk=min(512, L), tq=min(512, L), tk=min(512, L))


def kernel(x, mem, positions, norm_g, w_in, w_out, s5_lambda_re, s5_lambda_im, s5_log_dt, s5_b_re, s5_b_im, s5_c_re, s5_c_im, s5_d, s5_w_glu, da_lambda_q1, da_lambda_k1, da_lambda_q2, da_lambda_k2, da_subln_g, lru_conv_w, lru_conv_b, lru_w_a, lru_b_a, lru_w_x, lru_b_x, lru_lambda, mem_norm_g, w_mem_kv, final_norm_g):
    B, L, D = x.shape
    M = mem.shape[1]
    depth = w_in.shape[0]
    T = B * L
    t = _tiles(L)
    assert L % t["tk"] == 0 and L % t["chunk"] == 0 and t["tk"] % t["chunk"] == 0
    assert t["chunk"] % (max(S5_ROW_GROUPS, TAIL_ROW_GROUPS) * LANES) == 0
    assert SUBLANES == 2 * S5_UNROLL and GROUP_WIDTH % (ATTN_STREAMS * LANES) == 0

    gw = GROUP_WIDTH
    w_in_b = w_in.astype(BF16)
    w_out_b = w_out.astype(BF16)
    wglu_b = s5_w_glu.astype(BF16)
    wb, wc, coef = _s5_prep(s5_lambda_re, s5_lambda_im, s5_log_dt, s5_b_re, s5_b_im, s5_c_re, s5_c_im)
    s5_d2 = s5_d.reshape(depth, 1, gw)
    lamv = jnp.stack([da_lambda_q1, da_lambda_k1, da_lambda_q2, da_lambda_k2], axis=1)
    sg = da_subln_g.reshape(depth, 1, LANES)
    wg = jnp.concatenate([_block_diag(lru_w_a), _block_diag(lru_w_x)], axis=-1).astype(BF16)
    bg = jnp.concatenate([lru_b_a, lru_b_x], axis=-1).reshape(depth, 1, 2 * gw)
    lru_cb = lru_conv_b.reshape(depth, 1, gw)
    lru_lam = lru_lambda.reshape(depth, 1, gw)
    norm_g3 = norm_g.reshape(depth, 1, D)
    rc, rs1, rs2 = _rope_tables(positions)

    kv = _memkv(mem.reshape(B * M, D), mem_norm_g.reshape(depth, 1, D), w_mem_kv.astype(BF16), B, M)

    xf = x.reshape(T, D)
    fg = final_norm_g.reshape(1, D)
    for layer in range(depth):
        rest, qkv, y_a, vt = _inproj_s5(xf, norm_g3, w_in_b, layer, rc, rs1, rs2,
                                        wb, coef, wc, s5_d2, wglu_b, B, L, t["chunk"], t["tk"])
        y_b = _attn(qkv, vt, rest, lamv, sg, layer, B, L, t["tq"], t["tk"])
        xf = _tail(rest, y_a, y_b, xf, kv, lru_conv_w, lru_cb, wg, bg, lru_lam, w_out_b, fg,
                   layer, layer == depth - 1, B, L, M, t["chunk"])
    return xf.reshape(B, L, D)
```

```python
import functools
import math

import jax
import jax.numpy as jnp
from jax import lax
from jax.experimental import pallas as pl
from jax.experimental.pallas import tpu as pltpu

F32 = jnp.float32
BF16 = jnp.bfloat16

EPS = 1e-6
GROUP_WIDTH = 512
LANES = 128
SUBLANES = 8
S5_STATE = 64
S5_CH = 16
S5_BLOCKS = 4
S5_BLOCK_STATES = 512
S5_UNROLL = 4
S5_ROW_GROUPS = 2
TAIL_ROW_GROUPS = 1
ATTN_UNROLL = 2
ATTN_STREAMS = 4
DA_HEADS = 4
DA_HEAD_DIM = 64
ROPE_DIMS = 16
ROPE_THETA = 500000.0
LRU_BLOCKS = 8
LRU_C = 8.0
MEM_HEADS = 4
MEM_HEAD_DIM = 128
NEG_BIG = -1e30
LOG2E = math.log2(math.e)
Q_SCALE = DA_HEAD_DIM ** -0.5 * LOG2E
VMEM_LIMIT_BYTES = 56 * 1024 * 1024

W_COL_AU = 0
W_COL_Q = 2 * GROUP_WIDTH
W_COL_REST = 5 * GROUP_WIDTH
COL_BG, COL_CX, COL_CG, COL_MQ, COL_MG = range(5)
N_REST = 5 * GROUP_WIDTH


def _params(*sem):
    return pltpu.CompilerParams(dimension_semantics=sem, vmem_limit_bytes=VMEM_LIMIT_BYTES)


def _sigmoid(x):
    return 0.5 * jnp.tanh(0.5 * x) + 0.5


def _silu(x):
    return x * _sigmoid(x)


def _gelu_tanh(x):
    return 0.5 * x * (1.0 + jnp.tanh(math.sqrt(2.0 / math.pi) * (x + 0.044715 * (x * x * x))))


def _dot(a, b):
    return jnp.dot(a, b, preferred_element_type=F32)


def _dot_nt(a, b):
    return lax.dot_general(a, b, (((1,), (1,)), ((), ())), preferred_element_type=F32)


def _rms(x, g):
    ms = jnp.mean(x * x, axis=-1, keepdims=True)
    return x * lax.rsqrt(ms + EPS) * g


def _s5_input(u, wb_ref, us_ref, xs_ref, chunk):
    us_ref[0:SUBLANES, :] = jnp.zeros((SUBLANES, GROUP_WIDTH), F32)
    us_ref[SUBLANES:SUBLANES + chunk, :] = u
    row_in_tile = lax.broadcasted_iota(jnp.int32, (chunk, LANES), 0) % SUBLANES
    blk = 2 * S5_BLOCK_STATES
    for q in range(S5_BLOCKS):
        cols = slice(q * LANES, (q + 1) * LANES)
        parts = [u[:, cols].astype(BF16)]
        for d in range(1, S5_UNROLL):
            shifted = us_ref[SUBLANES - d:SUBLANES - d + chunk, cols]
            parts.append(jnp.where(row_in_tile >= d, shifted, 0.0).astype(BF16))
        xs_ref[:, q * blk:(q + 1) * blk] = _dot(jnp.concatenate(parts, axis=1), wb_ref[q])


def _s5_scan_out(u, gate, coef_ref, wc_ref, d_ref, wglu_ref, xs_ref, carry_ref, chunk):
    blk = 2 * S5_BLOCK_STATES

    def cmul_add(br, bi, ar, ai, sr, si):
        return br + (ar * sr - ai * si), bi + (ar * si + ai * sr)

    for r in range(chunk // SUBLANES):
        rows = slice(r * SUBLANES, (r + 1) * SUBLANES)
        for q in range(S5_BLOCKS):
            for j in range(S5_BLOCK_STATES // LANES):
                lr = q * blk + j * LANES
                li = lr + S5_BLOCK_STATES
                lc = q * S5_BLOCK_STATES + j * LANES
                xr, xi = cmul_add(xs_ref[rows, lr:lr + LANES], xs_ref[rows, li:li + LANES],
                                  coef_ref[0, :, lc:lc + LANES], coef_ref[1, :, lc:lc + LANES],
                                  carry_ref[:, lr:lr + LANES], carry_ref[:, li:li + LANES])
                xr, xi = cmul_add(xr, xi, coef_ref[2, :, lc:lc + LANES], coef_ref[3, :, lc:lc + LANES],
                                  pltpu.roll(xr, S5_UNROLL, axis=0), pltpu.roll(xi, S5_UNROLL, axis=0))
                xs_ref[rows, lr:lr + LANES] = xr
                xs_ref[rows, li:li + LANES] = xi
                carry_ref[:, lr:lr + LANES] = jnp.broadcast_to(xr[SUBLANES - 1:, :], (SUBLANES, LANES))
                carry_ref[:, li:li + LANES] = jnp.broadcast_to(xi[SUBLANES - 1:, :], (SUBLANES, LANES))

    ys = []
    for q in range(S5_BLOCKS):
        xb = xs_ref[:, q * blk:(q + 1) * blk].astype(BF16)
        ys.append(_dot(xb, wc_ref[q]))
    y = jnp.concatenate(ys, axis=1) + d_ref[...] * u
    y = _gelu_tanh(y)
    z = _dot(y.astype(BF16), wglu_ref[...])
    return z[:, :GROUP_WIDTH] * _sigmoid(z[:, GROUP_WIDTH:]) * _silu(gate)


def _inproj_s5_kernel(x_ref, g_ref, w_ref, rc_ref, rs1_ref, rs2_ref,
                      wb_ref, coef_ref, wc_ref, d_ref, wglu_ref,
                      rest_ref, qkv_ref, ya_ref, vt_ref, us_ref, xs_ref, carry_ref, *, chunk, n_sub):
    @pl.when(pl.program_id(1) == 0)
    def _():
        carry_ref[...] = jnp.zeros(carry_ref.shape, F32)

    gw = GROUP_WIDTH
    sub = chunk // n_sub

    def projections(h):
        rows = pl.ds(h * sub, sub)
        hb = _rms(x_ref[rows, :], g_ref[...]).astype(BF16)
        u = _dot(hb, w_ref[:, W_COL_AU:W_COL_AU + gw])
        _s5_input(u, wb_ref, us_ref.at[h], xs_ref.at[rows], sub)
        gate = _dot(hb, w_ref[:, W_COL_AU + gw:W_COL_AU + 2 * gw])
        rest_ref[rows, :] = _dot(hb, w_ref[:, W_COL_REST:W_COL_REST + N_REST])
        qk = _dot(hb, w_ref[:, W_COL_Q:W_COL_Q + 2 * gw])
        c, s1, s2 = rc_ref[rows, :], rs1_ref[rows, :], rs2_ref[rows, :]
        for j in range(2 * gw // LANES):
            t = qk[:, j * LANES:(j + 1) * LANES]
            t = (t * c + pltpu.roll(t, LANES - ROPE_DIMS // 2, axis=1) * s1
                 + pltpu.roll(t, ROPE_DIMS // 2, axis=1) * s2)
            if j < gw // LANES:
                t = t * Q_SCALE
            qkv_ref[rows, j * LANES:(j + 1) * LANES] = t.astype(BF16)
        v = _dot(hb, w_ref[:, W_COL_Q + 2 * gw:W_COL_Q + 3 * gw])
        qkv_ref[rows, 2 * gw:] = v.astype(BF16)
        vt_ref[:, h * sub:(h + 1) * sub] = v.T.astype(BF16)
        return u, gate

    fronts = [projections(h) for h in range(n_sub)]
    for h, (u, gate) in enumerate(fronts):
        rows = pl.ds(h * sub, sub)
        ya_ref[rows, :] = _s5_scan_out(u, gate, coef_ref, wc_ref, d_ref, wglu_ref,
                                       xs_ref.at[rows], carry_ref, sub).astype(BF16)


def _inproj_s5(x, g, w_all, layer, rc, rs1, rs2, wb, coef, wc, d, wglu, B, L, chunk, tk):
    T, D = x.shape
    n_in = w_all.shape[-1]
    nc = L // chunk
    per_kv = tk // chunk
    n_state = S5_BLOCKS * 2 * S5_BLOCK_STATES

    def row_blk(b, c):
        return (b * nc + c, 0)

    return pl.pallas_call(
        functools.partial(_inproj_s5_kernel, chunk=chunk, n_sub=S5_ROW_GROUPS),
        grid=(B, nc),
        in_specs=[
            pl.BlockSpec((chunk, D), row_blk),
            pl.BlockSpec((None, 1, D), lambda b, c: (layer, 0, 0)),
            pl.BlockSpec((None, D, n_in), lambda b, c: (layer, 0, 0), pipeline_mode=pl.Buffered(1)),
            pl.BlockSpec((chunk, LANES), row_blk),
            pl.BlockSpec((chunk, LANES), row_blk),
            pl.BlockSpec((chunk, LANES), row_blk),
            pl.BlockSpec((None,) + wb.shape[1:], lambda b, c: (layer, 0, 0, 0)),
            pl.BlockSpec((None,) + coef.shape[1:], lambda b, c: (layer, 0, 0, 0)),
            pl.BlockSpec((None,) + wc.shape[1:], lambda b, c: (layer, 0, 0, 0)),
            pl.BlockSpec((None, 1, GROUP_WIDTH), lambda b, c: (layer, 0, 0)),
            pl.BlockSpec((None,) + wglu.shape[1:], lambda b, c: (layer, 0, 0)),
        ],
        out_specs=[
            pl.BlockSpec((chunk, N_REST), row_blk),
            pl.BlockSpec((chunk, 3 * GROUP_WIDTH), row_blk),
            pl.BlockSpec((chunk, GROUP_WIDTH), row_blk),
            pl.BlockSpec((None, None, GROUP_WIDTH, chunk), lambda b, c: (b, c // per_kv, 0, c % per_kv)),
        ],
        out_shape=[jax.ShapeDtypeStruct((T, N_REST), F32),
                   jax.ShapeDtypeStruct((T, 3 * GROUP_WIDTH), BF16),
                   jax.ShapeDtypeStruct((T, GROUP_WIDTH), BF16),
                   jax.ShapeDtypeStruct((B, L // tk, GROUP_WIDTH, tk), BF16)],
        scratch_shapes=[pltpu.VMEM((S5_ROW_GROUPS, chunk // S5_ROW_GROUPS + SUBLANES, GROUP_WIDTH), F32),
                        pltpu.VMEM((chunk, n_state), F32),
                        pltpu.VMEM((SUBLANES, n_state), F32)],
        compiler_params=_params("arbitrary", "arbitrary"),
        name="inproj_s5",
    )(x, g, w_all, rc, rs1, rs2, wb, coef, wc, d, wglu)


def _attn_kernel(lamv_ref, q_ref, k_ref, vt_ref, g_ref, sg_ref, o_ref,
                 qs_ref, s_ref, smax_ref, m_ref, acc_ref, *, tq, tk, n_streams, lam_init):
    i = pl.program_id(2)
    lane = lax.broadcasted_iota(jnp.int32, (tq, LANES), 1)
    for st in range(n_streams):
        q = q_ref[:, st * LANES:(st + 1) * LANES]
        zero = jnp.zeros_like(q)
        qs_ref[st, 0:tq, :] = jnp.where(lane < DA_HEAD_DIM, q, zero)
        qs_ref[st, tq:, :] = jnp.where(lane >= DA_HEAD_DIM, q, zero)
    m_ref[...] = jnp.full(m_ref.shape, NEG_BIG, F32)
    acc_ref[...] = jnp.zeros(acc_ref.shape, F32)
    ones = jnp.ones((2 * SUBLANES, tk), BF16)

    def scores(st, kb, masked=False):
        k = k_ref[pl.ds(pl.multiple_of(kb * tk, tk), tk), st * LANES:(st + 1) * LANES]
        s = _dot_nt(k, qs_ref[st])
        if masked:
            key = lax.broadcasted_iota(jnp.int32, s.shape, 0)
            col = lax.broadcasted_iota(jnp.int32, s.shape, 1)
            qpos = i * tq + jnp.where(col >= tq, col - tq, col)
            s = jnp.where(kb * tk + key <= qpos, s, NEG_BIG)
        s_ref[st] = s
        smax_ref[st] = jnp.broadcast_to(jnp.max(s, axis=0, keepdims=True), smax_ref.shape[1:])

    def softmax_pv(st, kb):
        vt = jnp.concatenate([vt_ref[kb, st * LANES:(st + 1) * LANES, :], ones], axis=0)
        m_prev = m_ref[st]
        m_new = jnp.maximum(m_prev, smax_ref[st])
        alpha = jnp.exp2(m_prev - m_new)
        p = jnp.exp2(s_ref[st] - m_new[0:1, :])
        acc_ref[st] = alpha[0:1, :] * acc_ref[st] + _dot(vt, p.astype(BF16))
        m_ref[st] = m_new

    last = n_streams - 1
    scores(0, i, masked=True)
    for st in range(n_streams):
        if st < last:
            scores(st + 1, i, masked=True)
        else:
            scores(0, 0)
        softmax_pv(st, i)

    def step(kb):
        for st in range(n_streams):
            if st < last:
                scores(st + 1, kb)
            else:
                scores(0, kb + 1)
            softmax_pv(st, kb)

    def unrolled_steps(jj, carry):
        for n in range(ATTN_UNROLL):
            step(ATTN_UNROLL * jj + n)
        return carry

    def single_step(kb, carry):
        step(kb)
        return carry

    n_main = lax.div(i, ATTN_UNROLL)
    lax.fori_loop(0, n_main, unrolled_steps, 0)
    lax.fori_loop(n_main * ATTN_UNROLL, i, single_step, 0)

    lv = lamv_ref[...]
    lam = (jnp.exp(jnp.sum(lv[0:1, :] * lv[1:2, :], axis=-1, keepdims=True))
           - jnp.exp(jnp.sum(lv[2:3, :] * lv[3:4, :], axis=-1, keepdims=True)) + lam_init)
    for st in range(n_streams):
        acc = acc_ref[st]
        inv_l = 1.0 / acc[LANES:LANES + 1, :]
        ot = acc[:LANES, :tq] * inv_l[:, :tq] - lam * (acc[:LANES, tq:] * inv_l[:, tq:])
        o = _rms(ot.T, sg_ref[...]) * (1.0 - lam_init)
        cols = slice(st * LANES, (st + 1) * LANES)
        o_ref[:, cols] = (o * _silu(g_ref[:, cols])).astype(BF16)


def _attn(qkv, vt, rest, lamv, sg, layer, B, L, tq, tk):
    T = B * L
    nq = L // tq
    assert tq == tk, "the causal block bookkeeping assumes square score blocks"
    lam_init = 0.8 - 0.6 * math.exp(-0.3 * layer)
    ns = ATTN_STREAMS
    width = ns * LANES
    n_grp = GROUP_WIDTH // width
    gate_blk = COL_BG * n_grp
    resident = pl.Buffered(1)
    return pl.pallas_call(
        functools.partial(_attn_kernel, tq=tq, tk=tk, n_streams=ns, lam_init=lam_init),
        grid=(B, n_grp, nq),
        in_specs=[
            pl.BlockSpec((None, 4, DA_HEAD_DIM), lambda b, h, i: (layer, 0, 0)),
            pl.BlockSpec((tq, width), lambda b, h, i: (b * nq + i, h)),
            pl.BlockSpec((L, width), lambda b, h, i: (b, n_grp + h), pipeline_mode=resident),
            pl.BlockSpec((None, L // tk, width, tk), lambda b, h, i: (b, 0, h, 0), pipeline_mode=resident),
            pl.BlockSpec((tq, width), lambda b, h, i: (b * nq + i, gate_blk + h)),
            pl.BlockSpec((None, 1, LANES), lambda b, h, i: (layer, 0, 0)),
        ],
        out_specs=pl.BlockSpec((tq, width), lambda b, h, i: (b * nq + i, h)),
        out_shape=jax.ShapeDtypeStruct((T, GROUP_WIDTH), BF16),
        scratch_shapes=[
            pltpu.VMEM((ns, 2 * tq, LANES), BF16),
            pltpu.VMEM((ns, tk, 2 * tq), F32),
            pltpu.VMEM((ns, SUBLANES, 2 * tq), F32),
            pltpu.VMEM((ns, SUBLANES, 2 * tq), F32),
            pltpu.VMEM((ns, LANES + 2 * SUBLANES, 2 * tq), F32),
        ],
        compiler_params=_params("parallel", "parallel", "arbitrary"),
        name="diffattn",
    )(lamv, qkv, qkv, vt, rest, sg)


def _memkv_kernel(mem_ref, g_ref, w_ref, o_ref):
    o_ref[...] = _dot(_rms(mem_ref[...], g_ref[...]).astype(BF16), w_ref[...]).astype(BF16)


def _memkv(mem2d, g, w, B, M):
    depth, D, n_kv = w.shape
    return pl.pallas_call(
        _memkv_kernel,
        grid=(depth, B),
        in_specs=[
            pl.BlockSpec((M, D), lambda l, b: (b, 0)),
            pl.BlockSpec((None, 1, D), lambda l, b: (l, 0, 0)),
            pl.BlockSpec((None, D, n_kv), lambda l, b: (l, 0, 0)),
        ],
        out_specs=pl.BlockSpec((None, M, n_kv), lambda l, b: (l, b, 0)),
        out_shape=jax.ShapeDtypeStruct((depth, B * M, n_kv), BF16),
        compiler_params=_params("parallel", "parallel"),
        name="memkv",
    )(mem2d, g, w)


def _lru_gates(x, cw_ref, cb_ref, wg_ref, bg_ref, xe_ref, row0, n_rows):
    base = SUBLANES + row0
    xe_ref[base:base + n_rows, :] = x
    xc = (cw_ref[3:4, :] * x
          + cw_ref[2:3, :] * xe_ref[base - 1:base - 1 + n_rows, :]
          + cw_ref[1:2, :] * xe_ref[base - 2:base - 2 + n_rows, :]
          + cw_ref[0:1, :] * xe_ref[base - 3:base - 3 + n_rows, :]
          + cb_ref[...])
    return xc, _dot(xc.astype(BF16), wg_ref[...]) + bg_ref[...]


def _lru_scan(xc, z, gate, lam_ref, a_ref, b_ref, h_ref, carry_ref, chunk):
    r = _sigmoid(z[:, :GROUP_WIDTH])
    gate_i = _sigmoid(z[:, GROUP_WIDTH:])
    nl = -lam_ref[...]
    softplus = jnp.maximum(nl, 0.0) + jnp.log1p(jnp.exp(-jnp.abs(nl)))
    log_a = (-LRU_C) * r * softplus
    a = jnp.exp(log_a)
    a_ref[...] = a
    mult = jnp.sqrt(-jnp.tanh(log_a) * (1.0 + a * a))
    b_ref[...] = mult * (gate_i * xc)

    row = lax.broadcasted_iota(jnp.int32, (SUBLANES, LANES), 0)
    for t in range(chunk // SUBLANES):
        rows = slice(t * SUBLANES, (t + 1) * SUBLANES)
        for j in range(GROUP_WIDTH // LANES):
            cols = slice(j * LANES, (j + 1) * LANES)
            a = a_ref[rows, cols]
            b = b_ref[rows, cols] + jnp.where(row == 0, a, 0.0) * carry_ref[:, cols]
            w = jnp.where(row == 0, 0.0, a)
            for shift in (1, 2, 4):
                b = b + w * pltpu.roll(b, shift, axis=0)
                if shift != 4:
                    w = w * pltpu.roll(w, shift, axis=0)
            h_ref[rows, cols] = b
            carry_ref[:, cols] = jnp.broadcast_to(b[SUBLANES - 1:, :], (SUBLANES, LANES))
    return h_ref[...] * _silu(gate)


def _memattn_body(q, gate, kv_ref):
    qb = q.astype(BF16)
    scale = MEM_HEAD_DIM ** -0.5 * LOG2E
    outs = []
    for h in range(MEM_HEADS):
        cols = slice(h * MEM_HEAD_DIM, (h + 1) * MEM_HEAD_DIM)
        k = kv_ref[:, h * MEM_HEAD_DIM:(h + 1) * MEM_HEAD_DIM]
        v = kv_ref[:, GROUP_WIDTH + h * MEM_HEAD_DIM:GROUP_WIDTH + (h + 1) * MEM_HEAD_DIM]
        s = _dot_nt(qb[:, cols], k) * scale
        e = jnp.exp2(s - jnp.max(s, axis=-1, keepdims=True))
        outs.append(_dot(e.astype(BF16), v) / jnp.sum(e, axis=-1, keepdims=True))
    return jnp.concatenate(outs, axis=1) * _silu(gate)


def _tail_kernel(cx_ref, cg_ref, mq_ref, mg_ref, ya_ref, yb_ref, x_ref, kv_ref,
                 cw_ref, cb_ref, wg_ref, bg_ref, lam_ref, w_ref, fg_ref, o_ref,
                 xe_ref, a_ref, b_ref, h_ref, carry_ref, *, chunk, n_sub, final):
    c = pl.program_id(1)

    @pl.when(c == 0)
    def _():
        xe_ref[0:SUBLANES, :] = jnp.zeros((SUBLANES, GROUP_WIDTH), F32)
        carry_ref[...] = jnp.zeros(carry_ref.shape, F32)

    @pl.when(c > 0)
    def _():
        xe_ref[0:SUBLANES, :] = xe_ref[chunk:chunk + SUBLANES, :]

    gw = GROUP_WIDTH
    sub = chunk // n_sub

    def front(h):
        rows = pl.ds(h * sub, sub)
        xc, z = _lru_gates(cx_ref[rows, :], cw_ref, cb_ref, wg_ref, bg_ref, xe_ref, h * sub, sub)
        acc = (x_ref[rows, :] + _dot(ya_ref[rows, :], w_ref[0:gw, :])
               + _dot(yb_ref[rows, :], w_ref[gw:2 * gw, :]))
        y_m = _memattn_body(mq_ref[rows, :], mg_ref[rows, :], kv_ref)
        return xc, z, acc + _dot(y_m.astype(BF16), w_ref[3 * gw:4 * gw, :])

    fronts = [front(h) for h in range(n_sub)]
    for h, (xc, z, acc) in enumerate(fronts):
        rows = pl.ds(h * sub, sub)
        y_c = _lru_scan(xc, z, cg_ref[rows, :], lam_ref, a_ref.at[rows], b_ref.at[rows],
                        h_ref.at[rows], carry_ref, sub)
        acc = acc + _dot(y_c.astype(BF16), w_ref[2 * gw:3 * gw, :])
        o_ref[rows, :] = _rms(acc, fg_ref[...]) if final else acc


def _tail(rest, y_a, y_b, x, kv, cw, cb, wg, bg, lam, w_all, fg, layer, final, B, L, M, chunk):
    T, D = x.shape
    nc = L // chunk
    gw = GROUP_WIDTH

    def row_blk(b, c):
        return (b * nc + c, 0)

    def rest_spec(col):
        return pl.BlockSpec((chunk, gw), lambda b, c: (b * nc + c, col))

    def per_layer(shape):
        return pl.BlockSpec((None,) + shape, lambda b, c: (layer,) + (0,) * len(shape))

    return pl.pallas_call(
        functools.partial(_tail_kernel, chunk=chunk, n_sub=TAIL_ROW_GROUPS, final=final),
        grid=(B, nc),
        in_specs=[
            rest_spec(COL_CX), rest_spec(COL_CG), rest_spec(COL_MQ), rest_spec(COL_MG),
            pl.BlockSpec((chunk, gw), row_blk),
            pl.BlockSpec((chunk, gw), row_blk),
            pl.BlockSpec((chunk, D), row_blk),
            pl.BlockSpec((None, M, 2 * gw), lambda b, c: (layer, b, 0)),
            per_layer((4, gw)), per_layer((1, gw)), per_layer((gw, 2 * gw)), per_layer((1, 2 * gw)),
            per_layer((1, gw)), per_layer(w_all.shape[1:]),
            pl.BlockSpec((1, D), lambda b, c: (0, 0)),
        ],
        out_specs=pl.BlockSpec((chunk, D), row_blk),
        out_shape=jax.ShapeDtypeStruct((T, D), F32),
        scratch_shapes=[
            pltpu.VMEM((chunk + SUBLANES, gw), F32),
            pltpu.VMEM((chunk, gw), F32),
            pltpu.VMEM((chunk, gw), F32),
            pltpu.VMEM((chunk, gw), F32),
            pltpu.VMEM((SUBLANES, gw), F32),
        ],
        compiler_params=_params("arbitrary", "arbitrary"),
        name="tail",
    )(rest, rest, rest, rest, y_a, y_b, x, kv, cw, cb, wg, bg, lam, w_all, fg)


def _block_diag(blocks):
    n = blocks.shape[-3]
    eye = jnp.eye(n, dtype=blocks.dtype)
    out = jnp.einsum('...nrc,nm->...nrmc', blocks, eye)
    return out.reshape(blocks.shape[:-3] + (n * blocks.shape[-2], n * blocks.shape[-1]))


def _s5_prep(lam_re, lam_im, log_dt, b_re, b_im, c_re, c_im):
    depth = lam_re.shape[0]
    dt = jnp.exp(log_dt)[..., None]
    mag = jnp.exp(lam_re * dt)
    abar_re = mag * jnp.cos(lam_im * dt)
    abar_im = mag * jnp.sin(lam_im * dt)
    den = lam_re * lam_re + lam_im * lam_im
    nr, ni = abar_re - 1.0, abar_im
    f_re = (nr * lam_re + ni * lam_im) / den
    f_im = (ni * lam_re - nr * lam_im) / den
    bb_re = f_re[..., None] * b_re - f_im[..., None] * b_im
    bb_im = f_re[..., None] * b_im + f_im[..., None] * b_re
    gpb = LANES // S5_CH

    def in_blocks(bb):
        return _block_diag(jnp.swapaxes(bb, -1, -2).reshape(depth, S5_BLOCKS, gpb, S5_CH, S5_STATE))

    def out_blocks(cc):
        return _block_diag(jnp.swapaxes(cc, -1, -2).reshape(depth, S5_BLOCKS, gpb, S5_STATE, S5_CH))

    pr, pi = jnp.ones_like(abar_re), jnp.zeros_like(abar_im)
    wb_d, powers = [], []
    for _ in range(S5_UNROLL):
        wd_re = pr[..., None] * bb_re - pi[..., None] * bb_im
        wd_im = pr[..., None] * bb_im + pi[..., None] * bb_re
        wb_d.append(jnp.concatenate([in_blocks(wd_re), in_blocks(wd_im)], axis=-1))
        pr, pi = pr * abar_re - pi * abar_im, pr * abar_im + pi * abar_re
        powers.append((pr.reshape(depth, 1, -1), pi.reshape(depth, 1, -1)))
    wb = jnp.concatenate(wb_d, axis=-2).astype(BF16)
    wc = jnp.concatenate([out_blocks(c_re), out_blocks(-c_im)], axis=-2).astype(BF16)

    zero = jnp.zeros_like(powers[0][0])
    pad = [zero] * (SUBLANES - S5_UNROLL)
    carry_re = jnp.concatenate([p[0] for p in powers] + pad, axis=1)
    carry_im = jnp.concatenate([p[1] for p in powers] + pad, axis=1)
    step_re = jnp.concatenate([zero] * S5_UNROLL + [powers[-1][0]] * (SUBLANES - S5_UNROLL), axis=1)
    step_im = jnp.concatenate([zero] * S5_UNROLL + [powers[-1][1]] * (SUBLANES - S5_UNROLL), axis=1)
    coef = jnp.stack([carry_re, carry_im, step_re, step_im], axis=1)
    return wb, wc, coef.astype(F32)


def _rope_tables(positions):
    inv_freq = ROPE_THETA ** (-jnp.arange(0, ROPE_DIMS, 2, dtype=F32) / ROPE_DIMS)
    ang = positions.astype(F32).reshape(-1, 1) * inv_freq
    cos, sin = jnp.cos(ang), jnp.sin(ang)
    half = ROPE_DIMS // 2
    pad = DA_HEAD_DIM - ROPE_DIMS
    n = cos.shape[0]
    ones, zeros = jnp.ones((n, pad), F32), jnp.zeros((n, pad), F32)
    zh = jnp.zeros((n, half), F32)
    rc = jnp.concatenate([cos, cos, ones], axis=1)
    rs1 = jnp.concatenate([-sin, zh, zeros], axis=1)
    rs2 = jnp.concatenate([zh, sin, zeros], axis=1)
    rep = LANES // DA_HEAD_DIM
    return jnp.tile(rc, (1, rep)), jnp.tile(rs1, (1, rep)), jnp.tile(rs2, (1, rep))


def _tiles(L):
    return dict(chunk=min(512, L), tq=min(512, L), tk=min(512, L))


def kernel(x, mem, positions, norm_g, w_in, w_out, s5_lambda_re, s5_lambda_im, s5_log_dt, s5_b_re, s5_b_im, s5_c_re, s5_c_im, s5_d, s5_w_glu, da_lambda_q1, da_lambda_k1, da_lambda_q2, da_lambda_k2, da_subln_g, lru_conv_w, lru_conv_b, lru_w_a, lru_b_a, lru_w_x, lru_b_x, lru_lambda, mem_norm_g, w_mem_kv, final_norm_g):
    B, L, D = x.shape
    M = mem.shape[1]
    depth = w_in.shape[0]
    T = B * L
    t = _tiles(L)
    assert L % t["tk"] == 0 and L % t["chunk"] == 0 and t["tk"] % t["chunk"] == 0
    assert t["chunk"] % (max(S5_ROW_GROUPS, TAIL_ROW_GROUPS) * LANES) == 0
    assert SUBLANES == 2 * S5_UNROLL and GROUP_WIDTH % (ATTN_STREAMS * LANES) == 0

    gw = GROUP_WIDTH
    w_in_b = w_in.astype(BF16)
    w_out_b = w_out.astype(BF16)
    wglu_b = s5_w_glu.astype(BF16)
    wb, wc, coef = _s5_prep(s5_lambda_re, s5_lambda_im, s5_log_dt, s5_b_re, s5_b_im, s5_c_re, s5_c_im)
    s5_d2 = s5_d.reshape(depth, 1, gw)
    lamv = jnp.stack([da_lambda_q1, da_lambda_k1, da_lambda_q2, da_lambda_k2], axis=1)
    sg = da_subln_g.reshape(depth, 1, LANES)
    wg = jnp.concatenate([_block_diag(lru_w_a), _block_diag(lru_w_x)], axis=-1).astype(BF16)
    bg = jnp.concatenate([lru_b_a, lru_b_x], axis=-1).reshape(depth, 1, 2 * gw)
    lru_cb = lru_conv_b.reshape(depth, 1, gw)
    lru_lam = lru_lambda.reshape(depth, 1, gw)
    norm_g3 = norm_g.reshape(depth, 1, D)
    rc, rs1, rs2 = _rope_tables(positions)

    kv = _memkv(mem.reshape(B * M, D), mem_norm_g.reshape(depth, 1, D), w_mem_kv.astype(BF16), B, M)

    xf = x.reshape(T, D)
    fg = final_norm_g.reshape(1, D)
    for layer in range(depth):
        rest, qkv, y_a, vt = _inproj_s5(xf, norm_g3, w_in_b, layer, rc, rs1, rs2,
                                        wb, coef, wc, s5_d2, wglu_b, B, L, t["chunk"], t["tk"])
        y_b = _attn(qkv, vt, rest, lamv, sg, layer, B, L, t["tq"], t["tk"])
        xf = _tail(rest, y_a, y_b, xf, kv, lru_conv_w, lru_cb, wg, bg, lru_lam, w_out_b, fg,
                   layer, layer == depth - 1, B, L, M, t["chunk"])
    return xf.reshape(B, L, D)
```

```python
import functools
import math

import jax
import jax.numpy as jnp
from jax import lax
from jax.experimental import pallas as pl
from jax.experimental.pallas import tpu as pltpu

F32 = jnp.float32
BF16 = jnp.bfloat16

EPS = 1e-6
GROUP_WIDTH = 512
LANES = 128
SUBLANES = 8
S5_STATE = 64
S5_CH = 16
S5_BLOCKS = 4
S5_BLOCK_STATES = 512
S5_UNROLL = 4
S5_ROW_GROUPS = 2
TAIL_ROW_GROUPS = 1
ATTN_UNROLL = 2
ATTN_STREAMS = 4
DA_HEADS = 4
DA_HEAD_DIM = 64
ROPE_DIMS = 16
ROPE_THETA = 500000.0
LRU_BLOCKS = 8
LRU_C = 8.0
MEM_HEADS = 4
MEM_HEAD_DIM = 128
NEG_BIG = -1e30
LOG2E = math.log2(math.e)
Q_SCALE = DA_HEAD_DIM ** -0.5 * LOG2E
VMEM_LIMIT_BYTES = 56 * 1024 * 1024

W_COL_AU = 0
W_COL_Q = 2 * GROUP_WIDTH
W_COL_REST = 5 * GROUP_WIDTH
COL_BG, COL_CX, COL_CG, COL_MQ, COL_MG = range(5)
N_REST = 5 * GROUP_WIDTH


def _params(*sem):
    return pltpu.CompilerParams(dimension_semantics=sem, vmem_limit_bytes=VMEM_LIMIT_BYTES)


def _sigmoid(x):
    return 0.5 * jnp.tanh(0.5 * x) + 0.5


def _silu(x):
    return x * _sigmoid(x)


def _gelu_tanh(x):
    return 0.5 * x * (1.0 + jnp.tanh(math.sqrt(2.0 / math.pi) * (x + 0.044715 * (x * x * x))))


def _dot(a, b):
    return jnp.dot(a, b, preferred_element_type=F32)


def _dot_nt(a, b):
    return lax.dot_general(a, b, (((1,), (1,)), ((), ())), preferred_element_type=F32)


def _rms(x, g):
    ms = jnp.mean(x * x, axis=-1, keepdims=True)
    return x * lax.rsqrt(ms + EPS) * g


def _s5_input(u, wb_ref, us_ref, xs_ref, chunk):
    us_ref[0:SUBLANES, :] = jnp.zeros((SUBLANES, GROUP_WIDTH), F32)
    us_ref[SUBLANES:SUBLANES + chunk, :] = u
    row_in_tile = lax.broadcasted_iota(jnp.int32, (chunk, LANES), 0) % SUBLANES
    blk = 2 * S5_BLOCK_STATES
    for q in range(S5_BLOCKS):
        cols = slice(q * LANES, (q + 1) * LANES)
        parts = [u[:, cols].astype(BF16)]
        for d in range(1, S5_UNROLL):
            shifted = us_ref[SUBLANES - d:SUBLANES - d + chunk, cols]
            parts.append(jnp.where(row_in_tile >= d, shifted, 0.0).astype(BF16))
        xs_ref[:, q * blk:(q + 1) * blk] = _dot(jnp.concatenate(parts, axis=1), wb_ref[q])


def _s5_scan_out(u, gate, coef_ref, wc_ref, d_ref, wglu_ref, xs_ref, carry_ref, chunk):
    blk = 2 * S5_BLOCK_STATES

    def cmul_add(br, bi, ar, ai, sr, si):
        return br + (ar * sr - ai * si), bi + (ar * si + ai * sr)

    for r in range(chunk // SUBLANES):
        rows = slice(r * SUBLANES, (r + 1) * SUBLANES)
        for q in range(S5_BLOCKS):
            for j in range(S5_BLOCK_STATES // LANES):
                lr = q * blk + j * LANES
                li = lr + S5_BLOCK_STATES
                lc = q * S5_BLOCK_STATES + j * LANES
                xr, xi = cmul_add(xs_ref[rows, lr:lr + LANES], xs_ref[rows, li:li + LANES],
                                  coef_ref[0, :, lc:lc + LANES], coef_ref[1, :, lc:lc + LANES],
                                  carry_ref[:, lr:lr + LANES], carry_ref[:, li:li + LANES])
                xr, xi = cmul_add(xr, xi, coef_ref[2, :, lc:lc + LANES], coef_ref[3, :, lc:lc + LANES],
                                  pltpu.roll(xr, S5_UNROLL, axis=0), pltpu.roll(xi, S5_UNROLL, axis=0))
                xs_ref[rows, lr:lr + LANES] = xr
                xs_ref[rows, li:li + LANES] = xi
                carry_ref[:, lr:lr + LANES] = jnp.broadcast_to(xr[SUBLANES - 1:, :], (SUBLANES, LANES))
                carry_ref[:, li:li + LANES] = jnp.broadcast_to(xi[SUBLANES - 1:, :], (SUBLANES, LANES))

    ys = []
    for q in range(S5_BLOCKS):
        xb = xs_ref[:, q * blk:(q + 1) * blk].astype(BF16)
        ys.append(_dot(xb, wc_ref[q]))
    y = jnp.concatenate(ys, axis=1) + d_ref[...] * u
    y = _gelu_tanh(y)
    z = _dot(y.astype(BF16), wglu_ref[...])
    return z[:, :GROUP_WIDTH] * _sigmoid(z[:, GROUP_WIDTH:]) * _silu(gate)


def _inproj_s5_kernel(x_ref, g_ref, w_ref, rc_ref, rs1_ref, rs2_ref,
                      wb_ref, coef_ref, wc_ref, d_ref, wglu_ref,
                      rest_ref, qk_rot_ref, ya_ref, vt_ref, us_ref, xs_ref, carry_ref, *, chunk, n_sub):
    @pl.when(pl.program_id(1) == 0)
    def _():
        carry_ref[...] = jnp.zeros(carry_ref.shape, F32)

    gw = GROUP_WIDTH
    sub = chunk // n_sub

    def projections(h):
        rows = pl.ds(h * sub, sub)
        hb = _rms(x_ref[rows, :], g_ref[...]).astype(BF16)
        u = _dot(hb, w_ref[:, W_COL_AU:W_COL_AU + gw])
        _s5_input(u, wb_ref, us_ref.at[h], xs_ref.at[rows], sub)
        gate = _dot(hb, w_ref[:, W_COL_AU + gw:W_COL_AU + 2 * gw])
        rest_ref[rows, :] = _dot(hb, w_ref[:, W_COL_REST:W_COL_REST + N_REST])
        qk = _dot(hb, w_ref[:, W_COL_Q:W_COL_Q + 2 * gw])
        c, s1, s2 = rc_ref[rows, :], rs1_ref[rows, :], rs2_ref[rows, :]
        for j in range(2 * gw // LANES):
            t = qk[:, j * LANES:(j + 1) * LANES]
            t = (t * c + pltpu.roll(t, LANES - ROPE_DIMS // 2, axis=1) * s1
                 + pltpu.roll(t, ROPE_DIMS // 2, axis=1) * s2)
            if j < gw // LANES:
                t = t * Q_SCALE
            qk_rot_ref[rows, j * LANES:(j + 1) * LANES] = t.astype(BF16)
        v = _dot(hb, w_ref[:, W_COL_Q + 2 * gw:W_COL_Q + 3 * gw])
        vt_ref[:, h * sub:(h + 1) * sub] = v.T.astype(BF16)
        return u, gate

    fronts = [projections(h) for h in range(n_sub)]
    for h, (u, gate) in enumerate(fronts):
        rows = pl.ds(h * sub, sub)
        ya_ref[rows, :] = _s5_scan_out(u, gate, coef_ref, wc_ref, d_ref, wglu_ref,
                                       xs_ref.at[rows], carry_ref, sub).astype(BF16)


def _inproj_s5(x, g, w_all, layer, rc, rs1, rs2, wb, coef, wc, d, wglu, B, L, chunk, tk):
    T, D = x.shape
    n_in = w_all.shape[-1]
    nc = L // chunk
    per_kv = tk // chunk
    n_state = S5_BLOCKS * 2 * S5_BLOCK_STATES

    def row_blk(b, c):
        return (b * nc + c, 0)

    return pl.pallas_call(
        functools.partial(_inproj_s5_kernel, chunk=chunk, n_sub=S5_ROW_GROUPS),
        grid=(B, nc),
        in_specs=[
            pl.BlockSpec((chunk, D), row_blk),
            pl.BlockSpec((None, 1, D), lambda b, c: (layer, 0, 0)),
            pl.BlockSpec((None, D, n_in), lambda b, c: (layer, 0, 0), pipeline_mode=pl.Buffered(1)),
            pl.BlockSpec((chunk, LANES), row_blk),
            pl.BlockSpec((chunk, LANES), row_blk),
            pl.BlockSpec((chunk, LANES), row_blk),
            pl.BlockSpec((None,) + wb.shape[1:], lambda b, c: (layer, 0, 0, 0)),
            pl.BlockSpec((None,) + coef.shape[1:], lambda b, c: (layer, 0, 0, 0)),
            pl.BlockSpec((None,) + wc.shape[1:], lambda b, c: (layer, 0, 0, 0)),
            pl.BlockSpec((None, 1, GROUP_WIDTH), lambda b, c: (layer, 0, 0)),
            pl.BlockSpec((None,) + wglu.shape[1:], lambda b, c: (layer, 0, 0)),
        ],
        out_specs=[
            pl.BlockSpec((chunk, N_REST), row_blk),
            pl.BlockSpec((chunk, 2 * GROUP_WIDTH), row_blk),
            pl.BlockSpec((chunk, GROUP_WIDTH), row_blk),
            pl.BlockSpec((None, None, GROUP_WIDTH, chunk), lambda b, c: (b, c // per_kv, 0, c % per_kv)),
        ],
        out_shape=[jax.ShapeDtypeStruct((T, N_REST), F32),
                   jax.ShapeDtypeStruct((T, 2 * GROUP_WIDTH), BF16),
                   jax.ShapeDtypeStruct((T, GROUP_WIDTH), BF16),
                   jax.ShapeDtypeStruct((B, L // tk, GROUP_WIDTH, tk), BF16)],
        scratch_shapes=[pltpu.VMEM((S5_ROW_GROUPS, chunk // S5_ROW_GROUPS + SUBLANES, GROUP_WIDTH), F32),
                        pltpu.VMEM((chunk, n_state), F32),
                        pltpu.VMEM((SUBLANES, n_state), F32)],
        compiler_params=_params("arbitrary", "arbitrary"),
        name="inproj_s5",
    )(x, g, w_all, rc, rs1, rs2, wb, coef, wc, d, wglu)


def _attn_kernel(lamv_ref, q_ref, k_ref, vt_ref, g_ref, sg_ref, o_ref,
                 qs_ref, s_ref, smax_ref, m_ref, acc_ref, *, tq, tk, n_streams, lam_init):
    i = pl.program_id(2)
    lane = lax.broadcasted_iota(jnp.int32, (tq, LANES), 1)
    for st in range(n_streams):
        q = q_ref[:, st * LANES:(st + 1) * LANES]
        zero = jnp.zeros_like(q)
        qs_ref[st, 0:tq, :] = jnp.where(lane < DA_HEAD_DIM, q, zero)
        qs_ref[st, tq:, :] = jnp.where(lane >= DA_HEAD_DIM, q, zero)
    m_ref[...] = jnp.full(m_ref.shape, NEG_BIG, F32)
    acc_ref[...] = jnp.zeros(acc_ref.shape, F32)
    ones = jnp.ones((2 * SUBLANES, tk), BF16)

    def scores(st, kb, masked=False):
        k = k_ref[pl.ds(pl.multiple_of(kb * tk, tk), tk), st * LANES:(st + 1) * LANES]
        s = _dot_nt(k, qs_ref[st])
        if masked:
            key = lax.broadcasted_iota(jnp.int32, s.shape, 0)
            col = lax.broadcasted_iota(jnp.int32, s.shape, 1)
            qpos = i * tq + jnp.where(col >= tq, col - tq, col)
            s = jnp.where(kb * tk + key <= qpos, s, NEG_BIG)
        s_ref[st] = s
        smax_ref[st] = jnp.broadcast_to(jnp.max(s, axis=0, keepdims=True), smax_ref.shape[1:])

    def softmax_pv(st, kb):
        vt = jnp.concatenate([vt_ref[kb, st * LANES:(st + 1) * LANES, :], ones], axis=0)
        m_prev = m_ref[st]
        m_new = jnp.maximum(m_prev, smax_ref[st])
        alpha = jnp.exp2(m_prev - m_new)
        p = jnp.exp2(s_ref[st] - m_new[0:1, :])
        acc_ref[st] = alpha[0:1, :] * acc_ref[st] + _dot(vt, p.astype(BF16))
        m_ref[st] = m_new

    last = n_streams - 1
    scores(0, i, masked=True)
    for st in range(n_streams):
        if st < last:
            scores(st + 1, i, masked=True)
        else:
            scores(0, 0)
        softmax_pv(st, i)

    def step(kb):
        for st in range(n_streams):
            if st < last:
                scores(st + 1, kb)
            else:
                scores(0, kb + 1)
            softmax_pv(st, kb)

    def unrolled_steps(jj, carry):
        for n in range(ATTN_UNROLL):
            step(ATTN_UNROLL * jj + n)
        return carry

    def single_step(kb, carry):
        step(kb)
        return carry

    n_main = lax.div(i, ATTN_UNROLL)
    lax.fori_loop(0, n_main, unrolled_steps, 0)
    lax.fori_loop(n_main * ATTN_UNROLL, i, single_step, 0)

    lv = lamv_ref[...]
    lam = (jnp.exp(jnp.sum(lv[0:1, :] * lv[1:2, :], axis=-1, keepdims=True))
           - jnp.exp(jnp.sum(lv[2:3, :] * lv[3:4, :], axis=-1, keepdims=True)) + lam_init)
    for st in range(n_streams):
        acc = acc_ref[st]
        inv_l = 1.0 / acc[LANES:LANES + 1, :]
        ot = acc[:LANES, :tq] * inv_l[:, :tq] - lam * (acc[:LANES, tq:] * inv_l[:, tq:])
        o = _rms(ot.T, sg_ref[...]) * (1.0 - lam_init)
        cols = slice(st * LANES, (st + 1) * LANES)
        o_ref[:, cols] = (o * _silu(g_ref[:, cols])).astype(BF16)


def _attn(qk_rot, vt, rest, lamv, sg, layer, B, L, tq, tk):
    T = B * L
    nq = L // tq
    assert tq == tk, "the causal block bookkeeping assumes square score blocks"
    lam_init = 0.8 - 0.6 * math.exp(-0.3 * layer)
    ns = ATTN_STREAMS
    width = ns * LANES
    n_grp = GROUP_WIDTH // width
    gate_blk = COL_BG * n_grp
    resident = pl.Buffered(1)
    return pl.pallas_call(
        functools.partial(_attn_kernel, tq=tq, tk=tk, n_streams=ns, lam_init=lam_init),
        grid=(B, n_grp, nq),
        in_specs=[
            pl.BlockSpec((None, 4, DA_HEAD_DIM), lambda b, h, i: (layer, 0, 0)),
            pl.BlockSpec((tq, width), lambda b, h, i: (b * nq + i, h)),
            pl.BlockSpec((L, width), lambda b, h, i: (b, n_grp + h), pipeline_mode=resident),
            pl.BlockSpec((None, L // tk, width, tk), lambda b, h, i: (b, 0, h, 0), pipeline_mode=resident),
            pl.BlockSpec((tq, width), lambda b, h, i: (b * nq + i, gate_blk + h)),
            pl.BlockSpec((None, 1, LANES), lambda b, h, i: (layer, 0, 0)),
        ],
        out_specs=pl.BlockSpec((tq, width), lambda b, h, i: (b * nq + i, h)),
        out_shape=jax.ShapeDtypeStruct((T, GROUP_WIDTH), BF16),
        scratch_shapes=[
            pltpu.VMEM((ns, 2 * tq, LANES), BF16),
            pltpu.VMEM((ns, tk, 2 * tq), F32),
            pltpu.VMEM((ns, SUBLANES, 2 * tq), F32),
            pltpu.VMEM((ns, SUBLANES, 2 * tq), F32),
            pltpu.VMEM((ns, LANES + 2 * SUBLANES, 2 * tq), F32),
        ],
        compiler_params=_params("parallel", "parallel", "arbitrary"),
        name="diffattn",
    )(lamv, qk_rot, qk_rot, vt, rest, sg)


def _memkv_kernel(mem_ref, g_ref, w_ref, o_ref):
    o_ref[...] = _dot(_rms(mem_ref[...], g_ref[...]).astype(BF16), w_ref[...]).astype(BF16)


def _memkv(mem2d, g, w, B, M):
    depth, D, n_kv = w.shape
    return pl.pallas_call(
        _memkv_kernel,
        grid=(depth, B),
        in_specs=[
            pl.BlockSpec((M, D), lambda l, b: (b, 0)),
            pl.BlockSpec((None, 1, D), lambda l, b: (l, 0, 0)),
            pl.BlockSpec((None, D, n_kv), lambda l, b: (l, 0, 0)),
        ],
        out_specs=pl.BlockSpec((None, M, n_kv), lambda l, b: (l, b, 0)),
        out_shape=jax.ShapeDtypeStruct((depth, B * M, n_kv), BF16),
        compiler_params=_params("parallel", "parallel"),
        name="memkv",
    )(mem2d, g, w)


def _lru_gates(x, cw_ref, cb_ref, wg_ref, bg_ref, xe_ref, row0, n_rows):
    base = SUBLANES + row0
    xe_ref[base:base + n_rows, :] = x
    xc = (cw_ref[3:4, :] * x
          + cw_ref[2:3, :] * xe_ref[base - 1:base - 1 + n_rows, :]
          + cw_ref[1:2, :] * xe_ref[base - 2:base - 2 + n_rows, :]
          + cw_ref[0:1, :] * xe_ref[base - 3:base - 3 + n_rows, :]
          + cb_ref[...])
    return xc, _dot(xc.astype(BF16), wg_ref[...]) + bg_ref[...]


def _lru_scan(xc, z, gate, lam_ref, a_ref, b_ref, h_ref, carry_ref, chunk):
    r = _sigmoid(z[:, :GROUP_WIDTH])
    gate_i = _sigmoid(z[:, GROUP_WIDTH:])
    nl = -lam_ref[...]
    softplus = jnp.maximum(nl, 0.0) + jnp.log1p(jnp.exp(-jnp.abs(nl)))
    log_a = (-LRU_C) * r * softplus
    a = jnp.exp(log_a)
    a_ref[...] = a
    mult = jnp.sqrt(-jnp.tanh(log_a) * (1.0 + a * a))
    b_ref[...] = mult * (gate_i * xc)

    row = lax.broadcasted_iota(jnp.int32, (SUBLANES, LANES), 0)
    for t in range(chunk // SUBLANES):
        rows = slice(t * SUBLANES, (t + 1) * SUBLANES)
        for j in range(GROUP_WIDTH // LANES):
            cols = slice(j * LANES, (j + 1) * LANES)
            a = a_ref[rows, cols]
            b = b_ref[rows, cols] + jnp.where(row == 0, a, 0.0) * carry_ref[:, cols]
            w = jnp.where(row == 0, 0.0, a)
            for shift in (1, 2, 4):
                b = b + w * pltpu.roll(b, shift, axis=0)
                if shift != 4:
                    w = w * pltpu.roll(w, shift, axis=0)
            h_ref[rows, cols] = b
            carry_ref[:, cols] = jnp.broadcast_to(b[SUBLANES - 1:, :], (SUBLANES, LANES))
    return h_ref[...] * _silu(gate)


def _memattn_body(q, gate, kv_ref):
    qb = q.astype(BF16)
    scale = MEM_HEAD_DIM ** -0.5 * LOG2E
    outs = []
    for h in range(MEM_HEADS):
        cols = slice(h * MEM_HEAD_DIM, (h + 1) * MEM_HEAD_DIM)
        k = kv_ref[:, h * MEM_HEAD_DIM:(h + 1) * MEM_HEAD_DIM]
        v = kv_ref[:, GROUP_WIDTH + h * MEM_HEAD_DIM:GROUP_WIDTH + (h + 1) * MEM_HEAD_DIM]
        s = _dot_nt(qb[:, cols], k) * scale
        e = jnp.exp2(s - jnp.max(s, axis=-1, keepdims=True))
        outs.append(_dot(e.astype(BF16), v) / jnp.sum(e, axis=-1, keepdims=True))
    return jnp.concatenate(outs, axis=1) * _silu(gate)


def _tail_kernel(cx_ref, cg_ref, mq_ref, mg_ref, ya_ref, yb_ref, x_ref, kv_ref,
                 cw_ref, cb_ref, wg_ref, bg_ref, lam_ref, w_ref, fg_ref, o_ref,
                 xe_ref, a_ref, b_ref, h_ref, carry_ref, *, chunk, n_sub, final):
    c = pl.program_id(1)

    @pl.when(c == 0)
    def _():
        xe_ref[0:SUBLANES, :] = jnp.zeros((SUBLANES, GROUP_WIDTH), F32)
        carry_ref[...] = jnp.zeros(carry_ref.shape, F32)

    @pl.when(c > 0)
    def _():
        xe_ref[0:SUBLANES, :] = xe_ref[chunk:chunk + SUBLANES, :]

    gw = GROUP_WIDTH
    sub = chunk // n_sub

    def front(h):
        rows = pl.ds(h * sub, sub)
        xc, z = _lru_gates(cx_ref[rows, :], cw_ref, cb_ref, wg_ref, bg_ref, xe_ref, h * sub, sub)
        acc = (x_ref[rows, :] + _dot(ya_ref[rows, :], w_ref[0:gw, :])
               + _dot(yb_ref[rows, :], w_ref[gw:2 * gw, :]))
        y_m = _memattn_body(mq_ref[rows, :], mg_ref[rows, :], kv_ref)
        return xc, z, acc + _dot(y_m.astype(BF16), w_ref[3 * gw:4 * gw, :])

    fronts = [front(h) for h in range(n_sub)]
    for h, (xc, z, acc) in enumerate(fronts):
        rows = pl.ds(h * sub, sub)
        y_c = _lru_scan(xc, z, cg_ref[rows, :], lam_ref, a_ref.at[rows], b_ref.at[rows],
                        h_ref.at[rows], carry_ref, sub)
        acc = acc + _dot(y_c.astype(BF16), w_ref[2 * gw:3 * gw, :])
        o_ref[rows, :] = _rms(acc, fg_ref[...]) if final else acc


def _tail(rest, y_a, y_b, x, kv, cw, cb, wg, bg, lam, w_all, fg, layer, final, B, L, M, chunk):
    T, D = x.shape
    nc = L // chunk
    gw = GROUP_WIDTH

    def row_blk(b, c):
        return (b * nc + c, 0)

    def rest_spec(col):
        return pl.BlockSpec((chunk, gw), lambda b, c: (b * nc + c, col))

    def per_layer(shape):
        return pl.BlockSpec((None,) + shape, lambda b, c: (layer,) + (0,) * len(shape))

    return pl.pallas_call(
        functools.partial(_tail_kernel, chunk=chunk, n_sub=TAIL_ROW_GROUPS, final=final),
        grid=(B, nc),
        in_specs=[
            rest_spec(COL_CX), rest_spec(COL_CG), rest_spec(COL_MQ), rest_spec(COL_MG),
            pl.BlockSpec((chunk, gw), row_blk),
            pl.BlockSpec((chunk, gw), row_blk),
            pl.BlockSpec((chunk, D), row_blk),
            pl.BlockSpec((None, M, 2 * gw), lambda b, c: (layer, b, 0)),
            per_layer((4, gw)), per_layer((1, gw)), per_layer((gw, 2 * gw)), per_layer((1, 2 * gw)),
            per_layer((1, gw)), per_layer(w_all.shape[1:]),
            pl.BlockSpec((1, D), lambda b, c: (0, 0)),
        ],
        out_specs=pl.BlockSpec((chunk, D), row_blk),
        out_shape=jax.ShapeDtypeStruct((T, D), F32),
        scratch_shapes=[
            pltpu.VMEM((chunk + SUBLANES, gw), F32),
            pltpu.VMEM((chunk, gw), F32),
            pltpu.VMEM((chunk, gw), F32),
            pltpu.VMEM((chunk, gw), F32),
            pltpu.VMEM((SUBLANES, gw), F32),
        ],
        compiler_params=_params("arbitrary", "arbitrary"),
        name="tail",
    )(rest, rest, rest, rest, y_a, y_b, x, kv, cw, cb, wg, bg, lam, w_all, fg)


def _block_diag(blocks):
    n = blocks.shape[-3]
    eye = jnp.eye(n, dtype=blocks.dtype)
    out = jnp.einsum('...nrc,nm->...nrmc', blocks, eye)
    return out.reshape(blocks.shape[:-3] + (n * blocks.shape[-2], n * blocks.shape[-1]))


def _s5_prep(lam_re, lam_im, log_dt, b_re, b_im, c_re, c_im):
    depth = lam_re.shape[0]
    dt = jnp.exp(log_dt)[..., None]
    mag = jnp.exp(lam_re * dt)
    abar_re = mag * jnp.cos(lam_im * dt)
    abar_im = mag * jnp.sin(lam_im * dt)
    den = lam_re * lam_re + lam_im * lam_im
    nr, ni = abar_re - 1.0, abar_im
    f_re = (nr * lam_re + ni * lam_im) / den
    f_im = (ni * lam_re - nr * lam_im) / den
    bb_re = f_re[..., None] * b_re - f_im[..., None] * b_im
    bb_im = f_re[..., None] * b_im + f_im[..., None] * b_re
    gpb = LANES // S5_CH

    def in_blocks(bb):
        return _block_diag(jnp.swapaxes(bb, -1, -2).reshape(depth, S5_BLOCKS, gpb, S5_CH, S5_STATE))

    def out_blocks(cc):
        return _block_diag(jnp.swapaxes(cc, -1, -2).reshape(depth, S5_BLOCKS, gpb, S5_STATE, S5_CH))

    pr, pi = jnp.ones_like(abar_re), jnp.zeros_like(abar_im)
    wb_d, powers = [], []
    for _ in range(S5_UNROLL):
        wd_re = pr[..., None] * bb_re - pi[..., None] * bb_im
        wd_im = pr[..., None] * bb_im + pi[..., None] * bb_re
        wb_d.append(jnp.concatenate([in_blocks(wd_re), in_blocks(wd_im)], axis=-1))
        pr, pi = pr * abar_re - pi * abar_im, pr * abar_im + pi * abar_re
        powers.append((pr.reshape(depth, 1, -1), pi.reshape(depth, 1, -1)))
    wb = jnp.concatenate(wb_d, axis=-2).astype(BF16)
    wc = jnp.concatenate([out_blocks(c_re), out_blocks(-c_im)], axis=-2).astype(BF16)

    zero = jnp.zeros_like(powers[0][0])
    pad = [zero] * (SUBLANES - S5_UNROLL)
    carry_re = jnp.concatenate([p[0] for p in powers] + pad, axis=1)
    carry_im = jnp.concatenate([p[1] for p in powers] + pad, axis=1)
    step_re = jnp.concatenate([zero] * S5_UNROLL + [powers[-1][0]] * (SUBLANES - S5_UNROLL), axis=1)
    step_im = jnp.concatenate([zero] * S5_UNROLL + [powers[-1][1]] * (SUBLANES - S5_UNROLL), axis=1)
    coef = jnp.stack([carry_re, carry_im, step_re, step_im], axis=1)
    return wb, wc, coef.astype(F32)


def _rope_tables(positions):
    inv_freq = ROPE_THETA ** (-jnp.arange(0, ROPE_DIMS, 2, dtype=F32) / ROPE_DIMS)
    ang = positions.astype(F32).reshape(-1, 1) * inv_freq
    cos, sin = jnp.cos(ang), jnp.sin(ang)
    half = ROPE_DIMS // 2
    pad = DA_HEAD_DIM - ROPE_DIMS
    n = cos.shape[0]
    ones, zeros = jnp.ones((n, pad), F32), jnp.zeros((n, pad), F32)
    zh = jnp.zeros((n, half), F32)
    rc = jnp.concatenate([cos, cos, ones], axis=1)
    rs1 = jnp.concatenate([-sin, zh, zeros], axis=1)
    rs2 = jnp.concatenate([zh, sin, zeros], axis=1)
    rep = LANES // DA_HEAD_DIM
    return jnp.tile(rc, (1, rep)), jnp.tile(rs1, (1, rep)), jnp.tile(rs2, (1, rep))


def _tiles(L):
    return dict(chunk=min(512, L), tq=min(512, L), tk=min(512, L))


def kernel(x, mem, positions, norm_g, w_in, w_out, s5_lambda_re, s5_lambda_im, s5_log_dt, s5_b_re, s5_b_im, s5_c_re, s5_c_im, s5_d, s5_w_glu, da_lambda_q1, da_lambda_k1, da_lambda_q2, da_lambda_k2, da_subln_g, lru_conv_w, lru_conv_b, lru_w_a, lru_b_a, lru_w_x, lru_b_x, lru_lambda, mem_norm_g, w_mem_kv, final_norm_g):
    B, L, D = x.shape
    M = mem.shape[1]
    depth = w_in.shape[0]
    T = B * L
    t = _tiles(L)
    assert L % t["tk"] == 0 and L % t["chunk"] == 0 and t["tk"] % t["chunk"] == 0
    assert t["chunk"] % (max(S5_ROW_GROUPS, TAIL_ROW_GROUPS) * LANES) == 0
    assert SUBLANES == 2 * S5_UNROLL and GROUP_WIDTH % (ATTN_STREAMS * LANES) == 0

    gw = GROUP_WIDTH
    w_in_b = w_in.astype(BF16)
    w_out_b = w_out.astype(BF16)
    wglu_b = s5_w_glu.astype(BF16)
    wb, wc, coef = _s5_prep(s5_lambda_re, s5_lambda_im, s5_log_dt, s5_b_re, s5_b_im, s5_c_re, s5_c_im)
    s5_d2 = s5_d.reshape(depth, 1, gw)
    lamv = jnp.stack([da_lambda_q1, da_lambda_k1, da_lambda_q2, da_lambda_k2], axis=1)
    sg = da_subln_g.reshape(depth, 1, LANES)
    wg = jnp.concatenate([_block_diag(lru_w_a), _block_diag(lru_w_x)], axis=-1).astype(BF16)
    bg = jnp.concatenate([lru_b_a, lru_b_x], axis=-1).reshape(depth, 1, 2 * gw)
    lru_cb = lru_conv_b.reshape(depth, 1, gw)
    lru_lam = lru_lambda.reshape(depth, 1, gw)
    norm_g3 = norm_g.reshape(depth, 1, D)
    rc, rs1, rs2 = _rope_tables(positions)

    kv = _memkv(mem.reshape(B * M, D), mem_norm_g.reshape(depth, 1, D), w_mem_kv.astype(BF16), B, M)

    xf = x.reshape(T, D)
    fg = final_norm_g.reshape(1, D)
    for layer in range(depth):
        rest, qk_rot, y_a, vt = _inproj_s5(xf, norm_g3, w_in_b, layer, rc, rs1, rs2,
                                        wb, coef, wc, s5_d2, wglu_b, B, L, t["chunk"], t["tk"])
        y_b = _attn(qk_rot, vt, rest, lamv, sg, layer, B, L, t["tq"], t["tk"])
        xf = _tail(rest, y_a, y_b, xf, kv, lru_conv_w, lru_cb, wg, bg, lru_lam, w_out_b, fg,
                   layer, layer == depth - 1, B, L, M, t["chunk"])
    return xf.reshape(B, L, D)
```
